```python
import jax
import jax.numpy as jnp
from jax import lax
import numpy as np

D_MODEL = 1024
BATCH = 16
SEQ = 256
DEPTH = 2
DEC_BATCH = 8
DEC_SEQ = 2048
PAST_LEN = 512

GRID_W = 64
MIX_W = D_MODEL
NA_HEADS = 8
NA_HEAD_DIM = 64
NA_W = NA_HEADS * NA_HEAD_DIM
SC_W = MIX_W - NA_W
NA_SCALE = NA_HEAD_DIM ** -0.5
MAX_WR = 8
WIN_W = 16
QB_W = 16
KB_W = QB_W + WIN_W
CONV_K = 3
FN_W = MIX_W // 2
FN_GROUPS = 4
FN_GROUP_W = FN_W // FN_GROUPS
HG_W = MIX_W - FN_W
HG_HEADS = 4
HG_DK = HG_W // HG_HEADS
HG_CHUNK = 64
N_AB = (DEPTH + 1) // 2
N_CD = DEPTH // 2
AB_IN = 3 * NA_W + 3 * SC_W
CD_IN = FN_W + 5 * HG_W
N_EXPERTS = 32
TOP_K = 4
D_FF = D_MODEL
SWIGLU_LIMIT = 7.0
SWIGLU_ALPHA = 1.702
MOE_BLOCK = 128
ATTN_QBLK = 128
EPS = 1e-6
NEG_INF = -1e30

kernel_name = 'hybrid_prefix_diffusion_natten_conv_fnet_hgrn2_moe'


def rmsnorm(x, g):
    xf = x.astype(jnp.float32)
    y = xf * lax.rsqrt(jnp.mean(xf * xf, axis=-1, keepdims=True) + EPS)
    return (y * g.astype(jnp.float32)).astype(x.dtype)


def modulation(m, w, b):
    mod = jax.nn.silu(m) @ w + b
    return [t[:, None, :] for t in jnp.split(mod, 6, axis=-1)]


def heads(t, n):
    b, s, w = t.shape
    return t.reshape(b, s, n, w // n).transpose(0, 2, 1, 3)


def merge_heads(t):
    b, h, s, d = t.shape
    return t.transpose(0, 2, 1, 3).reshape(b, s, h * d)


def context_attention(q, k, v):
    b, h, s, d = q.shape
    nb = s // ATTN_QBLK
    qb = q.reshape(b, h, nb, ATTN_QBLK, d).transpose(2, 0, 1, 3, 4)

    def one_block(qblk):
        sc = jnp.einsum('bhqd,bhkd->bhqk', qblk, k).astype(jnp.float32) * NA_SCALE
        p = jax.nn.softmax(sc, axis=-1).astype(v.dtype)
        return jnp.einsum('bhqk,bhkd->bhqd', p, v)

    o = lax.map(one_block, qb)
    return o.transpose(1, 2, 0, 3, 4).reshape(b, h, s, d)


def neighbourhood_attention(q, k, v, ck, cv, rpb):
    b, h, t, d = q.shape
    rows = t // GRID_W
    wr = min(MAX_WR, rows)
    ncb = GRID_W // QB_W
    r = jnp.arange(rows)
    row_idx = jnp.clip(r - wr // 2, 0, rows - wr)[:, None] + jnp.arange(wr)[None, :]
    cb = jnp.arange(ncb)
    col_idx = jnp.clip(cb * QB_W - WIN_W // 2, 0, GRID_W - KB_W)[:, None] + jnp.arange(KB_W)[None, :]
    kg = k.reshape(b, h, rows, GRID_W, d)
    vg = v.reshape(b, h, rows, GRID_W, d)
    ri = row_idx[:, None, :, None]
    ci = col_idx[None, :, None, :]
    nk = wr * KB_W
    kn = kg[:, :, ri, ci, :].reshape(b, h, rows, ncb, nk, d)
    vn = vg[:, :, ri, ci, :].reshape(b, h, rows, ncb, nk, d)
    qb = q.reshape(b, h, rows, ncb, QB_W, d)
    qc = cb[:, None] * QB_W + jnp.arange(QB_W)[None, :]
    ws = jnp.clip(qc - WIN_W // 2, 0, GRID_W - WIN_W)
    kc = col_idx[:, None, :]
    valid = (kc >= ws[:, :, None]) & (kc < ws[:, :, None] + WIN_W)
    dc_idx = jnp.clip(kc - qc[:, :, None] + WIN_W - 1, 0, 2 * WIN_W - 2)
    dr_idx = row_idx - r[:, None] + MAX_WR - 1
    bias = rpb[:, dr_idx[:, None, None, :, None], dc_idx[None, :, :, None, :]].astype(jnp.float32)
    bias = jnp.where(valid[None, None, :, :, None, :], bias, NEG_INF).reshape(h, rows, ncb, QB_W, nk)
    s_loc = jnp.einsum('bhrcqd,bhrckd->bhrcqk', qb, kn).astype(jnp.float32) * NA_SCALE + bias[None]
    s_ctx = jnp.einsum('bhrcqd,bhld->bhrcql', qb, ck).astype(jnp.float32) * NA_SCALE
    p = jax.nn.softmax(jnp.concatenate([s_loc, s_ctx], axis=-1), axis=-1).astype(v.dtype)
    o = (jnp.einsum('bhrcqk,bhrckd->bhrcqd', p[..., :nk], vn)
         + jnp.einsum('bhrcql,bhld->bhrcqd', p[..., nk:], cv))
    return o.reshape(b, h, t, d)


def short_conv(z, w):
    zp = jnp.pad(z, ((0, 0), (1, 1), (0, 0)))
    return zp[:, :-2] * w[0] + zp[:, 1:-1] * w[1] + zp[:, 2:] * w[2]


def fourier_mix(z):
    b, t, _ = z.shape
    zg = z.reshape(b, t, FN_GROUPS, FN_GROUP_W).astype(jnp.float32)
    f = jnp.fft.fft2(zg, axes=(1, 3), norm='ortho').real
    return f.reshape(b, t, FN_W).astype(z.dtype)


def gla_chunk_scan(q, k, v, logf, s0):
    b, h, t, _ = q.shape
    n = t // HG_CHUNK

    def chunks(a):
        return a.reshape(b, h, n, HG_CHUNK, a.shape[-1]).transpose(2, 0, 1, 3, 4)

    tri = jnp.tril(jnp.ones((HG_CHUNK, HG_CHUNK), dtype=bool))[:, :, None]

    def step(s, inp):
        qc, kc, vc, lf = inp
        bcum = jnp.cumsum(lf, axis=-2)
        o_inter = jnp.einsum('bhtk,bhkv->bhtv', qc * jnp.exp(bcum), s)
        diff = bcum[:, :, :, None, :] - bcum[:, :, None, :, :]
        decay = jnp.where(tri, jnp.exp(jnp.where(tri, diff, 0.0)), 0.0)
        att = jnp.einsum('bhtk,bhtsk,bhsk->bhts', qc, decay, kc)
        o = o_inter + jnp.einsum('bhts,bhsv->bhtv', att, vc)
        b_last = bcum[:, :, -1:, :]
        s_new = (jnp.exp(b_last[:, :, 0, :])[..., None] * s
                 + jnp.einsum('bhsk,bhsv->bhkv', kc * jnp.exp(b_last - bcum), vc))
        return s_new, o

    s_fin, o = lax.scan(step, s0, (chunks(q), chunks(k), chunks(v), chunks(logf)))
    return o.transpose(1, 2, 0, 3, 4).reshape(b, h, t, -1), s_fin


def hgrn2_bidir(q, i, zf, zb, g, lb, s0f, s0b):
    dt = q.dtype
    qh = heads(q, HG_HEADS).astype(jnp.float32)
    ih = heads(i, HG_HEADS).astype(jnp.float32)

    def gates(z, lbd):
        lbh = lbd.reshape(HG_HEADS, 1, HG_DK)
        zh = heads(z, HG_HEADS).astype(jnp.float32)
        f = lbh + (1.0 - lbh) * jax.nn.sigmoid(zh)
        return jnp.log(f), (1.0 - lbh) * jax.nn.sigmoid(-zh)

    lf_f, k_f = gates(zf, lb[0])
    lf_b, k_b = gates(zb, lb[1])
    o_f, s_f = gla_chunk_scan(qh, k_f, ih, lf_f, s0f)
    fl = lambda a: jnp.flip(a, axis=2)
    o_b, s_b = gla_chunk_scan(fl(qh), fl(k_b), fl(ih), fl(lf_b), s0b)
    o = o_f + fl(o_b)
    o = o * lax.rsqrt(jnp.mean(o * o, axis=-1, keepdims=True) + EPS)
    return merge_heads(o).astype(dt) * jax.nn.silu(g), s_f, s_b


def moe_ffn(h, router_w, router_b, w1, b1, w2, b2):
    bsz, t, d = h.shape
    x = h.reshape(bsz * t, d)
    n_tok = bsz * t
    logits = (x @ router_w).astype(jnp.float32) + router_b.astype(jnp.float32)
    top_v, top_i = lax.top_k(logits, TOP_K)
    gates = jax.nn.softmax(top_v, axis=-1)
    n_asg = n_tok * TOP_K
    e_flat = top_i.reshape(-1)
    order = jnp.argsort(e_flat)
    e_sorted = e_flat[order]
    tok_sorted = order // TOP_K
    gate_sorted = gates.reshape(-1)[order]
    counts = jnp.zeros((N_EXPERTS,), jnp.int32).at[e_flat].add(1)
    padded = (counts + MOE_BLOCK - 1) // MOE_BLOCK * MOE_BLOCK
    pad_end = jnp.cumsum(padded)
    pad_start = pad_end - padded
    grp_start = jnp.cumsum(counts) - counts
    dest = pad_start[e_sorted] + jnp.arange(n_asg) - grp_start[e_sorted]
    n_blk = n_asg // MOE_BLOCK + N_EXPERTS
    n_slot = n_blk * MOE_BLOCK
    slot_tok = jnp.zeros((n_slot,), jnp.int32).at[dest].set(tok_sorted)
    slot_gate = jnp.zeros((n_slot,), jnp.float32).at[dest].set(gate_sorted)
    blk_expert = jnp.minimum(jnp.searchsorted(pad_end, jnp.arange(n_blk) * MOE_BLOCK, side='right'), N_EXPERTS - 1)
    xb = x[slot_tok].reshape(n_blk, MOE_BLOCK, d)

    def expert_block(args):
        xe, e = args
        u = xe @ w1[e] + b1[e]
        glu = jnp.minimum(u[:, :D_FF], SWIGLU_LIMIT)
        lin = jnp.clip(u[:, D_FF:], -SWIGLU_LIMIT, SWIGLU_LIMIT)
        a = glu * jax.nn.sigmoid(SWIGLU_ALPHA * glu) * (lin + 1.0)
        return a @ w2[e] + b2[e]

    yb = lax.map(expert_block, (xb, blk_expert)).reshape(n_slot, d)
    y = jax.ops.segment_sum(yb * slot_gate[:, None].astype(yb.dtype), slot_tok, num_segments=n_tok)
    return y.reshape(bsz, t, d)


def ab_project(h, w_in):
    p = h @ w_in
    q, k, v, bg, cg, hin = jnp.split(p, [NA_W, 2 * NA_W, 3 * NA_W, 3 * NA_W + SC_W, 3 * NA_W + 2 * SC_W], axis=-1)
    return heads(q, NA_HEADS), heads(k, NA_HEADS), heads(v, NA_HEADS), bg, cg, hin


def cd_project(h, w_in):
    p = h @ w_in
    return jnp.split(p, [FN_W, FN_W + HG_W, FN_W + 2 * HG_W, FN_W + 3 * HG_W, FN_W + 4 * HG_W], axis=-1)


def setup_inputs(seed: int = 0) -> dict:
    key = jax.random.key(seed)
    ks = iter(jax.random.split(key, 32))
    D = D_MODEL

    def nrm(shape, s):
        return jax.random.normal(next(ks), shape, jnp.float32) * s

    return {
        'x_prompt': nrm((BATCH, SEQ, D), 1.0),
        'x_sample': nrm((DEC_BATCH, DEC_SEQ, D), 1.0),
        'cache_attn_k': nrm((DEC_BATCH, N_AB, NA_HEADS, PAST_LEN, NA_HEAD_DIM), 1.0),
        'cache_attn_v': nrm((DEC_BATCH, N_AB, NA_HEADS, PAST_LEN, NA_HEAD_DIM), 1.0),
        'state_hgrn': nrm((DEC_BATCH, N_CD, 2, HG_HEADS, HG_DK, HG_DK), 0.5),
        'c': nrm((DEC_BATCH, D), 1.0),
        'c_ctx': nrm((D,), 1.0),
        'ada_w': nrm((DEPTH, D, 6 * D), 0.5 * D ** -0.5),
        'ada_b': nrm((DEPTH, 6 * D), 0.02),
        'norm1_g': 1.0 + nrm((DEPTH, D), 0.02),
        'norm2_g': 1.0 + nrm((DEPTH, D), 0.02),
        'final_g': 1.0 + nrm((D,), 0.02),
        'ab_w_in': nrm((N_AB, D, AB_IN), D ** -0.5),
        'ab_w_out': nrm((N_AB, MIX_W, D), MIX_W ** -0.5),
        'na_rpb': nrm((N_AB, NA_HEADS, 2 * MAX_WR - 1, 2 * WIN_W - 1), 0.1),
        'sc_conv_w': nrm((N_AB, CONV_K, SC_W), CONV_K ** -0.5),
        'cd_w_in': nrm((N_CD, D, CD_IN), D ** -0.5),
        'cd_w_out': nrm((N_CD, MIX_W, D), MIX_W ** -0.5),
        'hg_lb_logits': nrm((DEPTH, 2, HG_W), 1.0),
        'router_w': nrm((DEPTH, D, N_EXPERTS), D ** -0.5),
        'router_b': nrm((DEPTH, N_EXPERTS), 0.01),
        'moe_w1': nrm((DEPTH, N_EXPERTS, D, 2 * D_FF), D ** -0.5),
        'moe_b1': nrm((DEPTH, N_EXPERTS, 2 * D_FF), 0.01),
        'moe_w2': nrm((DEPTH, N_EXPERTS, D_FF, D), D_FF ** -0.5),
        'moe_b2': nrm((DEPTH, N_EXPERTS, D), 0.01),
    }


def reference(x_prompt, x_sample, cache_attn_k, cache_attn_v, state_hgrn, c, c_ctx,
              ada_w, ada_b, norm1_g, norm2_g, final_g,
              ab_w_in, ab_w_out, na_rpb, sc_conv_w,
              cd_w_in, cd_w_out, hg_lb_logits,
              router_w, router_b, moe_w1, moe_b1, moe_w2, moe_b2):
    f32 = jnp.float32
    lb_sm = jax.nn.softmax(hg_lb_logits.astype(f32), axis=0)
    lb_all = jnp.cumsum(lb_sm, axis=0) - lb_sm[0:1]
    xp, xs = x_prompt, x_sample
    new_k, new_v, new_s = [], [], []
    for l in range(DEPTH):
        mp = modulation(c_ctx[None, :], ada_w[l], ada_b[l])
        ms = modulation(c, ada_w[l], ada_b[l])
        hp = rmsnorm(xp, norm1_g[l]) * (1.0 + mp[1]) + mp[0]
        hs = rmsnorm(xs, norm1_g[l]) * (1.0 + ms[1]) + ms[0]
        j = l // 2
        if l % 2 == 0:
            qp, kp, vp, bgp, cgp, hinp = ab_project(hp, ab_w_in[j])
            qs, ks_, vs, bgs, cgs, hins = ab_project(hs, ab_w_in[j])
            att_p = context_attention(qp, kp, vp)
            att_s = neighbourhood_attention(qs, ks_, vs, cache_attn_k[:, j], cache_attn_v[:, j], na_rpb[j])
            conv_p = bgp * short_conv(cgp * hinp, sc_conv_w[j])
            conv_s = bgs * short_conv(cgs * hins, sc_conv_w[j])
            mix_p = jnp.concatenate([merge_heads(att_p), conv_p], axis=-1) @ ab_w_out[j]
            mix_s = jnp.concatenate([merge_heads(att_s), conv_s], axis=-1) @ ab_w_out[j]
            new_k.append(kp)
            new_v.append(vp)
        else:
            fn_p, q_p, i_p, zf_p, zb_p, g_p = cd_project(hp, cd_w_in[j])
            fn_s, q_s, i_s, zf_s, zb_s, g_s = cd_project(hs, cd_w_in[j])
            s_zero = jnp.zeros((xp.shape[0], HG_HEADS, HG_DK, HG_DK), f32)
            hg_p, sf_p, sb_p = hgrn2_bidir(q_p, i_p, zf_p, zb_p, g_p, lb_all[l], s_zero, s_zero)
            hg_s, _, _ = hgrn2_bidir(q_s, i_s, zf_s, zb_s, g_s, lb_all[l],
                                     state_hgrn[:, j, 0].astype(f32), state_hgrn[:, j, 1].astype(f32))
            mix_p = jnp.concatenate([fourier_mix(fn_p), hg_p], axis=-1) @ cd_w_out[j]
            mix_s = jnp.concatenate([fourier_mix(fn_s), hg_s], axis=-1) @ cd_w_out[j]
            new_s.append(jnp.stack([sf_p, sb_p], axis=1).astype(xp.dtype))
        xp = xp + mp[2] * mix_p
        xs = xs + ms[2] * mix_s
        h2p = rmsnorm(xp, norm2_g[l]) * (1.0 + mp[4]) + mp[3]
        h2s = rmsnorm(xs, norm2_g[l]) * (1.0 + ms[4]) + ms[3]
        xp = xp + mp[5] * moe_ffn(h2p, router_w[l], router_b[l], moe_w1[l], moe_b1[l], moe_w2[l], moe_b2[l])
        xs = xs + ms[5] * moe_ffn(h2s, router_w[l], router_b[l], moe_w1[l], moe_b1[l], moe_w2[l], moe_b2[l])
    y_prompt = rmsnorm(xp, final_g)
    y_sample = rmsnorm(xs, final_g)
    new_cache_attn_k = jnp.stack(new_k, axis=1)
    new_cache_attn_v = jnp.stack(new_v, axis=1)
    new_state_hgrn = jnp.stack(new_s, axis=1)
    return (y_prompt, y_sample, new_cache_attn_k, new_cache_attn_v, new_state_hgrn)
```

```python
import functools

import numpy as np
import jax
import jax.numpy as jnp
from jax import lax
from jax.experimental import pallas as pl
from jax.experimental.pallas import tpu as pltpu

F32 = jnp.float32
BF16 = jnp.bfloat16

D_MODEL = 1024
BATCH = 16
SEQ = 256
DEC_BATCH = 8
DEC_SEQ = 2048
PAST_LEN = 512
DEPTH = 2
GRID_W = 64
GRID_ROWS = DEC_SEQ // GRID_W
NA_HEADS = 8
NA_HEAD_DIM = 64
NA_W = NA_HEADS * NA_HEAD_DIM
SC_W = D_MODEL - NA_W
NA_SCALE = NA_HEAD_DIM ** -0.5
MAX_WR = 8
WIN_W = 16
FN_W = 512
FN_GROUPS = 4
FN_GROUP_W = 128
HG_W = 512
HG_HEADS = 4
HG_DK = 128
HG_CHUNK = 64
HG_SUB = 16
N_EXPERTS = 32
TOP_K = 4
D_FF = D_MODEL
SWIGLU_LIMIT = 7.0
SWIGLU_ALPHA = 1.702
EPS = 1e-6
NEG_INF = -1e30

N_PROMPT = BATCH * SEQ
N_SAMPLE = DEC_BATCH * DEC_SEQ
N_TOK = N_PROMPT + N_SAMPLE
TM = 256
N_TILES = N_TOK // TM
PROMPT_TILES = N_PROMPT // TM
TILES_PER_DEC_SEQ = DEC_SEQ // TM
N_MOD_ROWS = 16
LANES = 128
SUBLANES = 8

MOE_BLK = 256
N_ASG = N_TOK * TOP_K
N_BLK = N_ASG // MOE_BLK + N_EXPERTS
N_SLOT = N_BLK * MOE_BLK
VMEM_LIMIT = 56 * 1024 * 1024


def _mod_row(i):
    return jnp.where(i < PROMPT_TILES, 0, 1 + (i - PROMPT_TILES) // TILES_PER_DEC_SEQ)


def _sigmoid_pair(z):
    e = jnp.exp(-jnp.abs(z))
    r = 1.0 / (1.0 + e)
    er = e * r
    pos = z >= 0
    return jnp.where(pos, r, er), jnp.where(pos, er, r)


def _silu(x):
    return x * _sigmoid_pair(x)[0]


def _rms(x):
    return x * lax.rsqrt(jnp.mean(x * x, axis=-1, keepdims=True) + EPS)


def _dot(a, b):
    return jnp.dot(a, b, preferred_element_type=F32)


def _dot_nt(a, b):
    return lax.dot_general(a, b, (((1,), (1,)), ((), ())), preferred_element_type=F32)


def _dot_tn(a, b):
    return lax.dot_general(a, b, (((0,), (0,)), ((), ())), preferred_element_type=F32)


def _split3(x):
    hi = x.astype(BF16)
    r1 = x - hi.astype(F32)
    mid = r1.astype(BF16)
    lo = (r1 - mid.astype(F32)).astype(BF16)
    return hi, mid, lo


MOD_TN = 1024


def _mod_kernel(m_ref, w_ref, b_ref, o_ref):
    a = _silu(m_ref[...]).astype(BF16)
    o_ref[...] = _dot(a, w_ref[...].astype(BF16)) + b_ref[...]


def _modulation(m16, ada_w, ada_b):
    n_out = 6 * D_MODEL
    out = pl.pallas_call(
        _mod_kernel,
        out_shape=jax.ShapeDtypeStruct((DEPTH, N_MOD_ROWS, n_out), F32),
        grid=(DEPTH, n_out // MOD_TN),
        in_specs=[
            pl.BlockSpec((N_MOD_ROWS, D_MODEL), lambda l, j: (0, 0)),
            pl.BlockSpec((None, D_MODEL, MOD_TN), lambda l, j: (l, 0, j)),
            pl.BlockSpec((None, 1, MOD_TN), lambda l, j: (l, 0, j)),
        ],
        out_specs=pl.BlockSpec((None, N_MOD_ROWS, MOD_TN), lambda l, j: (l, 0, j)),
        compiler_params=pltpu.CompilerParams(
            dimension_semantics=("arbitrary", "arbitrary"), vmem_limit_bytes=VMEM_LIMIT),
    )(m16, ada_w, ada_b.reshape(DEPTH, 1, n_out))
    return out.reshape(DEPTH, N_MOD_ROWS, 6, D_MODEL)


def _pre_kernel(x_ref, mod_ref, g_ref, w_ref, o_ref):
    y = _rms(x_ref[...]) * g_ref[...]
    h = y * (1.0 + mod_ref[1:2, :]) + mod_ref[0:1, :]
    o_ref[...] = _dot(h.astype(BF16), w_ref[...])


def _pre_project(x_all, mod_l, g, w_bf):
    n_out = w_bf.shape[1]
    return pl.pallas_call(
        _pre_kernel,
        out_shape=jax.ShapeDtypeStruct((N_TOK, n_out), F32),
        grid=(N_TILES,),
        in_specs=[
            pl.BlockSpec((TM, D_MODEL), lambda i: (i, 0)),
            pl.BlockSpec((None, 6, D_MODEL), lambda i: (_mod_row(i), 0, 0)),
            pl.BlockSpec((1, D_MODEL), lambda i: (0, 0)),
            pl.BlockSpec((D_MODEL, n_out), lambda i: (0, 0)),
        ],
        out_specs=pl.BlockSpec((TM, n_out), lambda i: (i, 0)),
        compiler_params=pltpu.CompilerParams(
            dimension_semantics=("arbitrary",), vmem_limit_bytes=VMEM_LIMIT),
    )(x_all, mod_l, g.reshape(1, D_MODEL), w_bf)


def _softmax_pv(parts):
    m = None
    for s, _ in parts:
        mi = jnp.max(s, axis=-1, keepdims=True)
        m = mi if m is None else jnp.maximum(m, mi)
    acc = None
    den = None
    for s, v in parts:
        p = jnp.exp(s - m)
        d = jnp.sum(p, axis=-1, keepdims=True)
        o = _dot(p.astype(BF16), v)
        acc = o if acc is None else acc + o
        den = d if den is None else den + d
    return acc / den


def _ctx_attn_kernel(q_ref, k_ref, v_ref, o_ref):
    q = q_ref[...]
    kb = k_ref[...].astype(BF16)
    vb = v_ref[...].astype(BF16)
    lane = lax.broadcasted_iota(jnp.int32, q.shape, 1)
    out = jnp.zeros(q.shape, F32)
    for h in range(2):
        hm = (lane >= NA_HEAD_DIM) if h == 1 else (lane < NA_HEAD_DIM)
        qm = jnp.where(hm, q, 0.0).astype(BF16)
        s = _dot_nt(qm, kb) * NA_SCALE
        out = jnp.where(hm, _softmax_pv([(s, vb)]), out)
    o_ref[...] = out


def _context_attention(p_all):
    n_hp = NA_HEADS // 2
    return pl.pallas_call(
        _ctx_attn_kernel,
        out_shape=jax.ShapeDtypeStruct((N_PROMPT, NA_W), F32),
        grid=(BATCH, n_hp),
        in_specs=[
            pl.BlockSpec((SEQ, LANES), lambda b, hp: (b, hp)),
            pl.BlockSpec((SEQ, LANES), lambda b, hp: (b, n_hp + hp)),
            pl.BlockSpec((SEQ, LANES), lambda b, hp: (b, 2 * n_hp + hp)),
        ],
        out_specs=pl.BlockSpec((SEQ, LANES), lambda b, hp: (b, hp)),
        compiler_params=pltpu.CompilerParams(
            dimension_semantics=("arbitrary", "arbitrary"), vmem_limit_bytes=VMEM_LIMIT),
    )(p_all, p_all, p_all)


N_BIAS_VAR = MAX_WR
WIN_KEYS = MAX_WR * GRID_W


def _natten_kernel(q_ref, k_ref, v_ref, ck_ref, cv_ref, bias_ref, o_ref):
    ckb = ck_ref[...].astype(BF16)
    cvb = cv_ref[...].astype(BF16)
    lane = lax.broadcasted_iota(jnp.int32, (GRID_W, LANES), 1)

    def row(r, carry):
        r0 = jnp.clip(r - MAX_WR // 2, 0, GRID_ROWS - MAX_WR)
        var = r0 - r + (MAX_WR - 1)
        q = q_ref[pl.ds(pl.multiple_of(r * GRID_W, GRID_W), GRID_W), :]
        kw = k_ref[pl.ds(pl.multiple_of(r0 * GRID_W, GRID_W), WIN_KEYS), :].astype(BF16)
        vw = v_ref[pl.ds(pl.multiple_of(r0 * GRID_W, GRID_W), WIN_KEYS), :].astype(BF16)
        out = jnp.zeros((GRID_W, LANES), F32)
        for h in range(2):
            hm = (lane >= NA_HEAD_DIM) if h == 1 else (lane < NA_HEAD_DIM)
            qm = jnp.where(hm, q, 0.0).astype(BF16)
            s_loc = _dot_nt(qm, kw) * NA_SCALE + bias_ref[h, var]
            s_ctx = _dot_nt(qm, ckb) * NA_SCALE
            out = jnp.where(hm, _softmax_pv([(s_loc, vw), (s_ctx, cvb)]), out)
        o_ref[pl.ds(pl.multiple_of(r * GRID_W, GRID_W), GRID_W), :] = out
        return carry

    lax.fori_loop(0, GRID_ROWS, row, 0)


def _natten_bias(rpb):
    qc = np.arange(GRID_W)[:, None]
    kc = np.arange(GRID_W)[None, :]
    ws = np.clip(qc - WIN_W // 2, 0, GRID_W - WIN_W)
    valid = (kc >= ws) & (kc < ws + WIN_W)
    dc = np.clip(kc - qc + WIN_W - 1, 0, 2 * WIN_W - 2)
    t = rpb.astype(F32)[:, :, dc]
    t = jnp.where(jnp.asarray(valid)[None, None], t, NEG_INF)
    dr = np.arange(N_BIAS_VAR)[:, None] + np.arange(MAX_WR)[None, :]
    slab = t[:, dr]
    return slab.transpose(0, 1, 3, 2, 4).reshape(NA_HEADS, N_BIAS_VAR, GRID_W, WIN_KEYS)


def _neighbourhood_attention(p_all, ck, cv, bias):
    n_hp = NA_HEADS // 2
    off = N_PROMPT // DEC_SEQ
    return pl.pallas_call(
        _natten_kernel,
        out_shape=jax.ShapeDtypeStruct((N_SAMPLE, NA_W), F32),
        grid=(DEC_BATCH, n_hp),
        in_specs=[
            pl.BlockSpec((DEC_SEQ, LANES), lambda b, hp: (off + b, hp)),
            pl.BlockSpec((DEC_SEQ, LANES), lambda b, hp: (off + b, n_hp + hp)),
            pl.BlockSpec((DEC_SEQ, LANES), lambda b, hp: (off + b, 2 * n_hp + hp)),
            pl.BlockSpec((None, PAST_LEN, LANES), lambda b, hp: (b, 0, hp)),
            pl.BlockSpec((None, PAST_LEN, LANES), lambda b, hp: (b, 0, hp)),
            pl.BlockSpec((2, N_BIAS_VAR, GRID_W, WIN_KEYS), lambda b, hp: (hp, 0, 0, 0)),
        ],
        out_specs=pl.BlockSpec((DEC_SEQ, LANES), lambda b, hp: (b, hp)),
        compiler_params=pltpu.CompilerParams(
            dimension_semantics=("arbitrary", "arbitrary"), vmem_limit_bytes=VMEM_LIMIT),
    )(p_all, p_all, p_all, ck, cv, bias)


def _dft_tables(t):
    j = np.arange(t, dtype=np.int64)
    ang = 2.0 * np.pi * ((j[:, None] * j[None, :]) % t).astype(np.float64) / t
    return np.cos(ang), np.sin(ang)


def _fourier_kernel(z_ref, lt_ref, rc_ref, rs_ref, o_ref, ab_ref, *, t_len, scale):
    @pl.when(pl.program_id(1) == 0)
    def _():
        z = z_ref[...].astype(BF16)
        for g in range(FN_GROUPS):
            zg = z[:, g * FN_GROUP_W:(g + 1) * FN_GROUP_W]
            ab_ref[0:t_len, g * FN_GROUP_W:(g + 1) * FN_GROUP_W] = _dot(zg, rc_ref[...]).astype(BF16)
            ab_ref[t_len:2 * t_len, g * FN_GROUP_W:(g + 1) * FN_GROUP_W] = _dot(zg, rs_ref[...]).astype(BF16)

    o_ref[...] = _dot(lt_ref[...], ab_ref[...]) * scale


def _fourier_mix(p_all, row_off, n_batch, t_len):
    ct, st = _dft_tables(t_len)
    lt = jnp.asarray(np.concatenate([ct, -st], axis=1), dtype=BF16)
    cc, sc = _dft_tables(FN_GROUP_W)
    rc = jnp.asarray(cc, dtype=BF16)
    rs = jnp.asarray(sc, dtype=BF16)
    tr = min(t_len, 512)
    n_rt = t_len // tr
    off = row_off // t_len
    kern = functools.partial(_fourier_kernel, t_len=t_len, scale=float((t_len * FN_GROUP_W) ** -0.5))
    return pl.pallas_call(
        kern,
        out_shape=jax.ShapeDtypeStruct((n_batch * t_len, FN_W), F32),
        grid=(n_batch, n_rt),
        in_specs=[
            pl.BlockSpec((t_len, FN_W), lambda b, i: (off + b, 0)),
            pl.BlockSpec((tr, 2 * t_len), lambda b, i: (i, 0)),
            pl.BlockSpec((FN_GROUP_W, FN_GROUP_W), lambda b, i: (0, 0)),
            pl.BlockSpec((FN_GROUP_W, FN_GROUP_W), lambda b, i: (0, 0)),
        ],
        out_specs=pl.BlockSpec((tr, FN_W), lambda b, i: (b * n_rt + i, 0)),
        scratch_shapes=[pltpu.VMEM((2 * t_len, FN_W), BF16)],
        compiler_params=pltpu.CompilerParams(
            dimension_semantics=("arbitrary", "arbitrary"), vmem_limit_bytes=VMEM_LIMIT),
    )(p_all, lt, rc, rs)


N_SUB = HG_CHUNK // HG_SUB


def _gla_chunk(q, v, z, lb, st, backward):
    c = HG_CHUNK
    sg, sgn = _sigmoid_pair(z)
    lf = jnp.log(lb + (1.0 - lb) * sg)
    kk = (1.0 - lb) * sgn
    row = lax.broadcasted_iota(jnp.int32, (c, c), 0)
    col = lax.broadcasted_iota(jnp.int32, (c, c), 1)
    tri = ((col >= row) if backward else (col <= row)).astype(BF16)
    hi, mid, lo = _split3(lf)
    b = _dot(tri, hi) + _dot(tri, mid) + _dot(tri, lo)
    o_inter = _dot_nt((q * jnp.exp(b)).astype(BF16), st.astype(BF16))

    sub_row = lax.broadcasted_iota(jnp.int32, (HG_SUB, c), 0)
    sub_col = lax.broadcasted_iota(jnp.int32, (HG_SUB, c), 1)
    att_rows = []
    for i in range(N_SUB):
        lo_r, hi_r = i * HG_SUB, (i + 1) * HG_SUB
        q_i = q[lo_r:hi_r]
        b_i = b[lo_r:hi_r]
        kk_i = kk[lo_r:hi_r]
        has_off = (i < N_SUB - 1) if backward else (i > 0)
        if has_off:
            r_i = b[hi_r:hi_r + 1] if backward else b[lo_r - 1:lo_r]
            qs = (q_i * jnp.exp(b_i - r_i)).astype(BF16)
            ks = (kk * jnp.exp(jnp.minimum(r_i - b, 0.0))).astype(BF16)
            a = _dot_nt(qs, ks)
            off_mask = (sub_col >= hi_r) if backward else (sub_col < lo_r)
            a = jnp.where(off_mask, a, 0.0)
        else:
            a = jnp.zeros((HG_SUB, c), F32)
        for s in range(HG_SUB):
            dec = jnp.exp(jnp.minimum(b_i - b_i[s:s + 1], 0.0))
            colsum = jnp.sum(q_i * dec * kk_i[s:s + 1], axis=-1, keepdims=True)
            keep = (sub_row <= s) if backward else (sub_row >= s)
            a = jnp.where((sub_col == lo_r + s) & keep, colsum, a)
        att_rows.append(a)
    att = jnp.concatenate(att_rows, axis=0)
    o = o_inter + _dot(att.astype(BF16), v.astype(BF16))

    b_last = b[0:1] if backward else b[c - 1:c]
    kd = (kk * jnp.exp(b_last - b)).astype(BF16)
    st_new = st * jnp.exp(b_last) + _dot_tn(v.astype(BF16), kd)
    return o, st_new


def _hgrn_kernel(q_ref, i_ref, zf_ref, zb_ref, g_ref, lb_ref, s0_ref, o_ref, sfin_ref, of_ref, st_ref,
                 *, t_len):
    n_chunk = t_len // HG_CHUNK

    def sl(c):
        return pl.ds(pl.multiple_of(c * HG_CHUNK, HG_CHUNK), HG_CHUNK)

    st_ref[...] = s0_ref[0]

    def fwd(c, carry):
        o, st = _gla_chunk(q_ref[sl(c), :], i_ref[sl(c), :], zf_ref[sl(c), :], lb_ref[0:1, :],
                           st_ref[...], backward=False)
        of_ref[sl(c), :] = o
        st_ref[...] = st
        return carry

    lax.fori_loop(0, n_chunk, fwd, 0)
    sfin_ref[0] = st_ref[...]
    st_ref[...] = s0_ref[1]

    def bwd(j, carry):
        c = n_chunk - 1 - j
        o, st = _gla_chunk(q_ref[sl(c), :], i_ref[sl(c), :], zb_ref[sl(c), :], lb_ref[1:2, :],
                           st_ref[...], backward=True)
        st_ref[...] = st
        tot = of_ref[sl(c), :] + o
        o_ref[sl(c), :] = _rms(tot) * _silu(g_ref[sl(c), :])
        return carry

    lax.fori_loop(0, n_chunk, bwd, 0)
    sfin_ref[1] = st_ref[...]


def _hgrn2_bidir(p_all, lb, s0_t, row_off, n_batch, t_len):
    off = row_off // t_len
    cb = FN_W // LANES

    def col(j):
        return pl.BlockSpec((t_len, LANES), lambda b, h: (off + b, cb + j * HG_HEADS + h))

    kern = functools.partial(_hgrn_kernel, t_len=t_len)
    state_spec = pl.BlockSpec((None, 2, None, HG_DK, HG_DK), lambda b, h: (b, 0, h, 0, 0))
    return pl.pallas_call(
        kern,
        out_shape=(jax.ShapeDtypeStruct((n_batch * t_len, HG_W), F32),
                   jax.ShapeDtypeStruct((n_batch, 2, HG_HEADS, HG_DK, HG_DK), F32)),
        grid=(n_batch, HG_HEADS),
        in_specs=[col(0), col(1), col(2), col(3), col(4),
                  pl.BlockSpec((2, LANES), lambda b, h: (0, h)),
                  state_spec],
        out_specs=(pl.BlockSpec((t_len, LANES), lambda b, h: (b, h)), state_spec),
        scratch_shapes=[pltpu.VMEM((t_len, LANES), F32), pltpu.VMEM((HG_DK, HG_DK), F32)],
        compiler_params=pltpu.CompilerParams(
            dimension_semantics=("arbitrary", "arbitrary"), vmem_limit_bytes=VMEM_LIMIT),
    )(p_all, p_all, p_all, p_all, p_all, lb, s0_t)


def _route(h2, rw_ref, rb_ref, base_ref):
    hi, mid, _ = _split3(h2)
    w_hi = rw_ref[0]
    w_lo = rw_ref[1]
    logits = _dot(hi, w_hi) + _dot(mid, w_hi) + _dot(hi, w_lo) + rb_ref[...]
    lane = lax.broadcasted_iota(jnp.int32, (TM, LANES), 1)
    work = jnp.where(lane < N_EXPERTS, logits, -jnp.inf)
    vals, hots = [], []
    ids = jnp.zeros((TM, LANES), jnp.int32)
    for k in range(TOP_K):
        m = jnp.max(work, axis=-1, keepdims=True)
        idx = jnp.min(jnp.where(work == m, lane, LANES), axis=-1, keepdims=True)
        hot = lane == idx
        work = jnp.where(hot, -jnp.inf, work)
        ids = jnp.where(lane == k, idx, ids)
        vals.append(m)
        hots.append(hot)
    es = [jnp.exp(v - vals[0]) for v in vals]
    den = es[0] + es[1] + es[2] + es[3]
    gates = jnp.zeros((TM, LANES), F32)
    for k in range(TOP_K):
        gates = jnp.where(lane == k, es[k] / den, gates)
    cnt = jnp.zeros((TM, LANES), F32)
    for hot in hots:
        cnt = cnt + hot.astype(F32)
    row = lax.broadcasted_iota(jnp.int32, (TM, TM), 0)
    col = lax.broadcasted_iota(jnp.int32, (TM, TM), 1)
    before = _dot((col < row).astype(BF16), cnt.astype(BF16)) + base_ref[...]
    ranks = jnp.zeros((TM, LANES), jnp.int32)
    for k in range(TOP_K):
        rk = jnp.sum(jnp.where(hots[k], before, 0.0), axis=-1, keepdims=True)
        ranks = jnp.where(lane == k, rk.astype(jnp.int32), ranks)
    base_ref[...] = base_ref[...] + jnp.sum(cnt, axis=0, keepdims=True)
    return ids, gates, ranks


def _post_kernel(*refs, with_conv):
    if with_conv:
        (a_ref, c_ref, cprev_ref, cnext_ref, cw_ref, x_ref, mod_ref, g_ref, wo_ref, rw_ref, rb_ref,
         xo_ref, h2_ref, ids_ref, gates_ref, ranks_ref, cnt_ref, base_ref) = refs
    else:
        (a_ref, b_ref, x_ref, mod_ref, g_ref, wo_ref, rw_ref, rb_ref,
         xo_ref, h2_ref, ids_ref, gates_ref, ranks_ref, cnt_ref, base_ref) = refs
    i = pl.program_id(0)

    @pl.when(i == 0)
    def _():
        base_ref[...] = jnp.zeros_like(base_ref)

    if with_conv:
        j = i - PROMPT_TILES
        is_prompt = i < PROMPT_TILES
        seq_start = is_prompt | (j % TILES_PER_DEC_SEQ == 0)
        seq_end = is_prompt | (j % TILES_PER_DEC_SEQ == TILES_PER_DEC_SEQ - 1)
        bg = c_ref[:, 0:SC_W]
        z = c_ref[:, SC_W:2 * SC_W] * c_ref[:, 2 * SC_W:3 * SC_W]
        z_prev = cprev_ref[SUBLANES - 1:SUBLANES, SC_W:2 * SC_W] * cprev_ref[SUBLANES - 1:SUBLANES, 2 * SC_W:3 * SC_W]
        z_next = cnext_ref[0:1, SC_W:2 * SC_W] * cnext_ref[0:1, 2 * SC_W:3 * SC_W]
        z_prev = jnp.where(seq_start, 0.0, z_prev)
        z_next = jnp.where(seq_end, 0.0, z_next)
        row = lax.broadcasted_iota(jnp.int32, (TM, SC_W), 0)
        zm1 = jnp.where(row == 0, z_prev, pltpu.roll(z, 1, axis=0))
        zp1 = jnp.where(row == TM - 1, z_next, pltpu.roll(z, TM - 1, axis=0))
        b_val = bg * (zm1 * cw_ref[0:1, :] + z * cw_ref[1:2, :] + zp1 * cw_ref[2:3, :])
    else:
        b_val = b_ref[...]
    half = a_ref.shape[1]
    mix = _dot(a_ref[...].astype(BF16), wo_ref[0:half, :]) + _dot(b_val.astype(BF16), wo_ref[half:, :])
    x_new = x_ref[...] + mod_ref[2:3, :] * mix
    xo_ref[...] = x_new
    h2 = _rms(x_new) * g_ref[...] * (1.0 + mod_ref[4:5, :]) + mod_ref[3:4, :]
    h2_ref[...] = h2
    ids, gates, ranks = _route(h2, rw_ref, rb_ref, base_ref)
    ids_ref[...] = ids
    gates_ref[...] = gates
    ranks_ref[...] = ranks
    cnt_ref[...] = jnp.broadcast_to(base_ref[...], cnt_ref.shape)


def _post_mix(a_all, b_src, conv_w, x_all, mod_l, g2, wo_bf, rw2, rb, with_conv):
    tile = lambda w: pl.BlockSpec((TM, w), lambda i: (i, 0))
    full = lambda shp: pl.BlockSpec(shp, lambda i: tuple(0 for _ in shp))
    rows8 = TM // SUBLANES
    n8 = N_TOK // SUBLANES
    in_specs = [tile(a_all.shape[1])]
    args = [a_all]
    if with_conv:
        cw = 3 * SC_W
        in_specs += [
            pl.BlockSpec((TM, cw), lambda i: (i, 1)),
            pl.BlockSpec((SUBLANES, cw), lambda i: (jnp.maximum(i * rows8 - 1, 0), 1)),
            pl.BlockSpec((SUBLANES, cw), lambda i: (jnp.minimum((i + 1) * rows8, n8 - 1), 1)),
            full((3, SC_W)),
        ]
        args += [b_src, b_src, b_src, conv_w]
    else:
        in_specs += [tile(b_src.shape[1])]
        args += [b_src]
    in_specs += [
        tile(D_MODEL),
        pl.BlockSpec((None, 6, D_MODEL), lambda i: (_mod_row(i), 0, 0)),
        full((1, D_MODEL)),
        full((D_MODEL, D_MODEL)),
        full((2, D_MODEL, LANES)),
        full((1, LANES)),
    ]
    args += [x_all, mod_l, g2.reshape(1, D_MODEL), wo_bf, rw2, rb]
    out_shape = (
        jax.ShapeDtypeStruct((N_TOK, D_MODEL), F32),
        jax.ShapeDtypeStruct((N_TOK, D_MODEL), F32),
        jax.ShapeDtypeStruct((N_TOK, LANES), jnp.int32),
        jax.ShapeDtypeStruct((N_TOK, LANES), F32),
        jax.ShapeDtypeStruct((N_TOK, LANES), jnp.int32),
        jax.ShapeDtypeStruct((SUBLANES, LANES), F32),
    )
    out_specs = (tile(D_MODEL), tile(D_MODEL), tile(LANES), tile(LANES), tile(LANES),
                 full((SUBLANES, LANES)))
    return pl.pallas_call(
        functools.partial(_post_kernel, with_conv=with_conv),
        out_shape=out_shape,
        grid=(N_TILES,),
        in_specs=in_specs,
        out_specs=out_specs,
        scratch_shapes=[pltpu.VMEM((1, LANES), F32)],
        compiler_params=pltpu.CompilerParams(
            dimension_semantics=("arbitrary",), vmem_limit_bytes=VMEM_LIMIT),
    )(*args)


N_YA_ROWS = N_ASG + 2 * MOE_BLK


def _asg_token(a):
    a = jnp.maximum(a, 0)
    return a - N_TOK * ((a >= N_TOK).astype(jnp.int32) + (a >= 2 * N_TOK).astype(jnp.int32)
                        + (a >= 3 * N_TOK).astype(jnp.int32))


def _ffn_kernel(asg_ref, bexp_ref, h2_hbm, w1_ref, b1_ref, w2_ref, b2_ref, ya_hbm,
                xbuf, ybuf, w1b, w2b, gsem, ssem):
    b = pl.program_id(0)
    nb = pl.num_programs(0)
    slot = b % 2

    def gather_copy(blk, slot_, r):
        tok = _asg_token(asg_ref[blk * MOE_BLK + r])
        return pltpu.make_async_copy(h2_hbm.at[pl.ds(tok, 1), :], xbuf.at[slot_, pl.ds(r, 1), :],
                                     gsem.at[slot_])

    def issue_gather(blk, slot_):
        def body(r, carry):
            gather_copy(blk, slot_, r).start()
            return carry
        lax.fori_loop(0, MOE_BLK, body, 0)

    def wait_rows(sem, slot_):
        pltpu.make_async_copy(h2_hbm.at[pl.ds(0, MOE_BLK), :], xbuf.at[slot_], sem.at[slot_]).wait()

    @pl.when(b == 0)
    def _():
        issue_gather(0, 0)

    wait_rows(gsem, slot)

    @pl.when(b + 1 < nb)
    def _():
        issue_gather(b + 1, 1 - slot)

    prev = jnp.maximum(b - 1, 0)

    @pl.when((b == 0) | (bexp_ref[b] != bexp_ref[prev]))
    def _():
        w1b[...] = w1_ref[...].astype(BF16)
        w2b[...] = w2_ref[...].astype(BF16)

    @pl.when(b >= 2)
    def _():
        wait_rows(ssem, slot)

    x = xbuf[slot].astype(BF16)
    u = _dot(x, w1b[...]) + b1_ref[...]
    glu = jnp.minimum(u[:, :D_FF], SWIGLU_LIMIT)
    lin = jnp.clip(u[:, D_FF:], -SWIGLU_LIMIT, SWIGLU_LIMIT)
    act = glu * _sigmoid_pair(SWIGLU_ALPHA * glu)[0] * (lin + 1.0)
    ybuf[slot] = _dot(act.astype(BF16), w2b[...]) + b2_ref[...]

    def scatter(r, carry):
        a = asg_ref[b * MOE_BLK + r]
        dst = jnp.where(a >= 0, a, N_ASG + slot * MOE_BLK + r)
        pltpu.make_async_copy(ybuf.at[slot, pl.ds(r, 1), :], ya_hbm.at[pl.ds(dst, 1), :],
                              ssem.at[slot]).start()
        return carry

    lax.fori_loop(0, MOE_BLK, scatter, 0)

    @pl.when(b == nb - 1)
    def _():
        wait_rows(ssem, slot)
        wait_rows(ssem, 1 - slot)


def _moe_experts(slot_asg, blk_expert, h2, w1, b1, w2, b2):
    grid_spec = pltpu.PrefetchScalarGridSpec(
        num_scalar_prefetch=2,
        grid=(N_BLK,),
        in_specs=[
            pl.BlockSpec(memory_space=pl.ANY),
            pl.BlockSpec((None, D_MODEL, 2 * D_FF), lambda b, asg, be: (be[b], 0, 0)),
            pl.BlockSpec((None, 1, 2 * D_FF), lambda b, asg, be: (be[b], 0, 0)),
            pl.BlockSpec((None, D_FF, D_MODEL), lambda b, asg, be: (be[b], 0, 0)),
            pl.BlockSpec((None, 1, D_MODEL), lambda b, asg, be: (be[b], 0, 0)),
        ],
        out_specs=pl.BlockSpec(memory_space=pl.ANY),
        scratch_shapes=[
            pltpu.VMEM((2, MOE_BLK, D_MODEL), F32),
            pltpu.VMEM((2, MOE_BLK, D_MODEL), F32),
            pltpu.VMEM((D_MODEL, 2 * D_FF), BF16),
            pltpu.VMEM((D_FF, D_MODEL), BF16),
            pltpu.SemaphoreType.DMA((2,)),
            pltpu.SemaphoreType.DMA((2,)),
        ],
    )
    return pl.pallas_call(
        _ffn_kernel,
        out_shape=jax.ShapeDtypeStruct((N_YA_ROWS, D_MODEL), F32),
        grid_spec=grid_spec,
        compiler_params=pltpu.CompilerParams(
            dimension_semantics=("arbitrary",), vmem_limit_bytes=VMEM_LIMIT,
            has_side_effects=True),
    )(slot_asg, blk_expert, h2, w1, b1.reshape(N_EXPERTS, 1, 2 * D_FF), w2, b2.reshape(N_EXPERTS, 1, D_MODEL))


def _combine_kernel(y0_ref, y1_ref, y2_ref, y3_ref, gates_ref, x_ref, mod_ref, g_ref, o_ref, *, final):
    gates = gates_ref[...]
    acc = None
    for k, y_ref in enumerate((y0_ref, y1_ref, y2_ref, y3_ref)):
        t = gates[:, k:k + 1] * y_ref[...]
        acc = t if acc is None else acc + t
    x = x_ref[...] + mod_ref[5:6, :] * acc
    if final:
        x = _rms(x) * g_ref[...]
    o_ref[...] = x


def _moe_combine(ya, gates, x_all, mod_l, final_g, final):
    def plane(k):
        return pl.BlockSpec((TM, D_MODEL), lambda i, k=k: (k * N_TILES + i, 0))

    return pl.pallas_call(
        functools.partial(_combine_kernel, final=final),
        out_shape=jax.ShapeDtypeStruct((N_TOK, D_MODEL), F32),
        grid=(N_TILES,),
        in_specs=[plane(0), plane(1), plane(2), plane(3),
                  pl.BlockSpec((TM, LANES), lambda i: (i, 0)),
                  pl.BlockSpec((TM, D_MODEL), lambda i: (i, 0)),
                  pl.BlockSpec((None, 6, D_MODEL), lambda i: (_mod_row(i), 0, 0)),
                  pl.BlockSpec((1, D_MODEL), lambda i: (0, 0))],
        out_specs=pl.BlockSpec((TM, D_MODEL), lambda i: (i, 0)),
        compiler_params=pltpu.CompilerParams(
            dimension_semantics=("arbitrary",), vmem_limit_bytes=VMEM_LIMIT),
    )(ya, ya, ya, ya, gates, x_all, mod_l, final_g.reshape(1, D_MODEL))


def _moe_layout(ids, ranks, counts):
    e = ids[:, :TOP_K]
    rk = ranks[:, :TOP_K]
    cnt = counts[0, :N_EXPERTS].astype(jnp.int32)
    padded = (cnt + MOE_BLK - 1) // MOE_BLK * MOE_BLK
    pad_end = jnp.cumsum(padded)
    pad_start = pad_end - padded
    dest = pad_start[e] + rk
    asg = jnp.arange(N_TOK, dtype=jnp.int32)[:, None] + N_TOK * jnp.arange(TOP_K, dtype=jnp.int32)[None, :]
    slot_asg = jnp.full((N_SLOT,), -1, jnp.int32).at[dest.reshape(-1)].set(asg.reshape(-1))
    blk_start = jnp.arange(N_BLK, dtype=jnp.int32) * MOE_BLK
    blk_expert = jnp.minimum(jnp.searchsorted(pad_end, blk_start, side='right'), N_EXPERTS - 1).astype(jnp.int32)
    return slot_asg, blk_expert


def kernel(x_prompt, x_sample, cache_attn_k, cache_attn_v, state_hgrn, c, c_ctx, ada_w, ada_b, norm1_g, norm2_g, final_g, ab_w_in, ab_w_out, na_rpb, sc_conv_w, cd_w_in, cd_w_out, hg_lb_logits, router_w, router_b, moe_w1, moe_b1, moe_w2, moe_b2):
    x_all = jnp.concatenate([x_prompt.reshape(N_PROMPT, D_MODEL), x_sample.reshape(N_SAMPLE, D_MODEL)], axis=0)
    m16 = jnp.zeros((N_MOD_ROWS, D_MODEL), F32).at[0].set(c_ctx).at[1:1 + DEC_BATCH].set(c)
    mod = _modulation(m16, ada_w, ada_b)

    lb_sm = jax.nn.softmax(hg_lb_logits.astype(F32), axis=0)
    lb_all = jnp.cumsum(lb_sm, axis=0) - lb_sm[0:1]

    rw_pad = jnp.zeros((DEPTH, D_MODEL, LANES), F32).at[:, :, :N_EXPERTS].set(router_w)
    rw_hi = rw_pad.astype(BF16)
    rw_lo = (rw_pad - rw_hi.astype(F32)).astype(BF16)
    rw2 = jnp.stack([rw_hi, rw_lo], axis=1)
    rb_pad = jnp.zeros((DEPTH, 1, LANES), F32).at[:, 0, :N_EXPERTS].set(router_b)

    new_k = new_v = new_s = None
    for l in range(DEPTH):
        j = l // 2
        if l % 2 == 0:
            p_all = _pre_project(x_all, mod[l], norm1_g[l], ab_w_in[j].astype(BF16))
            att_p = _context_attention(p_all)
            ck = cache_attn_k[:, j].transpose(0, 2, 1, 3).reshape(DEC_BATCH, PAST_LEN, NA_W)
            cv = cache_attn_v[:, j].transpose(0, 2, 1, 3).reshape(DEC_BATCH, PAST_LEN, NA_W)
            att_s = _neighbourhood_attention(p_all, ck, cv, _natten_bias(na_rpb[j]))
            a_all = jnp.concatenate([att_p, att_s], axis=0)
            kp = p_all[:N_PROMPT, NA_W:2 * NA_W].reshape(BATCH, SEQ, NA_HEADS, NA_HEAD_DIM).transpose(0, 2, 1, 3)
            vp = p_all[:N_PROMPT, 2 * NA_W:3 * NA_W].reshape(BATCH, SEQ, NA_HEADS, NA_HEAD_DIM).transpose(0, 2, 1, 3)
            new_k, new_v = kp[:, None], vp[:, None]
            outs = _post_mix(a_all, p_all, sc_conv_w[j], x_all, mod[l], norm2_g[l],
                             ab_w_out[j].astype(BF16), rw2[l], rb_pad[l], with_conv=True)
        else:
            p_all = _pre_project(x_all, mod[l], norm1_g[l], cd_w_in[j].astype(BF16))
            fn_p = _fourier_mix(p_all, 0, BATCH, SEQ)
            fn_s = _fourier_mix(p_all, N_PROMPT, DEC_BATCH, DEC_SEQ)
            s0_p = jnp.zeros((BATCH, 2, HG_HEADS, HG_DK, HG_DK), F32)
            s0_s = jnp.swapaxes(state_hgrn[:, j].astype(F32), -1, -2)
            hg_p, sfin_p = _hgrn2_bidir(p_all, lb_all[l], s0_p, 0, BATCH, SEQ)
            hg_s, _ = _hgrn2_bidir(p_all, lb_all[l], s0_s, N_PROMPT, DEC_BATCH, DEC_SEQ)
            new_s = jnp.swapaxes(sfin_p, -1, -2)[:, None].astype(x_prompt.dtype)
            a_all = jnp.concatenate([fn_p, fn_s], axis=0)
            b_all = jnp.concatenate([hg_p, hg_s], axis=0)
            outs = _post_mix(a_all, b_all, None, x_all, mod[l], norm2_g[l],
                             cd_w_out[j].astype(BF16), rw2[l], rb_pad[l], with_conv=False)
        x_mid, h2, ids, gates, ranks, counts = outs
        slot_asg, blk_expert = _moe_layout(ids, ranks, counts)
        ya = _moe_experts(slot_asg, blk_expert, h2, moe_w1[l], moe_b1[l], moe_w2[l], moe_b2[l])
        x_all = _moe_combine(ya, gates, x_mid, mod[l], final_g, final=(l == DEPTH - 1))

    y_prompt = x_all[:N_PROMPT].reshape(BATCH, SEQ, D_MODEL)
    y_sample = x_all[N_PROMPT:].reshape(DEC_BATCH, DEC_SEQ, D_MODEL)
    return (y_prompt, y_sample, new_k, new_v, new_s)
```

```python
import functools

import numpy as np
import jax
import jax.numpy as jnp
from jax import lax
from jax.experimental import pallas as pl
from jax.experimental.pallas import tpu as pltpu

F32 = jnp.float32
BF16 = jnp.bfloat16

D_MODEL = 1024
BATCH = 16
SEQ = 256
DEC_BATCH = 8
DEC_SEQ = 2048
PAST_LEN = 512
DEPTH = 2
GRID_W = 64
GRID_ROWS = DEC_SEQ // GRID_W
NA_HEADS = 8
NA_HEAD_DIM = 64
NA_W = NA_HEADS * NA_HEAD_DIM
SC_W = D_MODEL - NA_W
NA_SCALE = NA_HEAD_DIM ** -0.5
MAX_WR = 8
WIN_W = 16
FN_W = 512
FN_GROUPS = 4
FN_GROUP_W = 128
HG_W = 512
HG_HEADS = 4
HG_DK = 128
HG_CHUNK = 64
HG_SUB = 16
N_EXPERTS = 32
TOP_K = 4
D_FF = D_MODEL
SWIGLU_LIMIT = 7.0
SWIGLU_ALPHA = 1.702
EPS = 1e-6
NEG_INF = -1e30

N_PROMPT = BATCH * SEQ
N_SAMPLE = DEC_BATCH * DEC_SEQ
N_TOK = N_PROMPT + N_SAMPLE
TM = 256
N_TILES = N_TOK // TM
PROMPT_TILES = N_PROMPT // TM
TILES_PER_DEC_SEQ = DEC_SEQ // TM
N_MOD_ROWS = 16
LANES = 128
SUBLANES = 8

MOE_BLK = 256
N_ASG = N_TOK * TOP_K
N_BLK = N_ASG // MOE_BLK + N_EXPERTS
N_SLOT = N_BLK * MOE_BLK
VMEM_LIMIT = 56 * 1024 * 1024


def _mod_row(i):
    return jnp.where(i < PROMPT_TILES, 0, 1 + (i - PROMPT_TILES) // TILES_PER_DEC_SEQ)


def _sigmoid_pair(z):
    e = jnp.exp(-jnp.abs(z))
    r = 1.0 / (1.0 + e)
    er = e * r
    pos = z >= 0
    return jnp.where(pos, r, er), jnp.where(pos, er, r)


def _silu(x):
    return x * _sigmoid_pair(x)[0]


def _rms(x):
    return x * lax.rsqrt(jnp.mean(x * x, axis=-1, keepdims=True) + EPS)


def _dot(a, b):
    return jnp.dot(a, b, preferred_element_type=F32)


def _dot_nt(a, b):
    return lax.dot_general(a, b, (((1,), (1,)), ((), ())), preferred_element_type=F32)


def _dot_tn(a, b):
    return lax.dot_general(a, b, (((0,), (0,)), ((), ())), preferred_element_type=F32)


def _split3(x):
    hi = x.astype(BF16)
    r1 = x - hi.astype(F32)
    mid = r1.astype(BF16)
    lo = (r1 - mid.astype(F32)).astype(BF16)
    return hi, mid, lo


MOD_TN = 1024


def _mod_kernel(m_ref, w_ref, b_ref, o_ref):
    a = _silu(m_ref[...]).astype(BF16)
    o_ref[...] = _dot(a, w_ref[...].astype(BF16)) + b_ref[...]


def _modulation(m16, ada_w, ada_b):
    n_out = 6 * D_MODEL
    out = pl.pallas_call(
        _mod_kernel,
        out_shape=jax.ShapeDtypeStruct((DEPTH, N_MOD_ROWS, n_out), F32),
        grid=(DEPTH, n_out // MOD_TN),
        in_specs=[
            pl.BlockSpec((N_MOD_ROWS, D_MODEL), lambda l, j: (0, 0)),
            pl.BlockSpec((None, D_MODEL, MOD_TN), lambda l, j: (l, 0, j)),
            pl.BlockSpec((None, 1, MOD_TN), lambda l, j: (l, 0, j)),
        ],
        out_specs=pl.BlockSpec((None, N_MOD_ROWS, MOD_TN), lambda l, j: (l, 0, j)),
        compiler_params=pltpu.CompilerParams(
            dimension_semantics=("arbitrary", "arbitrary"), vmem_limit_bytes=VMEM_LIMIT),
    )(m16, ada_w, ada_b.reshape(DEPTH, 1, n_out))
    return out.reshape(DEPTH, N_MOD_ROWS, 6, D_MODEL)


def _pre_kernel(x_ref, mod_ref, g_ref, w_ref, o_ref):
    y = _rms(x_ref[...]) * g_ref[...]
    h = y * (1.0 + mod_ref[1:2, :]) + mod_ref[0:1, :]
    o_ref[...] = _dot(h.astype(BF16), w_ref[...])


def _pre_project(x_all, mod_l, g, w_bf):
    n_out = w_bf.shape[1]
    return pl.pallas_call(
        _pre_kernel,
        out_shape=jax.ShapeDtypeStruct((N_TOK, n_out), F32),
        grid=(N_TILES,),
        in_specs=[
            pl.BlockSpec((TM, D_MODEL), lambda i: (i, 0)),
            pl.BlockSpec((None, 6, D_MODEL), lambda i: (_mod_row(i), 0, 0)),
            pl.BlockSpec((1, D_MODEL), lambda i: (0, 0)),
            pl.BlockSpec((D_MODEL, n_out), lambda i: (0, 0)),
        ],
        out_specs=pl.BlockSpec((TM, n_out), lambda i: (i, 0)),
        compiler_params=pltpu.CompilerParams(
            dimension_semantics=("arbitrary",), vmem_limit_bytes=VMEM_LIMIT),
    )(x_all, mod_l, g.reshape(1, D_MODEL), w_bf)


def _softmax_pv(parts):
    m = None
    for s, _ in parts:
        mi = jnp.max(s, axis=-1, keepdims=True)
        m = mi if m is None else jnp.maximum(m, mi)
    acc = None
    den = None
    for s, v in parts:
        p = jnp.exp(s - m)
        d = jnp.sum(p, axis=-1, keepdims=True)
        o = _dot(p.astype(BF16), v)
        acc = o if acc is None else acc + o
        den = d if den is None else den + d
    return acc / den


def _ctx_attn_kernel(q_ref, k_ref, v_ref, o_ref):
    q = q_ref[...]
    kb = k_ref[...].astype(BF16)
    vb = v_ref[...].astype(BF16)
    lane = lax.broadcasted_iota(jnp.int32, q.shape, 1)
    out = jnp.zeros(q.shape, F32)
    for h in range(2):
        hm = (lane >= NA_HEAD_DIM) if h == 1 else (lane < NA_HEAD_DIM)
        qm = jnp.where(hm, q, 0.0).astype(BF16)
        s = _dot_nt(qm, kb) * NA_SCALE
        out = jnp.where(hm, _softmax_pv([(s, vb)]), out)
    o_ref[...] = out


def _context_attention(p_all):
    n_hp = NA_HEADS // 2
    return pl.pallas_call(
        _ctx_attn_kernel,
        out_shape=jax.ShapeDtypeStruct((N_PROMPT, NA_W), F32),
        grid=(BATCH, n_hp),
        in_specs=[
            pl.BlockSpec((SEQ, LANES), lambda b, hp: (b, hp)),
            pl.BlockSpec((SEQ, LANES), lambda b, hp: (b, n_hp + hp)),
            pl.BlockSpec((SEQ, LANES), lambda b, hp: (b, 2 * n_hp + hp)),
        ],
        out_specs=pl.BlockSpec((SEQ, LANES), lambda b, hp: (b, hp)),
        compiler_params=pltpu.CompilerParams(
            dimension_semantics=("arbitrary", "arbitrary"), vmem_limit_bytes=VMEM_LIMIT),
    )(p_all, p_all, p_all)


N_BIAS_VAR = MAX_WR
WIN_KEYS = MAX_WR * GRID_W


def _natten_kernel(q_ref, k_ref, v_ref, ck_ref, cv_ref, bias_ref, o_ref):
    ckb = ck_ref[...].astype(BF16)
    cvb = cv_ref[...].astype(BF16)
    lane = lax.broadcasted_iota(jnp.int32, (GRID_W, LANES), 1)

    def row(r, carry):
        r0 = jnp.clip(r - MAX_WR // 2, 0, GRID_ROWS - MAX_WR)
        var = r0 - r + (MAX_WR - 1)
        q = q_ref[pl.ds(pl.multiple_of(r * GRID_W, GRID_W), GRID_W), :]
        kw = k_ref[pl.ds(pl.multiple_of(r0 * GRID_W, GRID_W), WIN_KEYS), :].astype(BF16)
        vw = v_ref[pl.ds(pl.multiple_of(r0 * GRID_W, GRID_W), WIN_KEYS), :].astype(BF16)
        out = jnp.zeros((GRID_W, LANES), F32)
        for h in range(2):
            hm = (lane >= NA_HEAD_DIM) if h == 1 else (lane < NA_HEAD_DIM)
            qm = jnp.where(hm, q, 0.0).astype(BF16)
            s_loc = _dot_nt(qm, kw) * NA_SCALE + bias_ref[h, var]
            s_ctx = _dot_nt(qm, ckb) * NA_SCALE
            out = jnp.where(hm, _softmax_pv([(s_loc, vw), (s_ctx, cvb)]), out)
        o_ref[pl.ds(pl.multiple_of(r * GRID_W, GRID_W), GRID_W), :] = out
        return carry

    lax.fori_loop(0, GRID_ROWS, row, 0)


def _natten_bias(rpb):
    qc = np.arange(GRID_W)[:, None]
    kc = np.arange(GRID_W)[None, :]
    ws = np.clip(qc - WIN_W // 2, 0, GRID_W - WIN_W)
    valid = (kc >= ws) & (kc < ws + WIN_W)
    dc = np.clip(kc - qc + WIN_W - 1, 0, 2 * WIN_W - 2)
    t = rpb.astype(F32)[:, :, dc]
    t = jnp.where(jnp.asarray(valid)[None, None], t, NEG_INF)
    dr = np.arange(N_BIAS_VAR)[:, None] + np.arange(MAX_WR)[None, :]
    slab = t[:, dr]
    return slab.transpose(0, 1, 3, 2, 4).reshape(NA_HEADS, N_BIAS_VAR, GRID_W, WIN_KEYS)


def _neighbourhood_attention(p_all, ck, cv, bias):
    n_hp = NA_HEADS // 2
    off = N_PROMPT // DEC_SEQ
    return pl.pallas_call(
        _natten_kernel,
        out_shape=jax.ShapeDtypeStruct((N_SAMPLE, NA_W), F32),
        grid=(DEC_BATCH, n_hp),
        in_specs=[
            pl.BlockSpec((DEC_SEQ, LANES), lambda b, hp: (off + b, hp)),
            pl.BlockSpec((DEC_SEQ, LANES), lambda b, hp: (off + b, n_hp + hp)),
            pl.BlockSpec((DEC_SEQ, LANES), lambda b, hp: (off + b, 2 * n_hp + hp)),
            pl.BlockSpec((None, PAST_LEN, LANES), lambda b, hp: (b, 0, hp)),
            pl.BlockSpec((None, PAST_LEN, LANES), lambda b, hp: (b, 0, hp)),
            pl.BlockSpec((2, N_BIAS_VAR, GRID_W, WIN_KEYS), lambda b, hp: (hp, 0, 0, 0)),
        ],
        out_specs=pl.BlockSpec((DEC_SEQ, LANES), lambda b, hp: (b, hp)),
        compiler_params=pltpu.CompilerParams(
            dimension_semantics=("arbitrary", "arbitrary"), vmem_limit_bytes=VMEM_LIMIT),
    )(p_all, p_all, p_all, ck, cv, bias)


def _dft_tables(t):
    j = np.arange(t, dtype=np.int64)
    ang = 2.0 * np.pi * ((j[:, None] * j[None, :]) % t).astype(np.float64) / t
    return np.cos(ang), np.sin(ang)


def _fourier_kernel(z_ref, lt_ref, rc_ref, rs_ref, o_ref, ab_ref, *, t_len, scale):
    @pl.when(pl.program_id(1) == 0)
    def _():
        z = z_ref[...].astype(BF16)
        for g in range(FN_GROUPS):
            zg = z[:, g * FN_GROUP_W:(g + 1) * FN_GROUP_W]
            ab_ref[0:t_len, g * FN_GROUP_W:(g + 1) * FN_GROUP_W] = _dot(zg, rc_ref[...]).astype(BF16)
            ab_ref[t_len:2 * t_len, g * FN_GROUP_W:(g + 1) * FN_GROUP_W] = _dot(zg, rs_ref[...]).astype(BF16)

    o_ref[...] = _dot(lt_ref[...], ab_ref[...]) * scale


def _fourier_mix(p_all, row_off, n_batch, t_len):
    ct, st = _dft_tables(t_len)
    lt = jnp.asarray(np.concatenate([ct, -st], axis=1), dtype=F32).astype(BF16)
    cc, sc = _dft_tables(FN_GROUP_W)
    rc = jnp.asarray(cc, dtype=F32).astype(BF16)
    rs = jnp.asarray(sc, dtype=F32).astype(BF16)
    tr = min(t_len, 512)
    n_rt = t_len // tr
    off = row_off // t_len
    kern = functools.partial(_fourier_kernel, t_len=t_len, scale=float((t_len * FN_GROUP_W) ** -0.5))
    return pl.pallas_call(
        kern,
        out_shape=jax.ShapeDtypeStruct((n_batch * t_len, FN_W), F32),
        grid=(n_batch, n_rt),
        in_specs=[
            pl.BlockSpec((t_len, FN_W), lambda b, i: (off + b, 0)),
            pl.BlockSpec((tr, 2 * t_len), lambda b, i: (i, 0)),
            pl.BlockSpec((FN_GROUP_W, FN_GROUP_W), lambda b, i: (0, 0)),
            pl.BlockSpec((FN_GROUP_W, FN_GROUP_W), lambda b, i: (0, 0)),
        ],
        out_specs=pl.BlockSpec((tr, FN_W), lambda b, i: (b * n_rt + i, 0)),
        scratch_shapes=[pltpu.VMEM((2 * t_len, FN_W), BF16)],
        compiler_params=pltpu.CompilerParams(
            dimension_semantics=("arbitrary", "arbitrary"), vmem_limit_bytes=VMEM_LIMIT),
    )(p_all, lt, rc, rs)


N_SUB = HG_CHUNK // HG_SUB


def _gla_chunk(q, v, z, lb, st, backward):
    c = HG_CHUNK
    sg, sgn = _sigmoid_pair(z)
    lf = jnp.log(lb + (1.0 - lb) * sg)
    kk = (1.0 - lb) * sgn
    row = lax.broadcasted_iota(jnp.int32, (c, c), 0)
    col = lax.broadcasted_iota(jnp.int32, (c, c), 1)
    tri = ((col >= row) if backward else (col <= row)).astype(BF16)
    hi, mid, lo = _split3(lf)
    b = _dot(tri, hi) + _dot(tri, mid) + _dot(tri, lo)
    o_inter = _dot_nt((q * jnp.exp(b)).astype(BF16), st.astype(BF16))

    sub_row = lax.broadcasted_iota(jnp.int32, (HG_SUB, c), 0)
    sub_col = lax.broadcasted_iota(jnp.int32, (HG_SUB, c), 1)
    att_rows = []
    for i in range(N_SUB):
        lo_r, hi_r = i * HG_SUB, (i + 1) * HG_SUB
        q_i = q[lo_r:hi_r]
        b_i = b[lo_r:hi_r]
        kk_i = kk[lo_r:hi_r]
        has_off = (i < N_SUB - 1) if backward else (i > 0)
        if has_off:
            r_i = b[hi_r:hi_r + 1] if backward else b[lo_r - 1:lo_r]
            qs = (q_i * jnp.exp(b_i - r_i)).astype(BF16)
            ks = (kk * jnp.exp(jnp.minimum(r_i - b, 0.0))).astype(BF16)
            a = _dot_nt(qs, ks)
            off_mask = (sub_col >= hi_r) if backward else (sub_col < lo_r)
            a = jnp.where(off_mask, a, 0.0)
        else:
            a = jnp.zeros((HG_SUB, c), F32)
        for s in range(HG_SUB):
            dec = jnp.exp(jnp.minimum(b_i - b_i[s:s + 1], 0.0))
            colsum = jnp.sum(q_i * dec * kk_i[s:s + 1], axis=-1, keepdims=True)
            keep = (sub_row <= s) if backward else (sub_row >= s)
            a = jnp.where((sub_col == lo_r + s) & keep, colsum, a)
        att_rows.append(a)
    att = jnp.concatenate(att_rows, axis=0)
    o = o_inter + _dot(att.astype(BF16), v.astype(BF16))

    b_last = b[0:1] if backward else b[c - 1:c]
    kd = (kk * jnp.exp(b_last - b)).astype(BF16)
    st_new = st * jnp.exp(b_last) + _dot_tn(v.astype(BF16), kd)
    return o, st_new


def _hgrn_kernel(q_ref, i_ref, zf_ref, zb_ref, g_ref, lb_ref, s0_ref, o_ref, sfin_ref, of_ref, st_ref,
                 *, t_len):
    n_chunk = t_len // HG_CHUNK

    def sl(c):
        return pl.ds(pl.multiple_of(c * HG_CHUNK, HG_CHUNK), HG_CHUNK)

    st_ref[...] = s0_ref[0]

    def fwd(c, carry):
        o, st = _gla_chunk(q_ref[sl(c), :], i_ref[sl(c), :], zf_ref[sl(c), :], lb_ref[0:1, :],
                           st_ref[...], backward=False)
        of_ref[sl(c), :] = o
        st_ref[...] = st
        return carry

    lax.fori_loop(0, n_chunk, fwd, 0)
    sfin_ref[0] = st_ref[...]
    st_ref[...] = s0_ref[1]

    def bwd(j, carry):
        c = n_chunk - 1 - j
        o, st = _gla_chunk(q_ref[sl(c), :], i_ref[sl(c), :], zb_ref[sl(c), :], lb_ref[1:2, :],
                           st_ref[...], backward=True)
        st_ref[...] = st
        tot = of_ref[sl(c), :] + o
        o_ref[sl(c), :] = _rms(tot) * _silu(g_ref[sl(c), :])
        return carry

    lax.fori_loop(0, n_chunk, bwd, 0)
    sfin_ref[1] = st_ref[...]


def _hgrn2_bidir(p_all, lb, s0_t, row_off, n_batch, t_len):
    off = row_off // t_len
    cb = FN_W // LANES

    def col(j):
        return pl.BlockSpec((t_len, LANES), lambda b, h: (off + b, cb + j * HG_HEADS + h))

    kern = functools.partial(_hgrn_kernel, t_len=t_len)
    state_spec = pl.BlockSpec((None, 2, None, HG_DK, HG_DK), lambda b, h: (b, 0, h, 0, 0))
    return pl.pallas_call(
        kern,
        out_shape=(jax.ShapeDtypeStruct((n_batch * t_len, HG_W), F32),
                   jax.ShapeDtypeStruct((n_batch, 2, HG_HEADS, HG_DK, HG_DK), F32)),
        grid=(n_batch, HG_HEADS),
        in_specs=[col(0), col(1), col(2), col(3), col(4),
                  pl.BlockSpec((2, LANES), lambda b, h: (0, h)),
                  state_spec],
        out_specs=(pl.BlockSpec((t_len, LANES), lambda b, h: (b, h)), state_spec),
        scratch_shapes=[pltpu.VMEM((t_len, LANES), F32), pltpu.VMEM((HG_DK, HG_DK), F32)],
        compiler_params=pltpu.CompilerParams(
            dimension_semantics=("arbitrary", "arbitrary"), vmem_limit_bytes=VMEM_LIMIT),
    )(p_all, p_all, p_all, p_all, p_all, lb, s0_t)


def _route(h2, rw_ref, rb_ref):
    hi, mid, _ = _split3(h2)
    w_hi = rw_ref[0]
    w_lo = rw_ref[1]
    logits = _dot(hi, w_hi) + _dot(mid, w_hi) + _dot(hi, w_lo) + rb_ref[...]
    lane = lax.broadcasted_iota(jnp.int32, (TM, LANES), 1)
    work = jnp.where(lane < N_EXPERTS, logits, -jnp.inf)
    vals, hots = [], []
    for k in range(TOP_K):
        m = jnp.max(work, axis=-1, keepdims=True)
        idx = jnp.min(jnp.where(work == m, lane, LANES), axis=-1, keepdims=True)
        hot = lane == idx
        work = jnp.where(hot, -jnp.inf, work)
        vals.append(m)
        hots.append(hot)
    es = [jnp.exp(v - vals[0]) for v in vals]
    den = es[0] + es[1] + es[2] + es[3]
    gates = jnp.zeros((TM, LANES), F32)
    for k in range(TOP_K):
        gates = jnp.where(lane == k, es[k] / den, gates)
    cnt = jnp.zeros((TM, LANES), F32)
    for hot in hots:
        cnt = cnt + hot.astype(F32)
    row = lax.broadcasted_iota(jnp.int32, (TM, TM), 0)
    col = lax.broadcasted_iota(jnp.int32, (TM, TM), 1)
    before = _dot((col < row).astype(BF16), cnt.astype(BF16))
    tile_cnt = jnp.sum(cnt, axis=0, keepdims=True)
    erow = lax.broadcasted_iota(jnp.int32, (LANES, LANES), 0)
    ecol = lax.broadcasted_iota(jnp.int32, (LANES, LANES), 1)
    upper = (erow < ecol).astype(BF16)
    c_hi, c_mid, c_lo = _split3(jnp.broadcast_to(tile_cnt, (SUBLANES, LANES)))
    run_off = (_dot(c_hi, upper) + _dot(c_mid, upper) + _dot(c_lo, upper))[0:1]
    where_to = before + run_off
    pos = jnp.zeros((TM, LANES), jnp.int32)
    for k in range(TOP_K):
        pk = jnp.sum(jnp.where(hots[k], where_to, 0.0), axis=-1, keepdims=True)
        pos = jnp.where(lane == k, pk.astype(jnp.int32), pos)
    return gates, pos, tile_cnt


def _post_kernel(*refs, with_conv):
    if with_conv:
        (a_ref, c_ref, cprev_ref, cnext_ref, cw_ref, x_ref, mod_ref, g_ref, wo_ref, rw_ref, rb_ref,
         xo_ref, h2_ref, gates_ref, pos_ref, cnt_ref) = refs
    else:
        (a_ref, b_ref, x_ref, mod_ref, g_ref, wo_ref, rw_ref, rb_ref,
         xo_ref, h2_ref, gates_ref, pos_ref, cnt_ref) = refs
    i = pl.program_id(0)

    if with_conv:
        j = i - PROMPT_TILES
        is_prompt = i < PROMPT_TILES
        seq_start = is_prompt | (j % TILES_PER_DEC_SEQ == 0)
        seq_end = is_prompt | (j % TILES_PER_DEC_SEQ == TILES_PER_DEC_SEQ - 1)
        bg = c_ref[:, 0:SC_W]
        z = c_ref[:, SC_W:2 * SC_W] * c_ref[:, 2 * SC_W:3 * SC_W]
        z_prev = cprev_ref[SUBLANES - 1:SUBLANES, SC_W:2 * SC_W] * cprev_ref[SUBLANES - 1:SUBLANES, 2 * SC_W:3 * SC_W]
        z_next = cnext_ref[0:1, SC_W:2 * SC_W] * cnext_ref[0:1, 2 * SC_W:3 * SC_W]
        z_prev = jnp.where(seq_start, 0.0, z_prev)
        z_next = jnp.where(seq_end, 0.0, z_next)
        row = lax.broadcasted_iota(jnp.int32, (TM, SC_W), 0)
        zm1 = jnp.where(row == 0, z_prev, pltpu.roll(z, 1, axis=0))
        zp1 = jnp.where(row == TM - 1, z_next, pltpu.roll(z, TM - 1, axis=0))
        b_val = bg * (zm1 * cw_ref[0:1, :] + z * cw_ref[1:2, :] + zp1 * cw_ref[2:3, :])
    else:
        b_val = b_ref[...]
    half = a_ref.shape[1]
    mix = _dot(a_ref[...].astype(BF16), wo_ref[0:half, :]) + _dot(b_val.astype(BF16), wo_ref[half:, :])
    x_new = x_ref[...] + mod_ref[2:3, :] * mix
    xo_ref[...] = x_new
    h2 = _rms(x_new) * g_ref[...] * (1.0 + mod_ref[4:5, :]) + mod_ref[3:4, :]
    h2_ref[...] = h2
    gates, pos, tile_cnt = _route(h2, rw_ref, rb_ref)
    gates_ref[...] = gates
    pos_ref[...] = pos
    cnt_ref[...] = jnp.broadcast_to(tile_cnt, cnt_ref.shape).astype(jnp.int32)


def _post_mix(a_all, b_src, conv_w, x_all, mod_l, g2, wo_bf, rw2, rb, with_conv):
    tile = lambda w: pl.BlockSpec((TM, w), lambda i: (i, 0))
    full = lambda shp: pl.BlockSpec(shp, lambda i: tuple(0 for _ in shp))
    rows8 = TM // SUBLANES
    n8 = N_TOK // SUBLANES
    in_specs = [tile(a_all.shape[1])]
    args = [a_all]
    if with_conv:
        cw = 3 * SC_W
        in_specs += [
            pl.BlockSpec((TM, cw), lambda i: (i, 1)),
            pl.BlockSpec((SUBLANES, cw), lambda i: (jnp.maximum(i * rows8 - 1, 0), 1)),
            pl.BlockSpec((SUBLANES, cw), lambda i: (jnp.minimum((i + 1) * rows8, n8 - 1), 1)),
            full((3, SC_W)),
        ]
        args += [b_src, b_src, b_src, conv_w]
    else:
        in_specs += [tile(b_src.shape[1])]
        args += [b_src]
    in_specs += [
        tile(D_MODEL),
        pl.BlockSpec((None, 6, D_MODEL), lambda i: (_mod_row(i), 0, 0)),
        full((1, D_MODEL)),
        full((D_MODEL, D_MODEL)),
        full((2, D_MODEL, LANES)),
        full((1, LANES)),
    ]
    args += [x_all, mod_l, g2.reshape(1, D_MODEL), wo_bf, rw2, rb]
    out_shape = (
        jax.ShapeDtypeStruct((N_TOK, D_MODEL), F32),
        jax.ShapeDtypeStruct((N_TOK, D_MODEL), F32),
        jax.ShapeDtypeStruct((N_TOK, LANES), F32),
        jax.ShapeDtypeStruct((N_TOK, LANES), jnp.int32),
        jax.ShapeDtypeStruct((N_TILES, SUBLANES, LANES), jnp.int32),
    )
    out_specs = (tile(D_MODEL), tile(D_MODEL), tile(LANES), tile(LANES),
                 pl.BlockSpec((None, SUBLANES, LANES), lambda i: (i, 0, 0)))
    return pl.pallas_call(
        functools.partial(_post_kernel, with_conv=with_conv),
        out_shape=out_shape,
        grid=(N_TILES,),
        in_specs=in_specs,
        out_specs=out_specs,
        compiler_params=pltpu.CompilerParams(
            dimension_semantics=("arbitrary",), vmem_limit_bytes=VMEM_LIMIT),
    )(*args)


ROW_SUB = D_MODEL // LANES
TILE_ASG = TM * TOP_K
RUN_BITS = TM.bit_length()
N_TAIL_BLOCKS = N_BLK - N_ASG // MOE_BLK
N_ZERO_BLOCKS = N_EXPERTS + N_TAIL_BLOCKS


def _rows_to_wide(ref, n_rows, lead=None):
    parts = []
    for j in range(ROW_SUB):
        idx = (pl.ds(j, n_rows, stride=ROW_SUB), slice(None))
        parts.append(ref[idx] if lead is None else ref[(lead,) + idx])
    return jnp.concatenate(parts, axis=1)


def _wide_to_rows(ref, val, lead=None):
    n_rows = val.shape[0]
    for j in range(ROW_SUB):
        idx = (pl.ds(j, n_rows, stride=ROW_SUB), slice(None))
        if lead is not None:
            idx = (lead,) + idx
        ref[idx] = val[:, j * LANES:(j + 1) * LANES]


def _row_window(ref, start_row, n_rows, lead=None):
    idx = (pl.ds(pl.multiple_of(start_row * ROW_SUB, ROW_SUB), n_rows * ROW_SUB), slice(None))
    return ref.at[idx] if lead is None else ref.at[(lead,) + idx]


def _run_copies(tile, rdst_ref, roff_ref, rlen_ref, make_copy):
    def per_expert(e, carry):
        r = tile * N_EXPERTS + e
        n = rlen_ref[r]
        src = roff_ref[r]
        dst = rdst_ref[r]
        for bit in reversed(range(RUN_BITS)):
            size = 1 << bit
            done = (n >> (bit + 1)) << (bit + 1)

            @pl.when(((n >> bit) & 1) == 1)
            def _():
                make_copy(src + done, dst + done, size).start()
        return carry

    lax.fori_loop(0, N_EXPERTS, per_expert, 0)


def _dispatch_kernel(rdst_ref, roff_ref, rlen_ref, zrow_ref, h2_ref, pos_ref, xs_hbm,
                     stage, zbuf, sem, zsem):
    i = pl.program_id(0)
    n = pl.num_programs(0)
    slot = i % 2

    def zero_copy(e):
        return pltpu.make_async_copy(zbuf, _row_window(xs_hbm, zrow_ref[e], MOE_BLK), zsem)

    @pl.when(i == 0)
    def _():
        zbuf[...] = jnp.zeros_like(zbuf)

        def start(e, carry):
            @pl.when(zrow_ref[e] >= 0)
            def _():
                zero_copy(e).start()
            return carry

        def wait(e, carry):
            @pl.when(zrow_ref[e] >= 0)
            def _():
                zero_copy(e).wait()
            return carry

        lax.fori_loop(0, N_ZERO_BLOCKS, start, 0)
        lax.fori_loop(0, N_ZERO_BLOCKS, wait, 0)

    def wait_tile(slot_):
        pltpu.make_async_copy(stage.at[slot_], _row_window(xs_hbm, 0, TILE_ASG), sem.at[slot_]).wait()

    @pl.when(i >= 2)
    def _():
        wait_tile(slot)

    pos = pos_ref[...]
    lane = lax.broadcasted_iota(jnp.int32, (TM, TILE_ASG), 1)
    hit = lane == pos[:, 0:1]
    for k in range(1, TOP_K):
        hit = hit | (lane == pos[:, k:k + 1])
    sorted_rows = _dot_tn(hit.astype(BF16), h2_ref[...].astype(BF16))
    _wide_to_rows(stage, sorted_rows, lead=slot)

    _run_copies(i, rdst_ref, roff_ref, rlen_ref,
                lambda s, d, nr: pltpu.make_async_copy(_row_window(stage, s, nr, lead=slot),
                                                       _row_window(xs_hbm, d, nr), sem.at[slot]))

    @pl.when(i == n - 1)
    def _():
        wait_tile(slot)
        wait_tile(1 - slot)


def _moe_dispatch(tables, h2, pos):
    rdst, roff, rlen, zrow = tables
    grid_spec = pltpu.PrefetchScalarGridSpec(
        num_scalar_prefetch=4,
        grid=(N_TILES,),
        in_specs=[
            pl.BlockSpec((TM, D_MODEL), lambda i, *_: (i, 0)),
            pl.BlockSpec((TM, LANES), lambda i, *_: (i, 0)),
        ],
        out_specs=pl.BlockSpec(memory_space=pl.ANY),
        scratch_shapes=[
            pltpu.VMEM((2, TILE_ASG * ROW_SUB, LANES), F32),
            pltpu.VMEM((MOE_BLK * ROW_SUB, LANES), F32),
            pltpu.SemaphoreType.DMA((2,)),
            pltpu.SemaphoreType.DMA(()),
        ],
    )
    return pl.pallas_call(
        _dispatch_kernel,
        out_shape=jax.ShapeDtypeStruct((N_SLOT * ROW_SUB, LANES), F32),
        grid_spec=grid_spec,
        compiler_params=pltpu.CompilerParams(
            dimension_semantics=("arbitrary",), vmem_limit_bytes=VMEM_LIMIT, has_side_effects=True),
    )(rdst, roff, rlen, zrow, h2, pos)


def _ffn_kernel(bexp_ref, nused_ref, xs_ref, w1_ref, b1_ref, w2_ref, b2_ref, ys_ref, w1b, w2b):
    b = pl.program_id(0)

    @pl.when(b < nused_ref[0])
    def _():
        prev = jnp.maximum(b - 1, 0)

        @pl.when((b == 0) | (bexp_ref[b] != bexp_ref[prev]))
        def _():
            w1b[...] = w1_ref[...].astype(BF16)
            w2b[...] = w2_ref[...].astype(BF16)

        x = _rows_to_wide(xs_ref, MOE_BLK).astype(BF16)
        u = _dot(x, w1b[...]) + b1_ref[...]
        glu = jnp.minimum(u[:, :D_FF], SWIGLU_LIMIT)
        lin = jnp.clip(u[:, D_FF:], -SWIGLU_LIMIT, SWIGLU_LIMIT)
        act = glu * _sigmoid_pair(SWIGLU_ALPHA * glu)[0] * (lin + 1.0)
        _wide_to_rows(ys_ref, _dot(act.astype(BF16), w2b[...]) + b2_ref[...])

    @pl.when(b >= nused_ref[0])
    def _():
        ys_ref[...] = jnp.zeros_like(ys_ref)


def _moe_experts(blk_expert, n_used, xs, w1, b1, w2, b2):
    def blk(b, be, nu):
        return (b, 0)

    def wsel(b, be, nu):
        return (be[jnp.minimum(b, nu[0] - 1)], 0, 0)

    grid_spec = pltpu.PrefetchScalarGridSpec(
        num_scalar_prefetch=2,
        grid=(N_BLK,),
        in_specs=[
            pl.BlockSpec((MOE_BLK * ROW_SUB, LANES), blk),
            pl.BlockSpec((None, D_MODEL, 2 * D_FF), wsel),
            pl.BlockSpec((None, 1, 2 * D_FF), wsel),
            pl.BlockSpec((None, D_FF, D_MODEL), wsel),
            pl.BlockSpec((None, 1, D_MODEL), wsel),
        ],
        out_specs=pl.BlockSpec((MOE_BLK * ROW_SUB, LANES), blk),
        scratch_shapes=[
            pltpu.VMEM((D_MODEL, 2 * D_FF), BF16),
            pltpu.VMEM((D_FF, D_MODEL), BF16),
        ],
    )
    return pl.pallas_call(
        _ffn_kernel,
        out_shape=jax.ShapeDtypeStruct((N_SLOT * ROW_SUB, LANES), F32),
        grid_spec=grid_spec,
        compiler_params=pltpu.CompilerParams(
            dimension_semantics=("arbitrary",), vmem_limit_bytes=VMEM_LIMIT),
    )(blk_expert, n_used, xs, w1, b1.reshape(N_EXPERTS, 1, 2 * D_FF), w2, b2.reshape(N_EXPERTS, 1, D_MODEL))


def _combine_kernel(rdst_ref, roff_ref, rlen_ref, ys_hbm, gates_ref, pos_ref, x_ref, mod_ref, g_ref, *rest,
                    final):
    if final:
        yp_ref, ysm_ref, stage, sem = rest
    else:
        o_ref, stage, sem = rest
    i = pl.program_id(0)
    n = pl.num_programs(0)
    slot = i % 2

    def fetch(tile, slot_):
        _run_copies(tile, rdst_ref, roff_ref, rlen_ref,
                    lambda s, d, nr: pltpu.make_async_copy(_row_window(ys_hbm, d, nr),
                                                           _row_window(stage, s, nr, lead=slot_), sem.at[slot_]))

    @pl.when(i == 0)
    def _():
        fetch(0, 0)

    pltpu.make_async_copy(_row_window(ys_hbm, 0, TILE_ASG), stage.at[slot], sem.at[slot]).wait()

    @pl.when(i + 1 < n)
    def _():
        fetch(i + 1, 1 - slot)

    pos = pos_ref[...]
    gates = gates_ref[...]
    lane = lax.broadcasted_iota(jnp.int32, (TM, TILE_ASG), 1)
    gmat = jnp.zeros((TM, TILE_ASG), F32)
    for k in range(TOP_K):
        gmat = jnp.where(lane == pos[:, k:k + 1], gates[:, k:k + 1], gmat)
    g_hi = gmat.astype(BF16)
    g_lo = (gmat - g_hi.astype(F32)).astype(BF16)
    cols = []
    for jj in range(ROW_SUB // 2):
        s = jnp.concatenate([stage[slot, pl.ds(2 * jj, TILE_ASG, stride=ROW_SUB), :],
                             stage[slot, pl.ds(2 * jj + 1, TILE_ASG, stride=ROW_SUB), :]], axis=1)
        s_hi = s.astype(BF16)
        s_lo = (s - s_hi.astype(F32)).astype(BF16)
        cols.append(_dot(g_hi, s_hi) + _dot(g_hi, s_lo) + _dot(g_lo, s_hi))
    moe = jnp.concatenate(cols, axis=1)
    x = x_ref[...] + mod_ref[5:6, :] * moe
    if final:
        x = _rms(x) * g_ref[...]

        @pl.when(i < PROMPT_TILES)
        def _():
            yp_ref[...] = x

        @pl.when(i >= PROMPT_TILES)
        def _():
            ysm_ref[...] = x
    else:
        o_ref[...] = x


def _moe_combine(tables, ys, gates, pos, x_all, mod_l, final_g, final):
    rdst, roff, rlen, _ = tables
    tile = pl.BlockSpec((TM, D_MODEL), lambda i, *_: (i, 0))
    if final:
        out_shape = (jax.ShapeDtypeStruct((N_PROMPT, D_MODEL), F32), jax.ShapeDtypeStruct((N_SAMPLE, D_MODEL), F32))
        out_specs = (pl.BlockSpec((TM, D_MODEL), lambda i, *_: (jnp.minimum(i, PROMPT_TILES - 1), 0)),
                     pl.BlockSpec((TM, D_MODEL), lambda i, *_: (jnp.maximum(i - PROMPT_TILES, 0), 0)))
    else:
        out_shape = jax.ShapeDtypeStruct((N_TOK, D_MODEL), F32)
        out_specs = tile
    grid_spec = pltpu.PrefetchScalarGridSpec(
        num_scalar_prefetch=3,
        grid=(N_TILES,),
        in_specs=[
            pl.BlockSpec(memory_space=pl.ANY),
            pl.BlockSpec((TM, LANES), lambda i, *_: (i, 0)),
            pl.BlockSpec((TM, LANES), lambda i, *_: (i, 0)),
            tile,
            pl.BlockSpec((None, 6, D_MODEL), lambda i, *_: (_mod_row(i), 0, 0)),
            pl.BlockSpec((1, D_MODEL), lambda i, *_: (0, 0)),
        ],
        out_specs=out_specs,
        scratch_shapes=[
            pltpu.VMEM((2, TILE_ASG * ROW_SUB, LANES), F32),
            pltpu.SemaphoreType.DMA((2,)),
        ],
    )
    return pl.pallas_call(
        functools.partial(_combine_kernel, final=final),
        out_shape=out_shape,
        grid_spec=grid_spec,
        compiler_params=pltpu.CompilerParams(
            dimension_semantics=("arbitrary",), vmem_limit_bytes=VMEM_LIMIT),
    )(rdst, roff, rlen, ys, gates, pos, x_all, mod_l, final_g.reshape(1, D_MODEL))


def _moe_layout(tile_cnt):
    cnt = tile_cnt[:, 0, :N_EXPERTS]
    total = jnp.sum(cnt, axis=0)
    padded = (total + MOE_BLK - 1) // MOE_BLK * MOE_BLK
    pad_end = jnp.cumsum(padded)
    pad_start = pad_end - padded
    before = jnp.cumsum(cnt, axis=0) - cnt
    run_dst = (pad_start[None, :] + before).reshape(-1).astype(jnp.int32)
    run_off = (jnp.cumsum(cnt, axis=1) - cnt).reshape(-1).astype(jnp.int32)
    run_len = cnt.reshape(-1).astype(jnp.int32)
    last_blk = jnp.where((padded > 0) & (total < padded), pad_end - MOE_BLK, -1)
    tail = pad_end[-1] + jnp.arange(N_TAIL_BLOCKS) * MOE_BLK
    zero_row = jnp.concatenate([last_blk, jnp.where(tail < N_SLOT, tail, -1)]).astype(jnp.int32)
    blk_start = jnp.arange(N_BLK, dtype=jnp.int32) * MOE_BLK
    blk_expert = jnp.minimum(jnp.sum((blk_start[:, None] >= pad_end[None, :]).astype(jnp.int32), axis=1),
                             N_EXPERTS - 1).astype(jnp.int32)
    n_used = (pad_end[-1:] // MOE_BLK).astype(jnp.int32)
    return (run_dst, run_off, run_len, zero_row), blk_expert, n_used


def kernel(x_prompt, x_sample, cache_attn_k, cache_attn_v, state_hgrn, c, c_ctx, ada_w, ada_b, norm1_g, norm2_g, final_g, ab_w_in, ab_w_out, na_rpb, sc_conv_w, cd_w_in, cd_w_out, hg_lb_logits, router_w, router_b, moe_w1, moe_b1, moe_w2, moe_b2):
    x_all = jnp.concatenate([x_prompt.reshape(N_PROMPT, D_MODEL), x_sample.reshape(N_SAMPLE, D_MODEL)], axis=0)
    m16 = jnp.zeros((N_MOD_ROWS, D_MODEL), F32).at[0].set(c_ctx).at[1:1 + DEC_BATCH].set(c)
    mod = _modulation(m16, ada_w, ada_b)

    lb_sm = jax.nn.softmax(hg_lb_logits.astype(F32), axis=0)
    lb_all = jnp.cumsum(lb_sm, axis=0) - lb_sm[0:1]

    rw_pad = jnp.zeros((DEPTH, D_MODEL, LANES), F32).at[:, :, :N_EXPERTS].set(router_w)
    rw_hi = rw_pad.astype(BF16)
    rw_lo = (rw_pad - rw_hi.astype(F32)).astype(BF16)
    rw2 = jnp.stack([rw_hi, rw_lo], axis=1)
    rb_pad = jnp.zeros((DEPTH, 1, LANES), F32).at[:, 0, :N_EXPERTS].set(router_b)

    new_k = new_v = new_s = None
    for l in range(DEPTH):
        j = l // 2
        if l % 2 == 0:
            p_all = _pre_project(x_all, mod[l], norm1_g[l], ab_w_in[j].astype(BF16))
            att_p = _context_attention(p_all)
            ck = cache_attn_k[:, j].transpose(0, 2, 1, 3).reshape(DEC_BATCH, PAST_LEN, NA_W)
            cv = cache_attn_v[:, j].transpose(0, 2, 1, 3).reshape(DEC_BATCH, PAST_LEN, NA_W)
            att_s = _neighbourhood_attention(p_all, ck, cv, _natten_bias(na_rpb[j]))
            a_all = jnp.concatenate([att_p, att_s], axis=0)
            kp = p_all[:N_PROMPT, NA_W:2 * NA_W].reshape(BATCH, SEQ, NA_HEADS, NA_HEAD_DIM).transpose(0, 2, 1, 3)
            vp = p_all[:N_PROMPT, 2 * NA_W:3 * NA_W].reshape(BATCH, SEQ, NA_HEADS, NA_HEAD_DIM).transpose(0, 2, 1, 3)
            new_k, new_v = kp[:, None], vp[:, None]
            outs = _post_mix(a_all, p_all, sc_conv_w[j], x_all, mod[l], norm2_g[l],
                             ab_w_out[j].astype(BF16), rw2[l], rb_pad[l], with_conv=True)
        else:
            p_all = _pre_project(x_all, mod[l], norm1_g[l], cd_w_in[j].astype(BF16))
            fn_p = _fourier_mix(p_all, 0, BATCH, SEQ)
            fn_s = _fourier_mix(p_all, N_PROMPT, DEC_BATCH, DEC_SEQ)
            s0_p = jnp.zeros((BATCH, 2, HG_HEADS, HG_DK, HG_DK), F32)
            s0_s = jnp.swapaxes(state_hgrn[:, j].astype(F32), -1, -2)
            hg_p, sfin_p = _hgrn2_bidir(p_all, lb_all[l], s0_p, 0, BATCH, SEQ)
            hg_s, _ = _hgrn2_bidir(p_all, lb_all[l], s0_s, N_PROMPT, DEC_BATCH, DEC_SEQ)
            new_s = jnp.swapaxes(sfin_p, -1, -2)[:, None].astype(x_prompt.dtype)
            a_all = jnp.concatenate([fn_p, fn_s], axis=0)
            b_all = jnp.concatenate([hg_p, hg_s], axis=0)
            outs = _post_mix(a_all, b_all, None, x_all, mod[l], norm2_g[l],
                             cd_w_out[j].astype(BF16), rw2[l], rb_pad[l], with_conv=False)
        x_mid, h2, gates, pos, tile_cnt = outs
        tables, blk_expert, n_used = _moe_layout(tile_cnt)
        xs = _moe_dispatch(tables, h2, pos)
        ys = _moe_experts(blk_expert, n_used, xs, moe_w1[l], moe_b1[l], moe_w2[l], moe_b2[l])
        x_all = _moe_combine(tables, ys, gates, pos, x_mid, mod[l], final_g, final=(l == DEPTH - 1))

    y_prompt, y_sample = x_all
    return (y_prompt.reshape(BATCH, SEQ, D_MODEL), y_sample.reshape(DEC_BATCH, DEC_SEQ, D_MODEL),
            new_k, new_v, new_s)
```

```python
import functools

import numpy as np
import jax
import jax.numpy as jnp
from jax import lax
from jax.experimental import pallas as pl
from jax.experimental.pallas import tpu as pltpu

F32 = jnp.float32
BF16 = jnp.bfloat16

D_MODEL = 1024
BATCH = 16
SEQ = 256
DEC_BATCH = 8
DEC_SEQ = 2048
PAST_LEN = 512
DEPTH = 2
GRID_W = 64
GRID_ROWS = DEC_SEQ // GRID_W
NA_HEADS = 8
NA_HEAD_DIM = 64
NA_W = NA_HEADS * NA_HEAD_DIM
SC_W = D_MODEL - NA_W
NA_SCALE = NA_HEAD_DIM ** -0.5
MAX_WR = 8
WIN_W = 16
FN_W = 512
FN_GROUPS = 4
FN_GROUP_W = 128
HG_W = 512
HG_HEADS = 4
HG_DK = 128
HG_CHUNK = 64
HG_SUB = 16
N_EXPERTS = 32
TOP_K = 4
D_FF = D_MODEL
SWIGLU_LIMIT = 7.0
SWIGLU_ALPHA = 1.702
EPS = 1e-6
NEG_INF = -1e30

N_PROMPT = BATCH * SEQ
N_SAMPLE = DEC_BATCH * DEC_SEQ
N_TOK = N_PROMPT + N_SAMPLE
TM = 256
N_TILES = N_TOK // TM
PROMPT_TILES = N_PROMPT // TM
TILES_PER_DEC_SEQ = DEC_SEQ // TM
N_MOD_ROWS = 16
LANES = 128
SUBLANES = 8

MOE_BLK = 512
N_ASG = N_TOK * TOP_K
N_BLK = N_ASG // MOE_BLK + N_EXPERTS
N_SLOT = N_BLK * MOE_BLK
VMEM_LIMIT = 56 * 1024 * 1024


def _mod_row(i):
    return jnp.where(i < PROMPT_TILES, 0, 1 + (i - PROMPT_TILES) // TILES_PER_DEC_SEQ)


def _sigmoid_pair(z):
    e = jnp.exp(-jnp.abs(z))
    r = 1.0 / (1.0 + e)
    er = e * r
    pos = z >= 0
    return jnp.where(pos, r, er), jnp.where(pos, er, r)


def _silu(x):
    return x * _sigmoid_pair(x)[0]


def _rms(x):
    return x * lax.rsqrt(jnp.mean(x * x, axis=-1, keepdims=True) + EPS)


def _dot(a, b):
    return jnp.dot(a, b, preferred_element_type=F32)


def _dot_nt(a, b):
    return lax.dot_general(a, b, (((1,), (1,)), ((), ())), preferred_element_type=F32)


def _dot_tn(a, b):
    return lax.dot_general(a, b, (((0,), (0,)), ((), ())), preferred_element_type=F32)


def _split3(x):
    hi = x.astype(BF16)
    r1 = x - hi.astype(F32)
    mid = r1.astype(BF16)
    lo = (r1 - mid.astype(F32)).astype(BF16)
    return hi, mid, lo


MOD_TN = 1024


def _mod_kernel(m_ref, w_ref, b_ref, o_ref):
    a = _silu(m_ref[...]).astype(BF16)
    o_ref[...] = _dot(a, w_ref[...].astype(BF16)) + b_ref[...]


def _modulation(m16, ada_w, ada_b):
    n_out = 6 * D_MODEL
    out = pl.pallas_call(
        _mod_kernel,
        out_shape=jax.ShapeDtypeStruct((DEPTH, N_MOD_ROWS, n_out), F32),
        grid=(DEPTH, n_out // MOD_TN),
        in_specs=[
            pl.BlockSpec((N_MOD_ROWS, D_MODEL), lambda l, j: (0, 0)),
            pl.BlockSpec((None, D_MODEL, MOD_TN), lambda l, j: (l, 0, j)),
            pl.BlockSpec((None, 1, MOD_TN), lambda l, j: (l, 0, j)),
        ],
        out_specs=pl.BlockSpec((None, N_MOD_ROWS, MOD_TN), lambda l, j: (l, 0, j)),
        compiler_params=pltpu.CompilerParams(
            dimension_semantics=("arbitrary", "arbitrary"), vmem_limit_bytes=VMEM_LIMIT),
    )(m16, ada_w, ada_b.reshape(DEPTH, 1, n_out))
    return out.reshape(DEPTH, N_MOD_ROWS, 6, D_MODEL)


def _pre_kernel(x_ref, mod_ref, g_ref, w_ref, o_ref):
    y = _rms(x_ref[...]) * g_ref[...]
    h = y * (1.0 + mod_ref[1:2, :]) + mod_ref[0:1, :]
    o_ref[...] = _dot(h.astype(BF16), w_ref[...])


def _pre_project(x_all, mod_l, g, w_bf):
    n_out = w_bf.shape[1]
    return pl.pallas_call(
        _pre_kernel,
        out_shape=jax.ShapeDtypeStruct((N_TOK, n_out), F32),
        grid=(N_TILES,),
        in_specs=[
            pl.BlockSpec((TM, D_MODEL), lambda i: (i, 0)),
            pl.BlockSpec((None, 6, D_MODEL), lambda i: (_mod_row(i), 0, 0)),
            pl.BlockSpec((1, D_MODEL), lambda i: (0, 0)),
            pl.BlockSpec((D_MODEL, n_out), lambda i: (0, 0)),
        ],
        out_specs=pl.BlockSpec((TM, n_out), lambda i: (i, 0)),
        compiler_params=pltpu.CompilerParams(
            dimension_semantics=("arbitrary",), vmem_limit_bytes=VMEM_LIMIT),
    )(x_all, mod_l, g.reshape(1, D_MODEL), w_bf)


def _softmax_pv(parts):
    m = None
    for s, _ in parts:
        mi = jnp.max(s, axis=-1, keepdims=True)
        m = mi if m is None else jnp.maximum(m, mi)
    acc = None
    den = None
    for s, v in parts:
        p = jnp.exp(s - m)
        d = jnp.sum(p, axis=-1, keepdims=True)
        o = _dot(p.astype(BF16), v)
        acc = o if acc is None else acc + o
        den = d if den is None else den + d
    return acc / den


def _ctx_attn_kernel(q_ref, k_ref, v_ref, o_ref):
    q = q_ref[...]
    kb = k_ref[...].astype(BF16)
    vb = v_ref[...].astype(BF16)
    lane = lax.broadcasted_iota(jnp.int32, q.shape, 1)
    out = jnp.zeros(q.shape, F32)
    for h in range(2):
        hm = (lane >= NA_HEAD_DIM) if h == 1 else (lane < NA_HEAD_DIM)
        qm = jnp.where(hm, q, 0.0).astype(BF16)
        s = _dot_nt(qm, kb) * NA_SCALE
        out = jnp.where(hm, _softmax_pv([(s, vb)]), out)
    o_ref[...] = out


def _context_attention(p_all):
    n_hp = NA_HEADS // 2
    return pl.pallas_call(
        _ctx_attn_kernel,
        out_shape=jax.ShapeDtypeStruct((N_PROMPT, NA_W), F32),
        grid=(BATCH, n_hp),
        in_specs=[
            pl.BlockSpec((SEQ, LANES), lambda b, hp: (b, hp)),
            pl.BlockSpec((SEQ, LANES), lambda b, hp: (b, n_hp + hp)),
            pl.BlockSpec((SEQ, LANES), lambda b, hp: (b, 2 * n_hp + hp)),
        ],
        out_specs=pl.BlockSpec((SEQ, LANES), lambda b, hp: (b, hp)),
        compiler_params=pltpu.CompilerParams(
            dimension_semantics=("arbitrary", "arbitrary"), vmem_limit_bytes=VMEM_LIMIT),
    )(p_all, p_all, p_all)


N_BIAS_VAR = MAX_WR
WIN_KEYS = MAX_WR * GRID_W


def _natten_kernel(q_ref, k_ref, v_ref, ck_ref, cv_ref, bias_ref, o_ref):
    ckb = ck_ref[...].astype(BF16)
    cvb = cv_ref[...].astype(BF16)
    lane = lax.broadcasted_iota(jnp.int32, (GRID_W, LANES), 1)

    def row(r, carry):
        r0 = jnp.clip(r - MAX_WR // 2, 0, GRID_ROWS - MAX_WR)
        var = r0 - r + (MAX_WR - 1)
        q = q_ref[pl.ds(pl.multiple_of(r * GRID_W, GRID_W), GRID_W), :]
        kw = k_ref[pl.ds(pl.multiple_of(r0 * GRID_W, GRID_W), WIN_KEYS), :].astype(BF16)
        vw = v_ref[pl.ds(pl.multiple_of(r0 * GRID_W, GRID_W), WIN_KEYS), :].astype(BF16)
        out = jnp.zeros((GRID_W, LANES), F32)
        for h in range(2):
            hm = (lane >= NA_HEAD_DIM) if h == 1 else (lane < NA_HEAD_DIM)
            qm = jnp.where(hm, q, 0.0).astype(BF16)
            s_loc = _dot_nt(qm, kw) * NA_SCALE + bias_ref[h, var]
            s_ctx = _dot_nt(qm, ckb) * NA_SCALE
            out = jnp.where(hm, _softmax_pv([(s_loc, vw), (s_ctx, cvb)]), out)
        o_ref[pl.ds(pl.multiple_of(r * GRID_W, GRID_W), GRID_W), :] = out
        return carry

    lax.fori_loop(0, GRID_ROWS, row, 0, unroll=2)


def _natten_bias(rpb):
    qc = np.arange(GRID_W)[:, None]
    kc = np.arange(GRID_W)[None, :]
    ws = np.clip(qc - WIN_W // 2, 0, GRID_W - WIN_W)
    valid = (kc >= ws) & (kc < ws + WIN_W)
    dc = np.clip(kc - qc + WIN_W - 1, 0, 2 * WIN_W - 2)
    t = rpb.astype(F32)[:, :, dc]
    t = jnp.where(jnp.asarray(valid)[None, None], t, NEG_INF)
    dr = np.arange(N_BIAS_VAR)[:, None] + np.arange(MAX_WR)[None, :]
    slab = t[:, dr]
    return slab.transpose(0, 1, 3, 2, 4).reshape(NA_HEADS, N_BIAS_VAR, GRID_W, WIN_KEYS)


def _neighbourhood_attention(p_all, ck, cv, bias):
    n_hp = NA_HEADS // 2
    off = N_PROMPT // DEC_SEQ
    return pl.pallas_call(
        _natten_kernel,
        out_shape=jax.ShapeDtypeStruct((N_SAMPLE, NA_W), F32),
        grid=(DEC_BATCH, n_hp),
        in_specs=[
            pl.BlockSpec((DEC_SEQ, LANES), lambda b, hp: (off + b, hp)),
            pl.BlockSpec((DEC_SEQ, LANES), lambda b, hp: (off + b, n_hp + hp)),
            pl.BlockSpec((DEC_SEQ, LANES), lambda b, hp: (off + b, 2 * n_hp + hp)),
            pl.BlockSpec((None, PAST_LEN, LANES), lambda b, hp: (b, 0, hp)),
            pl.BlockSpec((None, PAST_LEN, LANES), lambda b, hp: (b, 0, hp)),
            pl.BlockSpec((2, N_BIAS_VAR, GRID_W, WIN_KEYS), lambda b, hp: (hp, 0, 0, 0)),
        ],
        out_specs=pl.BlockSpec((DEC_SEQ, LANES), lambda b, hp: (b, hp)),
        compiler_params=pltpu.CompilerParams(
            dimension_semantics=("arbitrary", "arbitrary"), vmem_limit_bytes=VMEM_LIMIT),
    )(p_all, p_all, p_all, ck, cv, bias)


def _dft_tables(t):
    j = np.arange(t, dtype=np.int64)
    ang = 2.0 * np.pi * ((j[:, None] * j[None, :]) % t).astype(np.float64) / t
    return np.cos(ang), np.sin(ang)


def _fourier_kernel(z_ref, lt_ref, rc_ref, rs_ref, o_ref, ab_ref, *, t_len, scale):
    @pl.when(pl.program_id(1) == 0)
    def _():
        z = z_ref[...].astype(BF16)
        for g in range(FN_GROUPS):
            zg = z[:, g * FN_GROUP_W:(g + 1) * FN_GROUP_W]
            ab_ref[0:t_len, g * FN_GROUP_W:(g + 1) * FN_GROUP_W] = _dot(zg, rc_ref[...]).astype(BF16)
            ab_ref[t_len:2 * t_len, g * FN_GROUP_W:(g + 1) * FN_GROUP_W] = _dot(zg, rs_ref[...]).astype(BF16)

    o_ref[...] = _dot(lt_ref[...], ab_ref[...]) * scale


def _fourier_mix(p_all, row_off, n_batch, t_len):
    ct, st = _dft_tables(t_len)
    lt = jnp.asarray(np.concatenate([ct, -st], axis=1), dtype=F32).astype(BF16)
    cc, sc = _dft_tables(FN_GROUP_W)
    rc = jnp.asarray(cc, dtype=F32).astype(BF16)
    rs = jnp.asarray(sc, dtype=F32).astype(BF16)
    tr = min(t_len, 512)
    n_rt = t_len // tr
    off = row_off // t_len
    kern = functools.partial(_fourier_kernel, t_len=t_len, scale=float((t_len * FN_GROUP_W) ** -0.5))
    return pl.pallas_call(
        kern,
        out_shape=jax.ShapeDtypeStruct((n_batch * t_len, FN_W), F32),
        grid=(n_batch, n_rt),
        in_specs=[
            pl.BlockSpec((t_len, FN_W), lambda b, i: (off + b, 0)),
            pl.BlockSpec((tr, 2 * t_len), lambda b, i: (i, 0)),
            pl.BlockSpec((FN_GROUP_W, FN_GROUP_W), lambda b, i: (0, 0)),
            pl.BlockSpec((FN_GROUP_W, FN_GROUP_W), lambda b, i: (0, 0)),
        ],
        out_specs=pl.BlockSpec((tr, FN_W), lambda b, i: (b * n_rt + i, 0)),
        scratch_shapes=[pltpu.VMEM((2 * t_len, FN_W), BF16)],
        compiler_params=pltpu.CompilerParams(
            dimension_semantics=("arbitrary", "arbitrary"), vmem_limit_bytes=VMEM_LIMIT),
    )(p_all, lt, rc, rs)


N_SUB = HG_CHUNK // HG_SUB


def _gla_chunk(q, v, z, lb, st, backward):
    c = HG_CHUNK
    sg, sgn = _sigmoid_pair(z)
    lf = jnp.log(lb + (1.0 - lb) * sg)
    kk = (1.0 - lb) * sgn
    row = lax.broadcasted_iota(jnp.int32, (c, c), 0)
    col = lax.broadcasted_iota(jnp.int32, (c, c), 1)
    tri = ((col >= row) if backward else (col <= row)).astype(BF16)
    hi, mid, lo = _split3(lf)
    b = _dot(tri, hi) + _dot(tri, mid) + _dot(tri, lo)
    o_inter = _dot_nt((q * jnp.exp(b)).astype(BF16), st.astype(BF16))

    sub_row = lax.broadcasted_iota(jnp.int32, (HG_SUB, c), 0)
    sub_col = lax.broadcasted_iota(jnp.int32, (HG_SUB, c), 1)
    att_rows = []
    for i in range(N_SUB):
        lo_r, hi_r = i * HG_SUB, (i + 1) * HG_SUB
        q_i = q[lo_r:hi_r]
        b_i = b[lo_r:hi_r]
        kk_i = kk[lo_r:hi_r]
        has_off = (i < N_SUB - 1) if backward else (i > 0)
        if has_off:
            r_i = b[hi_r:hi_r + 1] if backward else b[lo_r - 1:lo_r]
            qs = (q_i * jnp.exp(b_i - r_i)).astype(BF16)
            ks = (kk * jnp.exp(jnp.minimum(r_i - b, 0.0))).astype(BF16)
            a = _dot_nt(qs, ks)
            off_mask = (sub_col >= hi_r) if backward else (sub_col < lo_r)
            a = jnp.where(off_mask, a, 0.0)
        else:
            a = jnp.zeros((HG_SUB, c), F32)
        for s in range(HG_SUB):
            dec = jnp.exp(jnp.minimum(b_i - b_i[s:s + 1], 0.0))
            colsum = jnp.sum(q_i * dec * kk_i[s:s + 1], axis=-1, keepdims=True)
            keep = (sub_row <= s) if backward else (sub_row >= s)
            a = jnp.where((sub_col == lo_r + s) & keep, colsum, a)
        att_rows.append(a)
    att = jnp.concatenate(att_rows, axis=0)
    o = o_inter + _dot(att.astype(BF16), v.astype(BF16))

    b_last = b[0:1] if backward else b[c - 1:c]
    kd = (kk * jnp.exp(b_last - b)).astype(BF16)
    st_new = st * jnp.exp(b_last) + _dot_tn(v.astype(BF16), kd)
    return o, st_new


def _hgrn_kernel(q_ref, i_ref, zf_ref, zb_ref, g_ref, lb_ref, s0_ref, o_ref, sfin_ref, of_ref, ob_ref,
                 stf_ref, stb_ref, *, t_len):
    n_chunk = t_len // HG_CHUNK

    def sl(c):
        return pl.ds(pl.multiple_of(c * HG_CHUNK, HG_CHUNK), HG_CHUNK)

    stf_ref[...] = s0_ref[0]
    stb_ref[...] = s0_ref[1]

    def step(j, carry):
        c = n_chunk - 1 - j
        o_f, st_f = _gla_chunk(q_ref[sl(j), :], i_ref[sl(j), :], zf_ref[sl(j), :], lb_ref[0:1, :],
                               stf_ref[...], backward=False)
        o_b, st_b = _gla_chunk(q_ref[sl(c), :], i_ref[sl(c), :], zb_ref[sl(c), :], lb_ref[1:2, :],
                               stb_ref[...], backward=True)
        of_ref[sl(j), :] = o_f
        ob_ref[sl(c), :] = o_b
        stf_ref[...] = st_f
        stb_ref[...] = st_b
        return carry

    lax.fori_loop(0, n_chunk, step, 0, unroll=2)
    sfin_ref[0] = stf_ref[...]
    sfin_ref[1] = stb_ref[...]

    def finish(c, carry):
        tot = of_ref[sl(c), :] + ob_ref[sl(c), :]
        o_ref[sl(c), :] = _rms(tot) * _silu(g_ref[sl(c), :])
        return carry

    lax.fori_loop(0, n_chunk, finish, 0)


def _hgrn2_bidir(p_all, lb, s0_t, row_off, n_batch, t_len):
    off = row_off // t_len
    cb = FN_W // LANES

    def col(j):
        return pl.BlockSpec((t_len, LANES), lambda b, h: (off + b, cb + j * HG_HEADS + h))

    kern = functools.partial(_hgrn_kernel, t_len=t_len)
    state_spec = pl.BlockSpec((None, 2, None, HG_DK, HG_DK), lambda b, h: (b, 0, h, 0, 0))
    return pl.pallas_call(
        kern,
        out_shape=(jax.ShapeDtypeStruct((n_batch * t_len, HG_W), F32),
                   jax.ShapeDtypeStruct((n_batch, 2, HG_HEADS, HG_DK, HG_DK), F32)),
        grid=(n_batch, HG_HEADS),
        in_specs=[col(0), col(1), col(2), col(3), col(4),
                  pl.BlockSpec((2, LANES), lambda b, h: (0, h)),
                  state_spec],
        out_specs=(pl.BlockSpec((t_len, LANES), lambda b, h: (b, h)), state_spec),
        scratch_shapes=[pltpu.VMEM((t_len, LANES), F32), pltpu.VMEM((t_len, LANES), F32),
                        pltpu.VMEM((HG_DK, HG_DK), F32), pltpu.VMEM((HG_DK, HG_DK), F32)],
        compiler_params=pltpu.CompilerParams(
            dimension_semantics=("arbitrary", "arbitrary"), vmem_limit_bytes=VMEM_LIMIT),
    )(p_all, p_all, p_all, p_all, p_all, lb, s0_t)


def _route(h2, rw_ref, rb_ref):
    hi, mid, _ = _split3(h2)
    w_hi = rw_ref[0]
    w_lo = rw_ref[1]
    logits = _dot(hi, w_hi) + _dot(mid, w_hi) + _dot(hi, w_lo) + rb_ref[...]
    lane = lax.broadcasted_iota(jnp.int32, (TM, LANES), 1)
    work = jnp.where(lane < N_EXPERTS, logits, -jnp.inf)
    vals, hots = [], []
    for k in range(TOP_K):
        m = jnp.max(work, axis=-1, keepdims=True)
        idx = jnp.min(jnp.where(work == m, lane, LANES), axis=-1, keepdims=True)
        hot = lane == idx
        work = jnp.where(hot, -jnp.inf, work)
        vals.append(m)
        hots.append(hot)
    es = [jnp.exp(v - vals[0]) for v in vals]
    den = es[0] + es[1] + es[2] + es[3]
    gates = jnp.zeros((TM, LANES), F32)
    for k in range(TOP_K):
        gates = jnp.where(lane == k, es[k] / den, gates)
    cnt = jnp.zeros((TM, LANES), F32)
    for hot in hots:
        cnt = cnt + hot.astype(F32)
    row = lax.broadcasted_iota(jnp.int32, (TM, TM), 0)
    col = lax.broadcasted_iota(jnp.int32, (TM, TM), 1)
    before = _dot((col < row).astype(BF16), cnt.astype(BF16))
    tile_cnt = jnp.sum(cnt, axis=0, keepdims=True)
    erow = lax.broadcasted_iota(jnp.int32, (LANES, LANES), 0)
    ecol = lax.broadcasted_iota(jnp.int32, (LANES, LANES), 1)
    upper = (erow < ecol).astype(BF16)
    c_hi, c_mid, c_lo = _split3(jnp.broadcast_to(tile_cnt, (SUBLANES, LANES)))
    run_off = (_dot(c_hi, upper) + _dot(c_mid, upper) + _dot(c_lo, upper))[0:1]
    where_to = before + run_off
    pos = jnp.zeros((TM, LANES), jnp.int32)
    for k in range(TOP_K):
        pk = jnp.sum(jnp.where(hots[k], where_to, 0.0), axis=-1, keepdims=True)
        pos = jnp.where(lane == k, pk.astype(jnp.int32), pos)
    return gates, pos, tile_cnt


def _post_kernel(*refs, with_conv):
    if with_conv:
        (a_ref, c_ref, cprev_ref, cnext_ref, cw_ref, x_ref, mod_ref, g_ref, wo_ref, rw_ref, rb_ref,
         xo_ref, h2_ref, gates_ref, pos_ref, cnt_ref) = refs
    else:
        (a_ref, b_ref, x_ref, mod_ref, g_ref, wo_ref, rw_ref, rb_ref,
         xo_ref, h2_ref, gates_ref, pos_ref, cnt_ref) = refs
    i = pl.program_id(0)

    if with_conv:
        j = i - PROMPT_TILES
        is_prompt = i < PROMPT_TILES
        seq_start = is_prompt | (j % TILES_PER_DEC_SEQ == 0)
        seq_end = is_prompt | (j % TILES_PER_DEC_SEQ == TILES_PER_DEC_SEQ - 1)
        bg = c_ref[:, 0:SC_W]
        z = c_ref[:, SC_W:2 * SC_W] * c_ref[:, 2 * SC_W:3 * SC_W]
        z_prev = cprev_ref[SUBLANES - 1:SUBLANES, SC_W:2 * SC_W] * cprev_ref[SUBLANES - 1:SUBLANES, 2 * SC_W:3 * SC_W]
        z_next = cnext_ref[0:1, SC_W:2 * SC_W] * cnext_ref[0:1, 2 * SC_W:3 * SC_W]
        z_prev = jnp.where(seq_start, 0.0, z_prev)
        z_next = jnp.where(seq_end, 0.0, z_next)
        row = lax.broadcasted_iota(jnp.int32, (TM, SC_W), 0)
        zm1 = jnp.where(row == 0, z_prev, pltpu.roll(z, 1, axis=0))
        zp1 = jnp.where(row == TM - 1, z_next, pltpu.roll(z, TM - 1, axis=0))
        b_val = bg * (zm1 * cw_ref[0:1, :] + z * cw_ref[1:2, :] + zp1 * cw_ref[2:3, :])
    else:
        b_val = b_ref[...]
    half = a_ref.shape[1]
    mix = _dot(a_ref[...].astype(BF16), wo_ref[0:half, :]) + _dot(b_val.astype(BF16), wo_ref[half:, :])
    x_new = x_ref[...] + mod_ref[2:3, :] * mix
    xo_ref[...] = x_new
    h2 = _rms(x_new) * g_ref[...] * (1.0 + mod_ref[4:5, :]) + mod_ref[3:4, :]
    h2_ref[...] = h2
    gates, pos, tile_cnt = _route(h2, rw_ref, rb_ref)
    gates_ref[...] = gates
    pos_ref[...] = pos
    cnt_ref[...] = jnp.broadcast_to(tile_cnt, cnt_ref.shape).astype(jnp.int32)


def _post_mix(a_all, b_src, conv_w, x_all, mod_l, g2, wo_bf, rw2, rb, with_conv):
    tile = lambda w: pl.BlockSpec((TM, w), lambda i: (i, 0))
    full = lambda shp: pl.BlockSpec(shp, lambda i: tuple(0 for _ in shp))
    rows8 = TM // SUBLANES
    n8 = N_TOK // SUBLANES
    in_specs = [tile(a_all.shape[1])]
    args = [a_all]
    if with_conv:
        cw = 3 * SC_W
        in_specs += [
            pl.BlockSpec((TM, cw), lambda i: (i, 1)),
            pl.BlockSpec((SUBLANES, cw), lambda i: (jnp.maximum(i * rows8 - 1, 0), 1)),
            pl.BlockSpec((SUBLANES, cw), lambda i: (jnp.minimum((i + 1) * rows8, n8 - 1), 1)),
            full((3, SC_W)),
        ]
        args += [b_src, b_src, b_src, conv_w]
    else:
        in_specs += [tile(b_src.shape[1])]
        args += [b_src]
    in_specs += [
        tile(D_MODEL),
        pl.BlockSpec((None, 6, D_MODEL), lambda i: (_mod_row(i), 0, 0)),
        full((1, D_MODEL)),
        full((D_MODEL, D_MODEL)),
        full((2, D_MODEL, LANES)),
        full((1, LANES)),
    ]
    args += [x_all, mod_l, g2.reshape(1, D_MODEL), wo_bf, rw2, rb]
    out_shape = (
        jax.ShapeDtypeStruct((N_TOK, D_MODEL), F32),
        jax.ShapeDtypeStruct((N_TOK, D_MODEL), F32),
        jax.ShapeDtypeStruct((N_TOK, LANES), F32),
        jax.ShapeDtypeStruct((N_TOK, LANES), jnp.int32),
        jax.ShapeDtypeStruct((N_TILES, SUBLANES, LANES), jnp.int32),
    )
    out_specs = (tile(D_MODEL), tile(D_MODEL), tile(LANES), tile(LANES),
                 pl.BlockSpec((None, SUBLANES, LANES), lambda i: (i, 0, 0)))
    return pl.pallas_call(
        functools.partial(_post_kernel, with_conv=with_conv),
        out_shape=out_shape,
        grid=(N_TILES,),
        in_specs=in_specs,
        out_specs=out_specs,
        compiler_params=pltpu.CompilerParams(
            dimension_semantics=("arbitrary",), vmem_limit_bytes=VMEM_LIMIT),
    )(*args)


ROW_SUB = D_MODEL // LANES
TILE_ASG = TM * TOP_K
RUN_BITS = TM.bit_length()
N_TAIL_BLOCKS = N_BLK - N_ASG // MOE_BLK
N_ZERO_BLOCKS = N_EXPERTS + N_TAIL_BLOCKS


def _rows_to_wide(ref, n_rows, lead=None):
    parts = []
    for j in range(ROW_SUB):
        idx = (pl.ds(j, n_rows, stride=ROW_SUB), slice(None))
        parts.append(ref[idx] if lead is None else ref[(lead,) + idx])
    return jnp.concatenate(parts, axis=1)


def _wide_to_rows(ref, val, lead=None):
    n_rows = val.shape[0]
    for j in range(ROW_SUB):
        idx = (pl.ds(j, n_rows, stride=ROW_SUB), slice(None))
        if lead is not None:
            idx = (lead,) + idx
        ref[idx] = val[:, j * LANES:(j + 1) * LANES]


def _row_window(ref, start_row, n_rows, lead=None):
    idx = (pl.ds(pl.multiple_of(start_row * ROW_SUB, ROW_SUB), n_rows * ROW_SUB), slice(None))
    return ref.at[idx] if lead is None else ref.at[(lead,) + idx]


def _run_copies(tile, rdst_ref, roff_ref, rlen_ref, make_copy):
    def per_expert(e, carry):
        r = tile * N_EXPERTS + e
        n = rlen_ref[r]
        src = roff_ref[r]
        dst = rdst_ref[r]
        for bit in reversed(range(RUN_BITS)):
            size = 1 << bit
            done = (n >> (bit + 1)) << (bit + 1)

            @pl.when(((n >> bit) & 1) == 1)
            def _():
                make_copy(src + done, dst + done, size).start()
        return carry

    lax.fori_loop(0, N_EXPERTS, per_expert, 0)


def _dispatch_kernel(rdst_ref, roff_ref, rlen_ref, zrow_ref, h2_ref, pos_ref, xs_hbm,
                     stage, zbuf, sem, zsem):
    i = pl.program_id(0)
    n = pl.num_programs(0)
    slot = i % 2

    def zero_copy(e):
        return pltpu.make_async_copy(zbuf, _row_window(xs_hbm, zrow_ref[e], MOE_BLK), zsem)

    @pl.when(i == 0)
    def _():
        zbuf[...] = jnp.zeros_like(zbuf)

        def start(e, carry):
            @pl.when(zrow_ref[e] >= 0)
            def _():
                zero_copy(e).start()
            return carry

        def wait(e, carry):
            @pl.when(zrow_ref[e] >= 0)
            def _():
                zero_copy(e).wait()
            return carry

        lax.fori_loop(0, N_ZERO_BLOCKS, start, 0)
        lax.fori_loop(0, N_ZERO_BLOCKS, wait, 0)

    def wait_tile(slot_):
        pltpu.make_async_copy(stage.at[slot_], _row_window(xs_hbm, 0, TILE_ASG), sem.at[slot_]).wait()

    @pl.when(i >= 2)
    def _():
        wait_tile(slot)

    pos = pos_ref[...]
    lane = lax.broadcasted_iota(jnp.int32, (TM, TILE_ASG), 1)
    hit = lane == pos[:, 0:1]
    for k in range(1, TOP_K):
        hit = hit | (lane == pos[:, k:k + 1])
    sorted_rows = _dot_tn(hit.astype(BF16), h2_ref[...].astype(BF16))
    _wide_to_rows(stage, sorted_rows, lead=slot)

    _run_copies(i, rdst_ref, roff_ref, rlen_ref,
                lambda s, d, nr: pltpu.make_async_copy(_row_window(stage, s, nr, lead=slot),
                                                       _row_window(xs_hbm, d, nr), sem.at[slot]))

    @pl.when(i == n - 1)
    def _():
        wait_tile(slot)
        wait_tile(1 - slot)


def _moe_dispatch(tables, h2, pos):
    rdst, roff, rlen, zrow = tables
    grid_spec = pltpu.PrefetchScalarGridSpec(
        num_scalar_prefetch=4,
        grid=(N_TILES,),
        in_specs=[
            pl.BlockSpec((TM, D_MODEL), lambda i, *_: (i, 0)),
            pl.BlockSpec((TM, LANES), lambda i, *_: (i, 0)),
        ],
        out_specs=pl.BlockSpec(memory_space=pl.ANY),
        scratch_shapes=[
            pltpu.VMEM((2, TILE_ASG * ROW_SUB, LANES), F32),
            pltpu.VMEM((MOE_BLK * ROW_SUB, LANES), F32),
            pltpu.SemaphoreType.DMA((2,)),
            pltpu.SemaphoreType.DMA(()),
        ],
    )
    return pl.pallas_call(
        _dispatch_kernel,
        out_shape=jax.ShapeDtypeStruct((N_SLOT * ROW_SUB, LANES), F32),
        grid_spec=grid_spec,
        compiler_params=pltpu.CompilerParams(
            dimension_semantics=("arbitrary",), vmem_limit_bytes=VMEM_LIMIT, has_side_effects=True),
    )(rdst, roff, rlen, zrow, h2, pos)


def _ffn_kernel(bexp_ref, brows_ref, xs_ref, w1_ref, b1_ref, w2_ref, b2_ref, ys_ref, w1b, w2b):
    b = pl.program_id(0)
    rows = brows_ref[b]
    half = MOE_BLK // 2

    def mlp(n_rows):
        x = _rows_to_wide(xs_ref, n_rows).astype(BF16)
        u = _dot(x, w1b[...]) + b1_ref[...]
        glu = jnp.minimum(u[:, :D_FF], SWIGLU_LIMIT)
        lin = jnp.clip(u[:, D_FF:], -SWIGLU_LIMIT, SWIGLU_LIMIT)
        act = glu * _sigmoid_pair(SWIGLU_ALPHA * glu)[0] * (lin + 1.0)
        _wide_to_rows(ys_ref, _dot(act.astype(BF16), w2b[...]) + b2_ref[...])

    def zero_from(first_row):
        n = (MOE_BLK - first_row) * ROW_SUB
        ys_ref[pl.ds(first_row * ROW_SUB, n), :] = jnp.zeros((n, LANES), F32)

    @pl.when(rows > 0)
    def _():
        prev = jnp.maximum(b - 1, 0)

        @pl.when((b == 0) | (bexp_ref[b] != bexp_ref[prev]))
        def _():
            w1b[...] = w1_ref[...].astype(BF16)
            w2b[...] = w2_ref[...].astype(BF16)

    @pl.when(rows > half)
    def _():
        mlp(MOE_BLK)

    @pl.when((rows > 0) & (rows <= half))
    def _():
        mlp(half)
        zero_from(half)

    @pl.when(rows == 0)
    def _():
        zero_from(0)


def _moe_experts(layer, blk_expert, blk_rows, xs, w1, b1, w2, b2):
    def blk(b, be, br):
        return (b, 0)

    def wsel(b, be, br):
        return (layer, be[b], 0, 0)

    grid_spec = pltpu.PrefetchScalarGridSpec(
        num_scalar_prefetch=2,
        grid=(N_BLK,),
        in_specs=[
            pl.BlockSpec((MOE_BLK * ROW_SUB, LANES), blk),
            pl.BlockSpec((None, None, D_MODEL, 2 * D_FF), wsel),
            pl.BlockSpec((None, None, 1, 2 * D_FF), wsel),
            pl.BlockSpec((None, None, D_FF, D_MODEL), wsel),
            pl.BlockSpec((None, None, 1, D_MODEL), wsel),
        ],
        out_specs=pl.BlockSpec((MOE_BLK * ROW_SUB, LANES), blk),
        scratch_shapes=[
            pltpu.VMEM((D_MODEL, 2 * D_FF), BF16),
            pltpu.VMEM((D_FF, D_MODEL), BF16),
        ],
    )
    return pl.pallas_call(
        _ffn_kernel,
        out_shape=jax.ShapeDtypeStruct((N_SLOT * ROW_SUB, LANES), F32),
        grid_spec=grid_spec,
        compiler_params=pltpu.CompilerParams(
            dimension_semantics=("arbitrary",), vmem_limit_bytes=VMEM_LIMIT),
    )(blk_expert, blk_rows, xs, w1, b1, w2, b2)


def _combine_kernel(rdst_ref, roff_ref, rlen_ref, ys_hbm, gates_ref, pos_ref, x_ref, mod_ref, g_ref, *rest,
                    final):
    if final:
        yp_ref, ysm_ref, stage, sem = rest
    else:
        o_ref, stage, sem = rest
    i = pl.program_id(0)
    n = pl.num_programs(0)
    slot = i % 2

    def fetch(tile, slot_):
        _run_copies(tile, rdst_ref, roff_ref, rlen_ref,
                    lambda s, d, nr: pltpu.make_async_copy(_row_window(ys_hbm, d, nr),
                                                           _row_window(stage, s, nr, lead=slot_), sem.at[slot_]))

    @pl.when(i == 0)
    def _():
        fetch(0, 0)

    pltpu.make_async_copy(_row_window(ys_hbm, 0, TILE_ASG), stage.at[slot], sem.at[slot]).wait()

    @pl.when(i + 1 < n)
    def _():
        fetch(i + 1, 1 - slot)

    pos = pos_ref[...]
    gates = gates_ref[...]
    lane = lax.broadcasted_iota(jnp.int32, (TM, TILE_ASG), 1)
    gmat = jnp.zeros((TM, TILE_ASG), F32)
    for k in range(TOP_K):
        gmat = jnp.where(lane == pos[:, k:k + 1], gates[:, k:k + 1], gmat)
    g_hi = gmat.astype(BF16)
    g_lo = (gmat - g_hi.astype(F32)).astype(BF16)
    cols = []
    for jj in range(ROW_SUB // 2):
        s = jnp.concatenate([stage[slot, pl.ds(2 * jj, TILE_ASG, stride=ROW_SUB), :],
                             stage[slot, pl.ds(2 * jj + 1, TILE_ASG, stride=ROW_SUB), :]], axis=1)
        s_hi = s.astype(BF16)
        s_lo = (s - s_hi.astype(F32)).astype(BF16)
        cols.append(_dot(g_hi, s_hi) + _dot(g_hi, s_lo) + _dot(g_lo, s_hi))
    moe = jnp.concatenate(cols, axis=1)
    x = x_ref[...] + mod_ref[5:6, :] * moe
    if final:
        x = _rms(x) * g_ref[...]

        @pl.when(i < PROMPT_TILES)
        def _():
            yp_ref[...] = x

        @pl.when(i >= PROMPT_TILES)
        def _():
            ysm_ref[...] = x
    else:
        o_ref[...] = x


def _moe_combine(tables, ys, gates, pos, x_all, mod_l, final_g, final):
    rdst, roff, rlen, _ = tables
    tile = pl.BlockSpec((TM, D_MODEL), lambda i, *_: (i, 0))
    if final:
        out_shape = (jax.ShapeDtypeStruct((N_PROMPT, D_MODEL), F32), jax.ShapeDtypeStruct((N_SAMPLE, D_MODEL), F32))
        out_specs = (pl.BlockSpec((TM, D_MODEL), lambda i, *_: (jnp.minimum(i, PROMPT_TILES - 1), 0)),
                     pl.BlockSpec((TM, D_MODEL), lambda i, *_: (jnp.maximum(i - PROMPT_TILES, 0), 0)))
    else:
        out_shape = jax.ShapeDtypeStruct((N_TOK, D_MODEL), F32)
        out_specs = tile
    grid_spec = pltpu.PrefetchScalarGridSpec(
        num_scalar_prefetch=3,
        grid=(N_TILES,),
        in_specs=[
            pl.BlockSpec(memory_space=pl.ANY),
            pl.BlockSpec((TM, LANES), lambda i, *_: (i, 0)),
            pl.BlockSpec((TM, LANES), lambda i, *_: (i, 0)),
            tile,
            pl.BlockSpec((None, 6, D_MODEL), lambda i, *_: (_mod_row(i), 0, 0)),
            pl.BlockSpec((1, D_MODEL), lambda i, *_: (0, 0)),
        ],
        out_specs=out_specs,
        scratch_shapes=[
            pltpu.VMEM((2, TILE_ASG * ROW_SUB, LANES), F32),
            pltpu.SemaphoreType.DMA((2,)),
        ],
    )
    return pl.pallas_call(
        functools.partial(_combine_kernel, final=final),
        out_shape=out_shape,
        grid_spec=grid_spec,
        compiler_params=pltpu.CompilerParams(
            dimension_semantics=("arbitrary",), vmem_limit_bytes=VMEM_LIMIT),
    )(rdst, roff, rlen, ys, gates, pos, x_all, mod_l, final_g.reshape(1, D_MODEL))


def _moe_layout(tile_cnt):
    cnt = tile_cnt[:, 0, :N_EXPERTS]
    total = jnp.sum(cnt, axis=0)
    padded = (total + MOE_BLK - 1) // MOE_BLK * MOE_BLK
    pad_end = jnp.cumsum(padded)
    pad_start = pad_end - padded
    before = jnp.cumsum(cnt, axis=0) - cnt
    run_dst = (pad_start[None, :] + before).reshape(-1).astype(jnp.int32)
    run_off = (jnp.cumsum(cnt, axis=1) - cnt).reshape(-1).astype(jnp.int32)
    run_len = cnt.reshape(-1).astype(jnp.int32)
    last_blk = jnp.where((padded > 0) & (total < padded), pad_end - MOE_BLK, -1)
    tail = pad_end[-1] + jnp.arange(N_TAIL_BLOCKS) * MOE_BLK
    zero_row = jnp.concatenate([last_blk, jnp.where(tail < N_SLOT, tail, -1)]).astype(jnp.int32)
    blk_start = jnp.arange(N_BLK, dtype=jnp.int32) * MOE_BLK
    blk_expert = jnp.minimum(jnp.sum((blk_start[:, None] >= pad_end[None, :]).astype(jnp.int32), axis=1),
                             N_EXPERTS - 1).astype(jnp.int32)
    blk_rows = jnp.clip((pad_start + total)[blk_expert] - blk_start, 0, MOE_BLK)
    blk_rows = jnp.where(blk_start < pad_end[-1], blk_rows, 0).astype(jnp.int32)
    return (run_dst, run_off, run_len, zero_row), blk_expert, blk_rows


def kernel(x_prompt, x_sample, cache_attn_k, cache_attn_v, state_hgrn, c, c_ctx, ada_w, ada_b, norm1_g, norm2_g, final_g, ab_w_in, ab_w_out, na_rpb, sc_conv_w, cd_w_in, cd_w_out, hg_lb_logits, router_w, router_b, moe_w1, moe_b1, moe_w2, moe_b2):
    x_all = jnp.concatenate([x_prompt.reshape(N_PROMPT, D_MODEL), x_sample.reshape(N_SAMPLE, D_MODEL)], axis=0)
    m16 = jnp.zeros((N_MOD_ROWS, D_MODEL), F32).at[0].set(c_ctx).at[1:1 + DEC_BATCH].set(c)
    mod = _modulation(m16, ada_w, ada_b)

    lb_sm = jax.nn.softmax(hg_lb_logits.astype(F32), axis=0)
    lb_all = jnp.cumsum(lb_sm, axis=0) - lb_sm[0:1]

    rw_pad = jnp.zeros((DEPTH, D_MODEL, LANES), F32).at[:, :, :N_EXPERTS].set(router_w)
    rw_hi = rw_pad.astype(BF16)
    rw_lo = (rw_pad - rw_hi.astype(F32)).astype(BF16)
    rw2 = jnp.stack([rw_hi, rw_lo], axis=1)
    rb_pad = jnp.zeros((DEPTH, 1, LANES), F32).at[:, 0, :N_EXPERTS].set(router_b)

    new_k = new_v = new_s = None
    for l in range(DEPTH):
        j = l // 2
        if l % 2 == 0:
            p_all = _pre_project(x_all, mod[l], norm1_g[l], ab_w_in[j].astype(BF16))
            att_p = _context_attention(p_all)
            ck = cache_attn_k[:, j].transpose(0, 2, 1, 3).reshape(DEC_BATCH, PAST_LEN, NA_W)
            cv = cache_attn_v[:, j].transpose(0, 2, 1, 3).reshape(DEC_BATCH, PAST_LEN, NA_W)
            att_s = _neighbourhood_attention(p_all, ck, cv, _natten_bias(na_rpb[j]))
            a_all = jnp.concatenate([att_p, att_s], axis=0)
            kp = p_all[:N_PROMPT, NA_W:2 * NA_W].reshape(BATCH, SEQ, NA_HEADS, NA_HEAD_DIM).transpose(0, 2, 1, 3)
            vp = p_all[:N_PROMPT, 2 * NA_W:3 * NA_W].reshape(BATCH, SEQ, NA_HEADS, NA_HEAD_DIM).transpose(0, 2, 1, 3)
            new_k, new_v = kp[:, None], vp[:, None]
            outs = _post_mix(a_all, p_all, sc_conv_w[j], x_all, mod[l], norm2_g[l],
                             ab_w_out[j].astype(BF16), rw2[l], rb_pad[l], with_conv=True)
        else:
            p_all = _pre_project(x_all, mod[l], norm1_g[l], cd_w_in[j].astype(BF16))
            fn_p = _fourier_mix(p_all, 0, BATCH, SEQ)
            fn_s = _fourier_mix(p_all, N_PROMPT, DEC_BATCH, DEC_SEQ)
            s0_p = jnp.zeros((BATCH, 2, HG_HEADS, HG_DK, HG_DK), F32)
            s0_s = jnp.swapaxes(state_hgrn[:, j].astype(F32), -1, -2)
            hg_p, sfin_p = _hgrn2_bidir(p_all, lb_all[l], s0_p, 0, BATCH, SEQ)
            hg_s, _ = _hgrn2_bidir(p_all, lb_all[l], s0_s, N_PROMPT, DEC_BATCH, DEC_SEQ)
            new_s = jnp.swapaxes(sfin_p, -1, -2)[:, None].astype(x_prompt.dtype)
            a_all = jnp.concatenate([fn_p, fn_s], axis=0)
            b_all = jnp.concatenate([hg_p, hg_s], axis=0)
            outs = _post_mix(a_all, b_all, None, x_all, mod[l], norm2_g[l],
                             cd_w_out[j].astype(BF16), rw2[l], rb_pad[l], with_conv=False)
        x_mid, h2, gates, pos, tile_cnt = outs
        tables, blk_expert, blk_rows = _moe_layout(tile_cnt)
        xs = _moe_dispatch(tables, h2, pos)
        ys = _moe_experts(l, blk_expert, blk_rows, xs, moe_w1, moe_b1[:, :, None, :], moe_w2, moe_b2[:, :, None, :])
        x_all = _moe_combine(tables, ys, gates, pos, x_mid, mod[l], final_g, final=(l == DEPTH - 1))

    y_prompt, y_sample = x_all
    return (y_prompt.reshape(BATCH, SEQ, D_MODEL), y_sample.reshape(DEC_BATCH, DEC_SEQ, D_MODEL),
            new_k, new_v, new_s)
```

```python
import functools

import numpy as np
import jax
import jax.numpy as jnp
from jax import lax
from jax.experimental import pallas as pl
from jax.experimental.pallas import tpu as pltpu

F32 = jnp.float32
BF16 = jnp.bfloat16

D_MODEL = 1024
BATCH = 16
SEQ = 256
DEC_BATCH = 8
DEC_SEQ = 2048
PAST_LEN = 512
DEPTH = 2
GRID_W = 64
GRID_ROWS = DEC_SEQ // GRID_W
NA_HEADS = 8
NA_HEAD_DIM = 64
NA_W = NA_HEADS * NA_HEAD_DIM
SC_W = D_MODEL - NA_W
NA_SCALE = NA_HEAD_DIM ** -0.5
MAX_WR = 8
WIN_W = 16
FN_W = 512
FN_GROUPS = 4
FN_GROUP_W = 128
HG_W = 512
HG_HEADS = 4
HG_DK = 128
HG_CHUNK = 64
HG_SUB = 16
N_EXPERTS = 32
TOP_K = 4
D_FF = D_MODEL
SWIGLU_LIMIT = 7.0
SWIGLU_ALPHA = 1.702
EPS = 1e-6
NEG_INF = -1e30

N_PROMPT = BATCH * SEQ
N_SAMPLE = DEC_BATCH * DEC_SEQ
N_TOK = N_PROMPT + N_SAMPLE
TM = 256
N_TILES = N_TOK // TM
PROMPT_TILES = N_PROMPT // TM
TILES_PER_DEC_SEQ = DEC_SEQ // TM
N_MOD_ROWS = 16
LANES = 128
SUBLANES = 8

ROUTE_ROWS = 16
MOE_BLK = 512
N_ASG = N_TOK * TOP_K
N_BLK = N_ASG // MOE_BLK + N_EXPERTS
N_SLOT = N_BLK * MOE_BLK
VMEM_LIMIT = 56 * 1024 * 1024


def _mod_row(i):
    return jnp.where(i < PROMPT_TILES, 0, 1 + (i - PROMPT_TILES) // TILES_PER_DEC_SEQ)


def _sigmoid_pair(z):
    e = jnp.exp(-jnp.abs(z))
    r = 1.0 / (1.0 + e)
    er = e * r
    pos = z >= 0
    return jnp.where(pos, r, er), jnp.where(pos, er, r)


def _silu(x):
    return x * _sigmoid_pair(x)[0]


def _rms(x):
    return x * lax.rsqrt(jnp.mean(x * x, axis=-1, keepdims=True) + EPS)


def _dot(a, b):
    return jnp.dot(a, b, preferred_element_type=F32)


def _dot_nt(a, b):
    return lax.dot_general(a, b, (((1,), (1,)), ((), ())), preferred_element_type=F32)


def _dot_tn(a, b):
    return lax.dot_general(a, b, (((0,), (0,)), ((), ())), preferred_element_type=F32)


def _split3(x):
    hi = x.astype(BF16)
    r1 = x - hi.astype(F32)
    mid = r1.astype(BF16)
    lo = (r1 - mid.astype(F32)).astype(BF16)
    return hi, mid, lo


MOD_TN = 1024


def _mod_kernel(m_ref, w_ref, b_ref, o_ref):
    a = _silu(m_ref[...]).astype(BF16)
    o_ref[...] = _dot(a, w_ref[...].astype(BF16)) + b_ref[...]


def _modulation(m16, ada_w, ada_b):
    n_out = 6 * D_MODEL
    out = pl.pallas_call(
        _mod_kernel,
        out_shape=jax.ShapeDtypeStruct((DEPTH, N_MOD_ROWS, n_out), F32),
        grid=(DEPTH, n_out // MOD_TN),
        in_specs=[
            pl.BlockSpec((N_MOD_ROWS, D_MODEL), lambda l, j: (0, 0)),
            pl.BlockSpec((None, D_MODEL, MOD_TN), lambda l, j: (l, 0, j)),
            pl.BlockSpec((None, 1, MOD_TN), lambda l, j: (l, 0, j)),
        ],
        out_specs=pl.BlockSpec((None, N_MOD_ROWS, MOD_TN), lambda l, j: (l, 0, j)),
        compiler_params=pltpu.CompilerParams(
            dimension_semantics=("arbitrary", "arbitrary"), vmem_limit_bytes=VMEM_LIMIT),
    )(m16, ada_w, ada_b.reshape(DEPTH, 1, n_out))
    return out.reshape(DEPTH, N_MOD_ROWS, 6, D_MODEL)


def _by_group(i, refs):
    if len(refs) == 1:
        return refs[0][...]
    return jnp.where(i < PROMPT_TILES, refs[0][...], refs[1][...])


def _group_specs(width):
    return [pl.BlockSpec((TM, width), lambda i: (jnp.minimum(i, PROMPT_TILES - 1), 0)),
            pl.BlockSpec((TM, width), lambda i: (jnp.maximum(i - PROMPT_TILES, 0), 0))]


def _pre_kernel(*refs):
    x_refs, (mod_ref, g_ref, w_ref, o_ref) = refs[:-4], refs[-4:]
    y = _rms(_by_group(pl.program_id(0), x_refs)) * g_ref[...]
    h = y * (1.0 + mod_ref[1:2, :]) + mod_ref[0:1, :]
    o_ref[...] = _dot(h.astype(BF16), w_ref[...])


def _pre_project(x_src, mod_l, g, w_bf):
    n_out = w_bf.shape[1]
    if isinstance(x_src, tuple):
        x_specs, x_args = _group_specs(D_MODEL), list(x_src)
    else:
        x_specs, x_args = [pl.BlockSpec((TM, D_MODEL), lambda i: (i, 0))], [x_src]
    return pl.pallas_call(
        _pre_kernel,
        out_shape=jax.ShapeDtypeStruct((N_TOK, n_out), F32),
        grid=(N_TILES,),
        in_specs=x_specs + [
            pl.BlockSpec((None, 6, D_MODEL), lambda i: (_mod_row(i), 0, 0)),
            pl.BlockSpec((1, D_MODEL), lambda i: (0, 0)),
            pl.BlockSpec((D_MODEL, n_out), lambda i: (0, 0)),
        ],
        out_specs=pl.BlockSpec((TM, n_out), lambda i: (i, 0)),
        compiler_params=pltpu.CompilerParams(
            dimension_semantics=("arbitrary",), vmem_limit_bytes=VMEM_LIMIT),
    )(*x_args, mod_l, g.reshape(1, D_MODEL), w_bf)


def _softmax_pv(parts):
    m = None
    for s, _ in parts:
        mi = jnp.max(s, axis=-1, keepdims=True)
        m = mi if m is None else jnp.maximum(m, mi)
    acc = None
    den = None
    for s, v in parts:
        p = jnp.exp(s - m)
        d = jnp.sum(p, axis=-1, keepdims=True)
        o = _dot(p.astype(BF16), v)
        acc = o if acc is None else acc + o
        den = d if den is None else den + d
    return acc / den


def _ctx_attn_kernel(q_ref, k_ref, v_ref, o_ref):
    q = q_ref[...]
    kb = k_ref[...].astype(BF16)
    vb = v_ref[...].astype(BF16)
    lane = lax.broadcasted_iota(jnp.int32, q.shape, 1)
    out = jnp.zeros(q.shape, F32)
    for h in range(2):
        hm = (lane >= NA_HEAD_DIM) if h == 1 else (lane < NA_HEAD_DIM)
        qm = jnp.where(hm, q, 0.0).astype(BF16)
        s = _dot_nt(qm, kb) * NA_SCALE
        out = jnp.where(hm, _softmax_pv([(s, vb)]), out)
    o_ref[...] = out


def _context_attention(p_all):
    n_hp = NA_HEADS // 2
    return pl.pallas_call(
        _ctx_attn_kernel,
        out_shape=jax.ShapeDtypeStruct((N_PROMPT, NA_W), F32),
        grid=(BATCH, n_hp),
        in_specs=[
            pl.BlockSpec((SEQ, LANES), lambda b, hp: (b, hp)),
            pl.BlockSpec((SEQ, LANES), lambda b, hp: (b, n_hp + hp)),
            pl.BlockSpec((SEQ, LANES), lambda b, hp: (b, 2 * n_hp + hp)),
        ],
        out_specs=pl.BlockSpec((SEQ, LANES), lambda b, hp: (b, hp)),
        compiler_params=pltpu.CompilerParams(
            dimension_semantics=("arbitrary", "arbitrary"), vmem_limit_bytes=VMEM_LIMIT),
    )(p_all, p_all, p_all)


N_BIAS_VAR = MAX_WR
WIN_KEYS = MAX_WR * GRID_W


def _natten_kernel(q_ref, k_ref, v_ref, ck_ref, cv_ref, bias_ref, o_ref):
    ckb = ck_ref[...].astype(BF16)
    cvb = cv_ref[...].astype(BF16)
    lane = lax.broadcasted_iota(jnp.int32, (GRID_W, LANES), 1)

    def row(r, carry):
        r0 = jnp.clip(r - MAX_WR // 2, 0, GRID_ROWS - MAX_WR)
        var = r0 - r + (MAX_WR - 1)
        q = q_ref[pl.ds(pl.multiple_of(r * GRID_W, GRID_W), GRID_W), :]
        kw = k_ref[pl.ds(pl.multiple_of(r0 * GRID_W, GRID_W), WIN_KEYS), :].astype(BF16)
        vw = v_ref[pl.ds(pl.multiple_of(r0 * GRID_W, GRID_W), WIN_KEYS), :].astype(BF16)
        first = lane < NA_HEAD_DIM
        q2 = jnp.concatenate([jnp.where(first, q, 0.0), jnp.where(first, 0.0, q)], axis=0).astype(BF16)
        s_loc = _dot_nt(q2, kw) * NA_SCALE + bias_ref[var]
        s_ctx = _dot_nt(q2, ckb) * NA_SCALE
        o2 = _softmax_pv([(s_loc, vw), (s_ctx, cvb)])
        o_ref[pl.ds(pl.multiple_of(r * GRID_W, GRID_W), GRID_W), :] = jnp.where(first, o2[:GRID_W], o2[GRID_W:])
        return carry

    lax.fori_loop(0, GRID_ROWS, row, 0, unroll=2)


def _natten_bias(rpb):
    qc = np.arange(GRID_W)[:, None]
    kc = np.arange(GRID_W)[None, :]
    ws = np.clip(qc - WIN_W // 2, 0, GRID_W - WIN_W)
    valid = (kc >= ws) & (kc < ws + WIN_W)
    dc = np.clip(kc - qc + WIN_W - 1, 0, 2 * WIN_W - 2)
    t = rpb.astype(F32)[:, :, dc]
    t = jnp.where(jnp.asarray(valid)[None, None], t, NEG_INF)
    dr = np.arange(N_BIAS_VAR)[:, None] + np.arange(MAX_WR)[None, :]
    slab = t[:, dr]
    slab = slab.transpose(0, 1, 3, 2, 4).reshape(NA_HEADS // 2, 2, N_BIAS_VAR, GRID_W, WIN_KEYS)
    return slab.transpose(0, 2, 1, 3, 4).reshape(NA_HEADS // 2, N_BIAS_VAR, 2 * GRID_W, WIN_KEYS)


def _neighbourhood_attention(p_all, ck, cv, bias):
    n_hp = NA_HEADS // 2
    off = N_PROMPT // DEC_SEQ
    return pl.pallas_call(
        _natten_kernel,
        out_shape=jax.ShapeDtypeStruct((N_SAMPLE, NA_W), F32),
        grid=(DEC_BATCH, n_hp),
        in_specs=[
            pl.BlockSpec((DEC_SEQ, LANES), lambda b, hp: (off + b, hp)),
            pl.BlockSpec((DEC_SEQ, LANES), lambda b, hp: (off + b, n_hp + hp)),
            pl.BlockSpec((DEC_SEQ, LANES), lambda b, hp: (off + b, 2 * n_hp + hp)),
            pl.BlockSpec((None, PAST_LEN, LANES), lambda b, hp: (b, 0, hp)),
            pl.BlockSpec((None, PAST_LEN, LANES), lambda b, hp: (b, 0, hp)),
            pl.BlockSpec((None, N_BIAS_VAR, 2 * GRID_W, WIN_KEYS), lambda b, hp: (hp, 0, 0, 0)),
        ],
        out_specs=pl.BlockSpec((DEC_SEQ, LANES), lambda b, hp: (b, hp)),
        compiler_params=pltpu.CompilerParams(
            dimension_semantics=("arbitrary", "arbitrary"), vmem_limit_bytes=VMEM_LIMIT),
    )(p_all, p_all, p_all, ck, cv, bias)


def _dft_tables(t):
    j = np.arange(t, dtype=np.int64)
    ang = 2.0 * np.pi * ((j[:, None] * j[None, :]) % t).astype(np.float64) / t
    return np.cos(ang), np.sin(ang)


def _fourier_kernel(z_ref, lt_ref, rc_ref, rs_ref, o_ref, ab_ref, *, t_len, scale):
    @pl.when(pl.program_id(1) == 0)
    def _():
        z = z_ref[...].astype(BF16)
        for g in range(FN_GROUPS):
            zg = z[:, g * FN_GROUP_W:(g + 1) * FN_GROUP_W]
            ab_ref[0:t_len, g * FN_GROUP_W:(g + 1) * FN_GROUP_W] = _dot(zg, rc_ref[...]).astype(BF16)
            ab_ref[t_len:2 * t_len, g * FN_GROUP_W:(g + 1) * FN_GROUP_W] = _dot(zg, rs_ref[...]).astype(BF16)

    o_ref[...] = _dot(lt_ref[...], ab_ref[...]) * scale


def _fourier_mix(p_all, row_off, n_batch, t_len):
    ct, st = _dft_tables(t_len)
    lt = jnp.asarray(np.concatenate([ct, -st], axis=1), dtype=F32).astype(BF16)
    cc, sc = _dft_tables(FN_GROUP_W)
    rc = jnp.asarray(cc, dtype=F32).astype(BF16)
    rs = jnp.asarray(sc, dtype=F32).astype(BF16)
    tr = min(t_len, 512)
    n_rt = t_len // tr
    off = row_off // t_len
    kern = functools.partial(_fourier_kernel, t_len=t_len, scale=float((t_len * FN_GROUP_W) ** -0.5))
    return pl.pallas_call(
        kern,
        out_shape=jax.ShapeDtypeStruct((n_batch * t_len, FN_W), F32),
        grid=(n_batch, n_rt),
        in_specs=[
            pl.BlockSpec((t_len, FN_W), lambda b, i: (off + b, 0)),
            pl.BlockSpec((tr, 2 * t_len), lambda b, i: (i, 0)),
            pl.BlockSpec((FN_GROUP_W, FN_GROUP_W), lambda b, i: (0, 0)),
            pl.BlockSpec((FN_GROUP_W, FN_GROUP_W), lambda b, i: (0, 0)),
        ],
        out_specs=pl.BlockSpec((tr, FN_W), lambda b, i: (b * n_rt + i, 0)),
        scratch_shapes=[pltpu.VMEM((2 * t_len, FN_W), BF16)],
        compiler_params=pltpu.CompilerParams(
            dimension_semantics=("arbitrary", "arbitrary"), vmem_limit_bytes=VMEM_LIMIT),
    )(p_all, lt, rc, rs)


N_SUB = HG_CHUNK // HG_SUB
LOG2_E = 1.4426950408889634


def _gla_chunk(q, v, z, lb, st, backward):
    c = HG_CHUNK
    sg, sgn = _sigmoid_pair(z)
    lf = jnp.log(lb + (1.0 - lb) * sg)
    kk = (1.0 - lb) * sgn
    row = lax.broadcasted_iota(jnp.int32, (c, c), 0)
    col = lax.broadcasted_iota(jnp.int32, (c, c), 1)
    tri = ((col >= row) if backward else (col <= row)).astype(BF16)
    hi, mid, lo = _split3(lf)
    b = _dot(tri, hi) + _dot(tri, mid) + _dot(tri, lo)
    o_inter = _dot_nt((q * jnp.exp(b)).astype(BF16), st.astype(BF16))
    b2 = b * LOG2_E
    c2 = b2 - jnp.log2(kk)

    sub_row = lax.broadcasted_iota(jnp.int32, (HG_SUB, c), 0)
    sub_col = lax.broadcasted_iota(jnp.int32, (HG_SUB, c), 1)
    att_rows = []
    for i in range(N_SUB):
        lo_r, hi_r = i * HG_SUB, (i + 1) * HG_SUB
        q_i = q[lo_r:hi_r]
        b_i = b[lo_r:hi_r]
        has_off = (i < N_SUB - 1) if backward else (i > 0)
        if has_off:
            r_i = b[hi_r:hi_r + 1] if backward else b[lo_r - 1:lo_r]
            qs = (q_i * jnp.exp(b_i - r_i)).astype(BF16)
            ks = (kk * jnp.exp(jnp.minimum(r_i - b, 0.0))).astype(BF16)
            a = _dot_nt(qs, ks)
            off_mask = (sub_col >= hi_r) if backward else (sub_col < lo_r)
            a = jnp.where(off_mask, a, 0.0)
        else:
            a = jnp.zeros((HG_SUB, c), F32)
        b2_i = b2[lo_r:hi_r]
        c2_i = c2[lo_r:hi_r]
        for s in range(HG_SUB):
            colsum = jnp.sum(q_i * jnp.exp2(b2_i - c2_i[s:s + 1]), axis=-1, keepdims=True)
            keep = (sub_row <= s) if backward else (sub_row >= s)
            a = jnp.where((sub_col == lo_r + s) & keep, colsum, a)
        att_rows.append(a)
    att = jnp.concatenate(att_rows, axis=0)
    o = o_inter + _dot(att.astype(BF16), v.astype(BF16))

    b_last = b[0:1] if backward else b[c - 1:c]
    kd = (kk * jnp.exp(b_last - b)).astype(BF16)
    st_new = st * jnp.exp(b_last) + _dot_tn(v.astype(BF16), kd)
    return o, st_new


def _hgrn_kernel(q_ref, i_ref, zf_ref, zb_ref, g_ref, lb_ref, s0_ref, o_ref, sfin_ref, of_ref, ob_ref,
                 stf_ref, stb_ref, *, t_len):
    n_chunk = t_len // HG_CHUNK

    def sl(c):
        return pl.ds(pl.multiple_of(c * HG_CHUNK, HG_CHUNK), HG_CHUNK)

    stf_ref[...] = s0_ref[0]
    stb_ref[...] = s0_ref[1]

    def step(j, carry):
        c = n_chunk - 1 - j
        o_f, st_f = _gla_chunk(q_ref[sl(j), :], i_ref[sl(j), :], zf_ref[sl(j), :], lb_ref[0:1, :],
                               stf_ref[...], backward=False)
        o_b, st_b = _gla_chunk(q_ref[sl(c), :], i_ref[sl(c), :], zb_ref[sl(c), :], lb_ref[1:2, :],
                               stb_ref[...], backward=True)
        of_ref[sl(j), :] = o_f
        ob_ref[sl(c), :] = o_b
        stf_ref[...] = st_f
        stb_ref[...] = st_b
        return carry

    lax.fori_loop(0, n_chunk, step, 0, unroll=2)
    sfin_ref[0] = stf_ref[...]
    sfin_ref[1] = stb_ref[...]

    def finish(c, carry):
        tot = of_ref[sl(c), :] + ob_ref[sl(c), :]
        o_ref[sl(c), :] = _rms(tot) * _silu(g_ref[sl(c), :])
        return carry

    lax.fori_loop(0, n_chunk, finish, 0)


def _hgrn2_bidir(p_all, lb, s0_t, row_off, n_batch, t_len):
    off = row_off // t_len
    cb = FN_W // LANES

    def col(j):
        return pl.BlockSpec((t_len, LANES), lambda b, h: (off + b, cb + j * HG_HEADS + h))

    kern = functools.partial(_hgrn_kernel, t_len=t_len)
    state_spec = pl.BlockSpec((None, 2, None, HG_DK, HG_DK), lambda b, h: (b, 0, h, 0, 0))
    return pl.pallas_call(
        kern,
        out_shape=(jax.ShapeDtypeStruct((n_batch * t_len, HG_W), F32),
                   jax.ShapeDtypeStruct((n_batch, 2, HG_HEADS, HG_DK, HG_DK), F32)),
        grid=(n_batch, HG_HEADS),
        in_specs=[col(0), col(1), col(2), col(3), col(4),
                  pl.BlockSpec((2, LANES), lambda b, h: (0, h)),
                  state_spec],
        out_specs=(pl.BlockSpec((t_len, LANES), lambda b, h: (b, h)), state_spec),
        scratch_shapes=[pltpu.VMEM((t_len, LANES), F32), pltpu.VMEM((t_len, LANES), F32),
                        pltpu.VMEM((HG_DK, HG_DK), F32), pltpu.VMEM((HG_DK, HG_DK), F32)],
        compiler_params=pltpu.CompilerParams(
            dimension_semantics=("arbitrary", "arbitrary"), vmem_limit_bytes=VMEM_LIMIT),
    )(p_all, p_all, p_all, p_all, p_all, lb, s0_t)


def _route(h2, rw_ref, rb_ref):
    hi, mid, _ = _split3(h2)
    w_hi = rw_ref[0]
    w_lo = rw_ref[1]
    logits = _dot_nt(w_hi, hi) + _dot_nt(w_hi, mid) + _dot_nt(w_lo, hi) + rb_ref[:, 0:1]
    sub = lax.broadcasted_iota(jnp.int32, (N_EXPERTS, TM), 0)
    work = logits
    vals, hots = [], []
    for k in range(TOP_K):
        m = jnp.max(work, axis=0, keepdims=True)
        idx = jnp.min(jnp.where(work == m, sub, N_EXPERTS), axis=0, keepdims=True)
        hot = sub == idx
        work = jnp.where(hot, -jnp.inf, work)
        vals.append(m)
        hots.append(hot)
    es = [jnp.exp(v - vals[0]) for v in vals]
    den = es[0] + es[1] + es[2] + es[3]
    cnt = jnp.zeros((N_EXPERTS, TM), F32)
    for hot in hots:
        cnt = cnt + hot.astype(F32)
    row = lax.broadcasted_iota(jnp.int32, (TM, TM), 0)
    col = lax.broadcasted_iota(jnp.int32, (TM, TM), 1)
    before = _dot(cnt.astype(BF16), (row < col).astype(BF16))
    tile_cnt = jnp.sum(cnt, axis=1, keepdims=True)
    erow = lax.broadcasted_iota(jnp.int32, (N_EXPERTS, N_EXPERTS), 0)
    ecol = lax.broadcasted_iota(jnp.int32, (N_EXPERTS, N_EXPERTS), 1)
    lower = (ecol < erow).astype(BF16)
    run_off = _dot(lower, jnp.broadcast_to(tile_cnt, (N_EXPERTS, LANES)).astype(BF16))[:, 0:1]
    where_to = before + run_off
    rsub = lax.broadcasted_iota(jnp.int32, (ROUTE_ROWS, TM), 0)
    rows = jnp.zeros((ROUTE_ROWS, TM), F32)
    for k in range(TOP_K):
        pk = jnp.sum(jnp.where(hots[k], where_to, 0.0), axis=0, keepdims=True)
        rows = jnp.where(rsub == k, pk, rows)
        rows = jnp.where(rsub == TOP_K + k, es[k] / den, rows)
    return rows, tile_cnt


def _post_kernel(*refs, with_conv, n_x):
    refs = list(refs)
    a_refs, refs = refs[:2], refs[2:]
    if with_conv:
        (c_ref, cprev_ref, cnext_ref, cw_ref), refs = refs[:4], refs[4:]
    else:
        b_refs, refs = refs[:2], refs[2:]
    x_refs, refs = refs[:n_x], refs[n_x:]
    mod_ref, g_ref, wo_ref, rw_ref, rb_ref, xo_ref, h2_ref, rows_ref, pg_ref, cnt_ref = refs
    i = pl.program_id(0)

    if with_conv:
        j = i - PROMPT_TILES
        is_prompt = i < PROMPT_TILES
        seq_start = is_prompt | (j % TILES_PER_DEC_SEQ == 0)
        seq_end = is_prompt | (j % TILES_PER_DEC_SEQ == TILES_PER_DEC_SEQ - 1)
        bg = c_ref[:, 0:SC_W]
        z = c_ref[:, SC_W:2 * SC_W] * c_ref[:, 2 * SC_W:3 * SC_W]
        z_prev = cprev_ref[SUBLANES - 1:SUBLANES, SC_W:2 * SC_W] * cprev_ref[SUBLANES - 1:SUBLANES, 2 * SC_W:3 * SC_W]
        z_next = cnext_ref[0:1, SC_W:2 * SC_W] * cnext_ref[0:1, 2 * SC_W:3 * SC_W]
        z_prev = jnp.where(seq_start, 0.0, z_prev)
        z_next = jnp.where(seq_end, 0.0, z_next)
        row = lax.broadcasted_iota(jnp.int32, (TM, SC_W), 0)
        zm1 = jnp.where(row == 0, z_prev, pltpu.roll(z, 1, axis=0))
        zp1 = jnp.where(row == TM - 1, z_next, pltpu.roll(z, TM - 1, axis=0))
        b_val = bg * (zm1 * cw_ref[0:1, :] + z * cw_ref[1:2, :] + zp1 * cw_ref[2:3, :])
    else:
        b_val = _by_group(i, b_refs)
    a_val = _by_group(i, a_refs)
    half = a_val.shape[1]
    mix = _dot(a_val.astype(BF16), wo_ref[0:half, :]) + _dot(b_val.astype(BF16), wo_ref[half:, :])
    x_new = _by_group(i, x_refs) + mod_ref[2:3, :] * mix
    xo_ref[...] = x_new
    h2 = _rms(x_new) * g_ref[...] * (1.0 + mod_ref[4:5, :]) + mod_ref[3:4, :]
    h2_ref[...] = h2
    rows, tile_cnt = _route(h2, rw_ref, rb_ref)
    rows_ref[...] = rows
    pick = (lax.broadcasted_iota(jnp.int32, (ROUTE_ROWS, LANES), 0)
            == lax.broadcasted_iota(jnp.int32, (ROUTE_ROWS, LANES), 1)).astype(BF16)
    r_hi, r_mid, r_lo = _split3(rows)
    pg_ref[...] = _dot_tn(r_hi, pick) + _dot_tn(r_mid, pick) + _dot_tn(r_lo, pick)
    cnt_ref[...] = jnp.broadcast_to(tile_cnt, cnt_ref.shape).astype(jnp.int32)


def _post_mix(a_pair, b_src, conv_w, x_src, mod_l, g2, wo_bf, rw2, rb, with_conv):
    tile = lambda w: pl.BlockSpec((TM, w), lambda i: (i, 0))
    full = lambda shp: pl.BlockSpec(shp, lambda i: tuple(0 for _ in shp))
    rows8 = TM // SUBLANES
    n8 = N_TOK // SUBLANES
    in_specs = _group_specs(a_pair[0].shape[1])
    args = list(a_pair)
    if with_conv:
        cw = 3 * SC_W
        in_specs += [
            pl.BlockSpec((TM, cw), lambda i: (i, 1)),
            pl.BlockSpec((SUBLANES, cw), lambda i: (jnp.maximum(i * rows8 - 1, 0), 1)),
            pl.BlockSpec((SUBLANES, cw), lambda i: (jnp.minimum((i + 1) * rows8, n8 - 1), 1)),
            full((3, SC_W)),
        ]
        args += [b_src, b_src, b_src, conv_w]
    else:
        in_specs += _group_specs(b_src[0].shape[1])
        args += list(b_src)
    if isinstance(x_src, tuple):
        in_specs += _group_specs(D_MODEL)
        args += list(x_src)
        n_x = 2
    else:
        in_specs += [tile(D_MODEL)]
        args += [x_src]
        n_x = 1
    in_specs += [
        pl.BlockSpec((None, 6, D_MODEL), lambda i: (_mod_row(i), 0, 0)),
        full((1, D_MODEL)),
        full((D_MODEL, D_MODEL)),
        full((2, N_EXPERTS, D_MODEL)),
        full((N_EXPERTS, LANES)),
    ]
    args += [mod_l, g2.reshape(1, D_MODEL), wo_bf, rw2, rb]
    out_shape = (
        jax.ShapeDtypeStruct((N_TOK, D_MODEL), F32),
        jax.ShapeDtypeStruct((N_TOK, D_MODEL), F32),
        jax.ShapeDtypeStruct((N_TILES * ROUTE_ROWS, TM), F32),
        jax.ShapeDtypeStruct((N_TOK, LANES), F32),
        jax.ShapeDtypeStruct((N_TILES, N_EXPERTS, LANES), jnp.int32),
    )
    out_specs = (tile(D_MODEL), tile(D_MODEL), pl.BlockSpec((ROUTE_ROWS, TM), lambda i: (i, 0)), tile(LANES),
                 pl.BlockSpec((None, N_EXPERTS, LANES), lambda i: (i, 0, 0)))
    return pl.pallas_call(
        functools.partial(_post_kernel, with_conv=with_conv, n_x=n_x),
        out_shape=out_shape,
        grid=(N_TILES,),
        in_specs=in_specs,
        out_specs=out_specs,
        compiler_params=pltpu.CompilerParams(
            dimension_semantics=("arbitrary",), vmem_limit_bytes=VMEM_LIMIT),
    )(*args)


ROW_SUB = D_MODEL // LANES
TILE_ASG = TM * TOP_K
RUN_BITS = TM.bit_length()
N_TAIL_BLOCKS = N_BLK - N_ASG // MOE_BLK
N_ZERO_BLOCKS = N_EXPERTS + N_TAIL_BLOCKS


def _rows_to_wide(ref, n_rows, lead=None):
    parts = []
    for j in range(ROW_SUB):
        idx = (pl.ds(j, n_rows, stride=ROW_SUB), slice(None))
        parts.append(ref[idx] if lead is None else ref[(lead,) + idx])
    return jnp.concatenate(parts, axis=1)


def _wide_to_rows(ref, val, lead=None):
    n_rows = val.shape[0]
    for j in range(ROW_SUB):
        idx = (pl.ds(j, n_rows, stride=ROW_SUB), slice(None))
        if lead is not None:
            idx = (lead,) + idx
        ref[idx] = val[:, j * LANES:(j + 1) * LANES]


def _row_window(ref, start_row, n_rows, lead=None):
    idx = (pl.ds(pl.multiple_of(start_row * ROW_SUB, ROW_SUB), n_rows * ROW_SUB), slice(None))
    return ref.at[idx] if lead is None else ref.at[(lead,) + idx]


def _run_copies(tile, rdst_ref, roff_ref, rlen_ref, make_copy):
    def per_expert(e, carry):
        r = tile * N_EXPERTS + e
        n = rlen_ref[r]
        src = roff_ref[r]
        dst = rdst_ref[r]
        for bit in reversed(range(RUN_BITS)):
            size = 1 << bit
            done = (n >> (bit + 1)) << (bit + 1)

            @pl.when(((n >> bit) & 1) == 1)
            def _():
                make_copy(src + done, dst + done, size).start()
        return carry

    lax.fori_loop(0, N_EXPERTS, per_expert, 0)


def _dispatch_kernel(rdst_ref, roff_ref, rlen_ref, zrow_ref, h2_ref, rows_ref, xs_hbm,
                     stage, zbuf, sem, zsem):
    i = pl.program_id(0)
    n = pl.num_programs(0)
    slot = i % 2

    def zero_copy(e):
        return pltpu.make_async_copy(zbuf, _row_window(xs_hbm, zrow_ref[e], MOE_BLK), zsem)

    @pl.when(i == 0)
    def _():
        zbuf[...] = jnp.zeros_like(zbuf)

        def start(e, carry):
            @pl.when(zrow_ref[e] >= 0)
            def _():
                zero_copy(e).start()
            return carry

        def wait(e, carry):
            @pl.when(zrow_ref[e] >= 0)
            def _():
                zero_copy(e).wait()
            return carry

        lax.fori_loop(0, N_ZERO_BLOCKS, start, 0)
        lax.fori_loop(0, N_ZERO_BLOCKS, wait, 0)

    def wait_tile(slot_):
        pltpu.make_async_copy(stage.at[slot_], _row_window(xs_hbm, 0, TILE_ASG), sem.at[slot_]).wait()

    @pl.when(i >= 2)
    def _():
        wait_tile(slot)

    pos = rows_ref[0:TOP_K, :].astype(jnp.int32)
    sub = lax.broadcasted_iota(jnp.int32, (TILE_ASG, TM), 0)
    hit = sub == pos[0:1, :]
    for k in range(1, TOP_K):
        hit = hit | (sub == pos[k:k + 1, :])
    sorted_rows = _dot(hit.astype(BF16), h2_ref[...].astype(BF16))
    _wide_to_rows(stage, sorted_rows, lead=slot)

    _run_copies(i, rdst_ref, roff_ref, rlen_ref,
                lambda s, d, nr: pltpu.make_async_copy(_row_window(stage, s, nr, lead=slot),
                                                       _row_window(xs_hbm, d, nr), sem.at[slot]))

    @pl.when(i == n - 1)
    def _():
        wait_tile(slot)
        wait_tile(1 - slot)


def _moe_dispatch(tables, h2, route_rows):
    rdst, roff, rlen, zrow = tables
    grid_spec = pltpu.PrefetchScalarGridSpec(
        num_scalar_prefetch=4,
        grid=(N_TILES,),
        in_specs=[
            pl.BlockSpec((TM, D_MODEL), lambda i, *_: (i, 0)),
            pl.BlockSpec((ROUTE_ROWS, TM), lambda i, *_: (i, 0)),
        ],
        out_specs=pl.BlockSpec(memory_space=pl.ANY),
        scratch_shapes=[
            pltpu.VMEM((2, TILE_ASG * ROW_SUB, LANES), F32),
            pltpu.VMEM((MOE_BLK * ROW_SUB, LANES), F32),
            pltpu.SemaphoreType.DMA((2,)),
            pltpu.SemaphoreType.DMA(()),
        ],
    )
    return pl.pallas_call(
        _dispatch_kernel,
        out_shape=jax.ShapeDtypeStruct((N_SLOT * ROW_SUB, LANES), F32),
        grid_spec=grid_spec,
        compiler_params=pltpu.CompilerParams(
            dimension_semantics=("arbitrary",), vmem_limit_bytes=VMEM_LIMIT, has_side_effects=True),
    )(rdst, roff, rlen, zrow, h2, route_rows)


def _ffn_kernel(bexp_ref, brows_ref, xs_ref, w1_ref, b1_ref, w2_ref, b2_ref, ys_ref, w1b, w2b):
    b = pl.program_id(0)
    rows = brows_ref[b]
    half = MOE_BLK // 2

    def mlp(n_rows):
        x = _rows_to_wide(xs_ref, n_rows).astype(BF16)
        u = _dot(x, w1b[...]) + b1_ref[...]
        glu = jnp.minimum(u[:, :D_FF], SWIGLU_LIMIT)
        lin = jnp.clip(u[:, D_FF:], -SWIGLU_LIMIT, SWIGLU_LIMIT)
        act = glu * _sigmoid_pair(SWIGLU_ALPHA * glu)[0] * (lin + 1.0)
        _wide_to_rows(ys_ref, _dot(act.astype(BF16), w2b[...]) + b2_ref[...])

    def zero_from(first_row):
        n = (MOE_BLK - first_row) * ROW_SUB
        ys_ref[pl.ds(first_row * ROW_SUB, n), :] = jnp.zeros((n, LANES), F32)

    @pl.when(rows > 0)
    def _():
        prev = jnp.maximum(b - 1, 0)

        @pl.when((b == 0) | (bexp_ref[b] != bexp_ref[prev]))
        def _():
            w1b[...] = w1_ref[...].astype(BF16)
            w2b[...] = w2_ref[...].astype(BF16)

    @pl.when(rows > half)
    def _():
        mlp(MOE_BLK)

    @pl.when((rows > 0) & (rows <= half))
    def _():
        mlp(half)
        zero_from(half)

    @pl.when(rows == 0)
    def _():
        zero_from(0)


def _moe_experts(layer, blk_expert, blk_rows, xs, w1, b1, w2, b2):
    def blk(b, be, br):
        return (b, 0)

    def wsel(b, be, br):
        return (layer, be[b], 0, 0)

    grid_spec = pltpu.PrefetchScalarGridSpec(
        num_scalar_prefetch=2,
        grid=(N_BLK,),
        in_specs=[
            pl.BlockSpec((MOE_BLK * ROW_SUB, LANES), blk),
            pl.BlockSpec((None, None, D_MODEL, 2 * D_FF), wsel),
            pl.BlockSpec((None, None, 1, 2 * D_FF), wsel),
            pl.BlockSpec((None, None, D_FF, D_MODEL), wsel),
            pl.BlockSpec((None, None, 1, D_MODEL), wsel),
        ],
        out_specs=pl.BlockSpec((MOE_BLK * ROW_SUB, LANES), blk),
        scratch_shapes=[
            pltpu.VMEM((D_MODEL, 2 * D_FF), BF16),
            pltpu.VMEM((D_FF, D_MODEL), BF16),
        ],
    )
    return pl.pallas_call(
        _ffn_kernel,
        out_shape=jax.ShapeDtypeStruct((N_SLOT * ROW_SUB, LANES), F32),
        grid_spec=grid_spec,
        compiler_params=pltpu.CompilerParams(
            dimension_semantics=("arbitrary",), vmem_limit_bytes=VMEM_LIMIT),
    )(blk_expert, blk_rows, xs, w1, b1, w2, b2)


def _combine_kernel(rdst_ref, roff_ref, rlen_ref, ys_hbm, pg_ref, x_ref, mod_ref, g_ref, *rest,
                    final):
    if final:
        yp_ref, ysm_ref, stage, sem = rest
    else:
        o_ref, stage, sem = rest
    i = pl.program_id(0)
    n = pl.num_programs(0)
    slot = i % 2

    def fetch(tile, slot_):
        _run_copies(tile, rdst_ref, roff_ref, rlen_ref,
                    lambda s, d, nr: pltpu.make_async_copy(_row_window(ys_hbm, d, nr),
                                                           _row_window(stage, s, nr, lead=slot_), sem.at[slot_]))

    @pl.when(i == 0)
    def _():
        fetch(0, 0)

    pltpu.make_async_copy(_row_window(ys_hbm, 0, TILE_ASG), stage.at[slot], sem.at[slot]).wait()

    @pl.when(i + 1 < n)
    def _():
        fetch(i + 1, 1 - slot)

    pg = pg_ref[...]
    pos = pg[:, 0:TOP_K].astype(jnp.int32)
    gates = pg[:, TOP_K:2 * TOP_K]
    lane = lax.broadcasted_iota(jnp.int32, (TM, TILE_ASG), 1)
    gmat = jnp.zeros((TM, TILE_ASG), F32)
    for k in range(TOP_K):
        gmat = jnp.where(lane == pos[:, k:k + 1], gates[:, k:k + 1], gmat)
    g_hi = gmat.astype(BF16)
    g_lo = (gmat - g_hi.astype(F32)).astype(BF16)
    cols = []
    for jj in range(ROW_SUB // 2):
        s = jnp.concatenate([stage[slot, pl.ds(2 * jj, TILE_ASG, stride=ROW_SUB), :],
                             stage[slot, pl.ds(2 * jj + 1, TILE_ASG, stride=ROW_SUB), :]], axis=1)
        s_hi = s.astype(BF16)
        s_lo = (s - s_hi.astype(F32)).astype(BF16)
        cols.append(_dot(g_hi, s_hi) + _dot(g_hi, s_lo) + _dot(g_lo, s_hi))
    moe = jnp.concatenate(cols, axis=1)
    x = x_ref[...] + mod_ref[5:6, :] * moe
    if final:
        x = _rms(x) * g_ref[...]

        @pl.when(i < PROMPT_TILES)
        def _():
            yp_ref[...] = x

        @pl.when(i >= PROMPT_TILES)
        def _():
            ysm_ref[...] = x
    else:
        o_ref[...] = x


def _moe_combine(tables, ys, route_tok, x_all, mod_l, final_g, final):
    rdst, roff, rlen, _ = tables
    tile = pl.BlockSpec((TM, D_MODEL), lambda i, *_: (i, 0))
    if final:
        out_shape = (jax.ShapeDtypeStruct((N_PROMPT, D_MODEL), F32), jax.ShapeDtypeStruct((N_SAMPLE, D_MODEL), F32))
        out_specs = (pl.BlockSpec((TM, D_MODEL), lambda i, *_: (jnp.minimum(i, PROMPT_TILES - 1), 0)),
                     pl.BlockSpec((TM, D_MODEL), lambda i, *_: (jnp.maximum(i - PROMPT_TILES, 0), 0)))
    else:
        out_shape = jax.ShapeDtypeStruct((N_TOK, D_MODEL), F32)
        out_specs = tile
    grid_spec = pltpu.PrefetchScalarGridSpec(
        num_scalar_prefetch=3,
        grid=(N_TILES,),
        in_specs=[
            pl.BlockSpec(memory_space=pl.ANY),
            pl.BlockSpec((TM, LANES), lambda i, *_: (i, 0)),
            tile,
            pl.BlockSpec((None, 6, D_MODEL), lambda i, *_: (_mod_row(i), 0, 0)),
            pl.BlockSpec((1, D_MODEL), lambda i, *_: (0, 0)),
        ],
        out_specs=out_specs,
        scratch_shapes=[
            pltpu.VMEM((2, TILE_ASG * ROW_SUB, LANES), F32),
            pltpu.SemaphoreType.DMA((2,)),
        ],
    )
    return pl.pallas_call(
        functools.partial(_combine_kernel, final=final),
        out_shape=out_shape,
        grid_spec=grid_spec,
        compiler_params=pltpu.CompilerParams(
            dimension_semantics=("arbitrary",), vmem_limit_bytes=VMEM_LIMIT),
    )(rdst, roff, rlen, ys, route_tok, x_all, mod_l, final_g.reshape(1, D_MODEL))


def _moe_layout(tile_cnt):
    cnt = tile_cnt[:, :, 0]
    total = jnp.sum(cnt, axis=0)
    padded = (total + MOE_BLK - 1) // MOE_BLK * MOE_BLK
    pad_end = jnp.cumsum(padded)
    pad_start = pad_end - padded
    before = jnp.cumsum(cnt, axis=0) - cnt
    run_dst = (pad_start[None, :] + before).reshape(-1).astype(jnp.int32)
    run_off = (jnp.cumsum(cnt, axis=1) - cnt).reshape(-1).astype(jnp.int32)
    run_len = cnt.reshape(-1).astype(jnp.int32)
    last_blk = jnp.where((padded > 0) & (total < padded), pad_end - MOE_BLK, -1)
    tail = pad_end[-1] + jnp.arange(N_TAIL_BLOCKS) * MOE_BLK
    zero_row = jnp.concatenate([last_blk, jnp.where(tail < N_SLOT, tail, -1)]).astype(jnp.int32)
    blk_start = jnp.arange(N_BLK, dtype=jnp.int32) * MOE_BLK
    blk_expert = jnp.minimum(jnp.sum((blk_start[:, None] >= pad_end[None, :]).astype(jnp.int32), axis=1),
                             N_EXPERTS - 1).astype(jnp.int32)
    blk_rows = jnp.clip((pad_start + total)[blk_expert] - blk_start, 0, MOE_BLK)
    blk_rows = jnp.where(blk_start < pad_end[-1], blk_rows, 0).astype(jnp.int32)
    return (run_dst, run_off, run_len, zero_row), blk_expert, blk_rows


def kernel(x_prompt, x_sample, cache_attn_k, cache_attn_v, state_hgrn, c, c_ctx, ada_w, ada_b, norm1_g, norm2_g, final_g, ab_w_in, ab_w_out, na_rpb, sc_conv_w, cd_w_in, cd_w_out, hg_lb_logits, router_w, router_b, moe_w1, moe_b1, moe_w2, moe_b2):
    x_all = (x_prompt.reshape(N_PROMPT, D_MODEL), x_sample.reshape(N_SAMPLE, D_MODEL))
    m16 = jnp.zeros((N_MOD_ROWS, D_MODEL), F32).at[0].set(c_ctx).at[1:1 + DEC_BATCH].set(c)
    mod = _modulation(m16, ada_w, ada_b)

    lb_sm = jax.nn.softmax(hg_lb_logits.astype(F32), axis=0)
    lb_all = jnp.cumsum(lb_sm, axis=0) - lb_sm[0:1]

    rw_t = jnp.swapaxes(router_w.astype(F32), 1, 2)
    rw_hi = rw_t.astype(BF16)
    rw_lo = (rw_t - rw_hi.astype(F32)).astype(BF16)
    rw2 = jnp.stack([rw_hi, rw_lo], axis=1)
    rb_pad = jnp.broadcast_to(router_b.astype(F32)[:, :, None], (DEPTH, N_EXPERTS, LANES))

    new_k = new_v = new_s = None
    for l in range(DEPTH):
        j = l // 2
        if l % 2 == 0:
            p_all = _pre_project(x_all, mod[l], norm1_g[l], ab_w_in[j].astype(BF16))
            att_p = _context_attention(p_all)
            ck = cache_attn_k[:, j].transpose(0, 2, 1, 3).reshape(DEC_BATCH, PAST_LEN, NA_W)
            cv = cache_attn_v[:, j].transpose(0, 2, 1, 3).reshape(DEC_BATCH, PAST_LEN, NA_W)
            att_s = _neighbourhood_attention(p_all, ck, cv, _natten_bias(na_rpb[j]))
            kp = p_all[:N_PROMPT, NA_W:2 * NA_W].reshape(BATCH, SEQ, NA_HEADS, NA_HEAD_DIM).transpose(0, 2, 1, 3)
            vp = p_all[:N_PROMPT, 2 * NA_W:3 * NA_W].reshape(BATCH, SEQ, NA_HEADS, NA_HEAD_DIM).transpose(0, 2, 1, 3)
            new_k, new_v = kp[:, None], vp[:, None]
            outs = _post_mix((att_p, att_s), p_all, sc_conv_w[j], x_all, mod[l], norm2_g[l],
                             ab_w_out[j].astype(BF16), rw2[l], rb_pad[l], with_conv=True)
        else:
            p_all = _pre_project(x_all, mod[l], norm1_g[l], cd_w_in[j].astype(BF16))
            fn_p = _fourier_mix(p_all, 0, BATCH, SEQ)
            fn_s = _fourier_mix(p_all, N_PROMPT, DEC_BATCH, DEC_SEQ)
            s0_p = jnp.zeros((BATCH, 2, HG_HEADS, HG_DK, HG_DK), F32)
            s0_s = jnp.swapaxes(state_hgrn[:, j].astype(F32), -1, -2)
            hg_p, sfin_p = _hgrn2_bidir(p_all, lb_all[l], s0_p, 0, BATCH, SEQ)
            hg_s, _ = _hgrn2_bidir(p_all, lb_all[l], s0_s, N_PROMPT, DEC_BATCH, DEC_SEQ)
            new_s = jnp.swapaxes(sfin_p, -1, -2)[:, None].astype(x_prompt.dtype)
            outs = _post_mix((fn_p, fn_s), (hg_p, hg_s), None, x_all, mod[l], norm2_g[l],
                             cd_w_out[j].astype(BF16), rw2[l], rb_pad[l], with_conv=False)
        x_mid, h2, route_rows, route_tok, tile_cnt = outs
        tables, blk_expert, blk_rows = _moe_layout(tile_cnt)
        xs = _moe_dispatch(tables, h2, route_rows)
        ys = _moe_experts(l, blk_expert, blk_rows, xs, moe_w1, moe_b1[:, :, None, :], moe_w2, moe_b2[:, :, None, :])
        x_all = _moe_combine(tables, ys, route_tok, x_mid, mod[l], final_g, final=(l == DEPTH - 1))

    y_prompt, y_sample = x_all
    return (y_prompt.reshape(BATCH, SEQ, D_MODEL), y_sample.reshape(DEC_BATCH, DEC_SEQ, D_MODEL),
            new_k, new_v, new_s)
```

```python
import functools

import numpy as np
import jax
import jax.numpy as jnp
from jax import lax
from jax.experimental import pallas as pl
from jax.experimental.pallas import tpu as pltpu

F32 = jnp.float32
BF16 = jnp.bfloat16

D_MODEL = 1024
BATCH = 16
SEQ = 256
DEC_BATCH = 8
DEC_SEQ = 2048
PAST_LEN = 512
DEPTH = 2
GRID_W = 64
GRID_ROWS = DEC_SEQ // GRID_W
NA_HEADS = 8
NA_HEAD_DIM = 64
NA_W = NA_HEADS * NA_HEAD_DIM
SC_W = D_MODEL - NA_W
NA_SCALE = NA_HEAD_DIM ** -0.5
MAX_WR = 8
WIN_W = 16
FN_W = 512
FN_GROUPS = 4
FN_GROUP_W = 128
HG_W = 512
HG_HEADS = 4
HG_DK = 128
HG_CHUNK = 64
HG_SUB = 16
N_EXPERTS = 32
TOP_K = 4
D_FF = D_MODEL
SWIGLU_LIMIT = 7.0
SWIGLU_ALPHA = 1.702
EPS = 1e-6
NEG_INF = -1e30

N_PROMPT = BATCH * SEQ
N_SAMPLE = DEC_BATCH * DEC_SEQ
N_TOK = N_PROMPT + N_SAMPLE
TM = 256
N_TILES = N_TOK // TM
PROMPT_TILES = N_PROMPT // TM
TILES_PER_DEC_SEQ = DEC_SEQ // TM
N_MOD_ROWS = 16
LANES = 128
SUBLANES = 8

ROUTE_ROWS = 16
MOE_BLK = 512
N_ASG = N_TOK * TOP_K
N_BLK = N_ASG // MOE_BLK + N_EXPERTS
N_SLOT = N_BLK * MOE_BLK
VMEM_LIMIT = 56 * 1024 * 1024


def _mod_row(i):
    return jnp.where(i < PROMPT_TILES, 0, 1 + (i - PROMPT_TILES) // TILES_PER_DEC_SEQ)


def _sigmoid_pair(z):
    e = jnp.exp(-jnp.abs(z))
    r = 1.0 / (1.0 + e)
    er = e * r
    pos = z >= 0
    return jnp.where(pos, r, er), jnp.where(pos, er, r)


def _silu(x):
    return x * _sigmoid_pair(x)[0]


def _rms(x):
    return x * lax.rsqrt(jnp.mean(x * x, axis=-1, keepdims=True) + EPS)


def _dot(a, b):
    return jnp.dot(a, b, preferred_element_type=F32)


def _dot_nt(a, b):
    return lax.dot_general(a, b, (((1,), (1,)), ((), ())), preferred_element_type=F32)


def _dot_tn(a, b):
    return lax.dot_general(a, b, (((0,), (0,)), ((), ())), preferred_element_type=F32)


def _split3(x):
    hi = x.astype(BF16)
    r1 = x - hi.astype(F32)
    mid = r1.astype(BF16)
    lo = (r1 - mid.astype(F32)).astype(BF16)
    return hi, mid, lo


MOD_TN = 1024


def _mod_kernel(m_ref, w_ref, b_ref, o_ref):
    a = _silu(m_ref[...]).astype(BF16)
    o_ref[...] = _dot(a, w_ref[...].astype(BF16)) + b_ref[...]


def _modulation(m16, ada_w, ada_b):
    n_out = 6 * D_MODEL
    out = pl.pallas_call(
        _mod_kernel,
        out_shape=jax.ShapeDtypeStruct((DEPTH, N_MOD_ROWS, n_out), F32),
        grid=(DEPTH, n_out // MOD_TN),
        in_specs=[
            pl.BlockSpec((N_MOD_ROWS, D_MODEL), lambda l, j: (0, 0)),
            pl.BlockSpec((None, D_MODEL, MOD_TN), lambda l, j: (l, 0, j)),
            pl.BlockSpec((None, 1, MOD_TN), lambda l, j: (l, 0, j)),
        ],
        out_specs=pl.BlockSpec((None, N_MOD_ROWS, MOD_TN), lambda l, j: (l, 0, j)),
        compiler_params=pltpu.CompilerParams(
            dimension_semantics=("arbitrary", "arbitrary"), vmem_limit_bytes=VMEM_LIMIT),
    )(m16, ada_w, ada_b.reshape(DEPTH, 1, n_out))
    return out.reshape(DEPTH, N_MOD_ROWS, 6, D_MODEL)


def _by_group(i, refs):
    if len(refs) == 1:
        return refs[0][...]
    return jnp.where(i < PROMPT_TILES, refs[0][...], refs[1][...])


def _group_specs(width):
    return [pl.BlockSpec((TM, width), lambda i: (jnp.minimum(i, PROMPT_TILES - 1), 0)),
            pl.BlockSpec((TM, width), lambda i: (jnp.maximum(i - PROMPT_TILES, 0), 0))]


def _pre_kernel(*refs):
    x_refs, (mod_ref, g_ref, w_ref, o_ref) = refs[:-4], refs[-4:]
    y = _rms(_by_group(pl.program_id(0), x_refs)) * g_ref[...]
    h = y * (1.0 + mod_ref[1:2, :]) + mod_ref[0:1, :]
    o_ref[...] = _dot(h.astype(BF16), w_ref[...])


def _pre_project(x_src, mod_l, g, w_bf):
    n_out = w_bf.shape[1]
    if isinstance(x_src, tuple):
        x_specs, x_args = _group_specs(D_MODEL), list(x_src)
    else:
        x_specs, x_args = [pl.BlockSpec((TM, D_MODEL), lambda i: (i, 0))], [x_src]
    return pl.pallas_call(
        _pre_kernel,
        out_shape=jax.ShapeDtypeStruct((N_TOK, n_out), F32),
        grid=(N_TILES,),
        in_specs=x_specs + [
            pl.BlockSpec((None, 6, D_MODEL), lambda i: (_mod_row(i), 0, 0)),
            pl.BlockSpec((1, D_MODEL), lambda i: (0, 0)),
            pl.BlockSpec((D_MODEL, n_out), lambda i: (0, 0)),
        ],
        out_specs=pl.BlockSpec((TM, n_out), lambda i: (i, 0)),
        compiler_params=pltpu.CompilerParams(
            dimension_semantics=("arbitrary",), vmem_limit_bytes=VMEM_LIMIT),
    )(*x_args, mod_l, g.reshape(1, D_MODEL), w_bf)


def _softmax_pv(parts):
    m = None
    for s, _ in parts:
        mi = jnp.max(s, axis=-1, keepdims=True)
        m = mi if m is None else jnp.maximum(m, mi)
    acc = None
    den = None
    for s, v in parts:
        p = jnp.exp(s - m)
        d = jnp.sum(p, axis=-1, keepdims=True)
        o = _dot(p.astype(BF16), v)
        acc = o if acc is None else acc + o
        den = d if den is None else den + d
    return acc / den


def _ctx_attn_kernel(q_ref, k_ref, v_ref, o_ref):
    q = q_ref[...]
    kb = k_ref[...].astype(BF16)
    vb = v_ref[...].astype(BF16)
    lane = lax.broadcasted_iota(jnp.int32, q.shape, 1)
    out = jnp.zeros(q.shape, F32)
    for h in range(2):
        hm = (lane >= NA_HEAD_DIM) if h == 1 else (lane < NA_HEAD_DIM)
        qm = jnp.where(hm, q, 0.0).astype(BF16)
        s = _dot_nt(qm, kb) * NA_SCALE
        out = jnp.where(hm, _softmax_pv([(s, vb)]), out)
    o_ref[...] = out


def _context_attention(p_all):
    n_hp = NA_HEADS // 2
    return pl.pallas_call(
        _ctx_attn_kernel,
        out_shape=jax.ShapeDtypeStruct((N_PROMPT, NA_W), F32),
        grid=(BATCH, n_hp),
        in_specs=[
            pl.BlockSpec((SEQ, LANES), lambda b, hp: (b, hp)),
            pl.BlockSpec((SEQ, LANES), lambda b, hp: (b, n_hp + hp)),
            pl.BlockSpec((SEQ, LANES), lambda b, hp: (b, 2 * n_hp + hp)),
        ],
        out_specs=pl.BlockSpec((SEQ, LANES), lambda b, hp: (b, hp)),
        compiler_params=pltpu.CompilerParams(
            dimension_semantics=("arbitrary", "arbitrary"), vmem_limit_bytes=VMEM_LIMIT),
    )(p_all, p_all, p_all)


N_BIAS_VAR = MAX_WR
WIN_KEYS = MAX_WR * GRID_W


def _natten_kernel(q_ref, k_ref, v_ref, ck_ref, cv_ref, bias_ref, o_ref):
    ckb = ck_ref[...].astype(BF16)
    cvb = cv_ref[...].astype(BF16)
    lane = lax.broadcasted_iota(jnp.int32, (GRID_W, LANES), 1)

    def row(r, carry):
        r0 = jnp.clip(r - MAX_WR // 2, 0, GRID_ROWS - MAX_WR)
        var = r0 - r + (MAX_WR - 1)
        q = q_ref[pl.ds(pl.multiple_of(r * GRID_W, GRID_W), GRID_W), :]
        kw = k_ref[pl.ds(pl.multiple_of(r0 * GRID_W, GRID_W), WIN_KEYS), :].astype(BF16)
        vw = v_ref[pl.ds(pl.multiple_of(r0 * GRID_W, GRID_W), WIN_KEYS), :].astype(BF16)
        first = lane < NA_HEAD_DIM
        q2 = jnp.concatenate([jnp.where(first, q, 0.0), jnp.where(first, 0.0, q)], axis=0).astype(BF16)
        s_loc = _dot_nt(q2, kw) * NA_SCALE + bias_ref[var]
        s_ctx = _dot_nt(q2, ckb) * NA_SCALE
        o2 = _softmax_pv([(s_loc, vw), (s_ctx, cvb)])
        o_ref[pl.ds(pl.multiple_of(r * GRID_W, GRID_W), GRID_W), :] = jnp.where(first, o2[:GRID_W], o2[GRID_W:])
        return carry

    lax.fori_loop(0, GRID_ROWS, row, 0, unroll=4)


def _natten_bias(rpb):
    qc = np.arange(GRID_W)[:, None]
    kc = np.arange(GRID_W)[None, :]
    ws = np.clip(qc - WIN_W // 2, 0, GRID_W - WIN_W)
    valid = (kc >= ws) & (kc < ws + WIN_W)
    dc = np.clip(kc - qc + WIN_W - 1, 0, 2 * WIN_W - 2)
    t = rpb.astype(F32)[:, :, dc]
    t = jnp.where(jnp.asarray(valid)[None, None], t, NEG_INF)
    dr = np.arange(N_BIAS_VAR)[:, None] + np.arange(MAX_WR)[None, :]
    slab = t[:, dr]
    slab = slab.transpose(0, 1, 3, 2, 4).reshape(NA_HEADS // 2, 2, N_BIAS_VAR, GRID_W, WIN_KEYS)
    return slab.transpose(0, 2, 1, 3, 4).reshape(NA_HEADS // 2, N_BIAS_VAR, 2 * GRID_W, WIN_KEYS)


def _neighbourhood_attention(p_all, ck, cv, bias):
    n_hp = NA_HEADS // 2
    off = N_PROMPT // DEC_SEQ
    return pl.pallas_call(
        _natten_kernel,
        out_shape=jax.ShapeDtypeStruct((N_SAMPLE, NA_W), F32),
        grid=(DEC_BATCH, n_hp),
        in_specs=[
            pl.BlockSpec((DEC_SEQ, LANES), lambda b, hp: (off + b, hp)),
            pl.BlockSpec((DEC_SEQ, LANES), lambda b, hp: (off + b, n_hp + hp)),
            pl.BlockSpec((DEC_SEQ, LANES), lambda b, hp: (off + b, 2 * n_hp + hp)),
            pl.BlockSpec((None, PAST_LEN, LANES), lambda b, hp: (b, 0, hp)),
            pl.BlockSpec((None, PAST_LEN, LANES), lambda b, hp: (b, 0, hp)),
            pl.BlockSpec((None, N_BIAS_VAR, 2 * GRID_W, WIN_KEYS), lambda b, hp: (hp, 0, 0, 0)),
        ],
        out_specs=pl.BlockSpec((DEC_SEQ, LANES), lambda b, hp: (b, hp)),
        compiler_params=pltpu.CompilerParams(
            dimension_semantics=("arbitrary", "arbitrary"), vmem_limit_bytes=VMEM_LIMIT),
    )(p_all, p_all, p_all, ck, cv, bias)


def _dft_tables(t):
    j = np.arange(t, dtype=np.int64)
    ang = 2.0 * np.pi * ((j[:, None] * j[None, :]) % t).astype(np.float64) / t
    return np.cos(ang), np.sin(ang)


def _fourier_kernel(z_ref, lt_ref, rc_ref, rs_ref, o_ref, ab_ref, *, t_len, scale):
    @pl.when(pl.program_id(1) == 0)
    def _():
        z = z_ref[...].astype(BF16)
        for g in range(FN_GROUPS):
            zg = z[:, g * FN_GROUP_W:(g + 1) * FN_GROUP_W]
            ab_ref[0:t_len, g * FN_GROUP_W:(g + 1) * FN_GROUP_W] = _dot(zg, rc_ref[...]).astype(BF16)
            ab_ref[t_len:2 * t_len, g * FN_GROUP_W:(g + 1) * FN_GROUP_W] = _dot(zg, rs_ref[...]).astype(BF16)

    o_ref[...] = _dot(lt_ref[...], ab_ref[...]) * scale


def _fourier_mix(p_all, row_off, n_batch, t_len):
    ct, st = _dft_tables(t_len)
    lt = jnp.asarray(np.concatenate([ct, -st], axis=1), dtype=F32).astype(BF16)
    cc, sc = _dft_tables(FN_GROUP_W)
    rc = jnp.asarray(cc, dtype=F32).astype(BF16)
    rs = jnp.asarray(sc, dtype=F32).astype(BF16)
    tr = min(t_len, 512)
    n_rt = t_len // tr
    off = row_off // t_len
    kern = functools.partial(_fourier_kernel, t_len=t_len, scale=float((t_len * FN_GROUP_W) ** -0.5))
    return pl.pallas_call(
        kern,
        out_shape=jax.ShapeDtypeStruct((n_batch * t_len, FN_W), F32),
        grid=(n_batch, n_rt),
        in_specs=[
            pl.BlockSpec((t_len, FN_W), lambda b, i: (off + b, 0)),
            pl.BlockSpec((tr, 2 * t_len), lambda b, i: (i, 0)),
            pl.BlockSpec((FN_GROUP_W, FN_GROUP_W), lambda b, i: (0, 0)),
            pl.BlockSpec((FN_GROUP_W, FN_GROUP_W), lambda b, i: (0, 0)),
        ],
        out_specs=pl.BlockSpec((tr, FN_W), lambda b, i: (b * n_rt + i, 0)),
        scratch_shapes=[pltpu.VMEM((2 * t_len, FN_W), BF16)],
        compiler_params=pltpu.CompilerParams(
            dimension_semantics=("arbitrary", "arbitrary"), vmem_limit_bytes=VMEM_LIMIT),
    )(p_all, lt, rc, rs)


N_SUB = HG_CHUNK // HG_SUB
LOG2_E = 1.4426950408889634


def _gla_chunk(q, v, z, lb, st, backward):
    c = HG_CHUNK
    sg, sgn = _sigmoid_pair(z)
    lf = jnp.log(lb + (1.0 - lb) * sg)
    kk = (1.0 - lb) * sgn
    row = lax.broadcasted_iota(jnp.int32, (c, c), 0)
    col = lax.broadcasted_iota(jnp.int32, (c, c), 1)
    tri = ((col >= row) if backward else (col <= row)).astype(BF16)
    hi, mid, lo = _split3(lf)
    b = _dot(tri, hi) + _dot(tri, mid) + _dot(tri, lo)
    o_inter = _dot_nt((q * jnp.exp(b)).astype(BF16), st.astype(BF16))
    b2 = b * LOG2_E
    c2 = b2 - jnp.log2(kk)

    sub_row = lax.broadcasted_iota(jnp.int32, (HG_SUB, c), 0)
    sub_col = lax.broadcasted_iota(jnp.int32, (HG_SUB, c), 1)
    att_rows = []
    for i in range(N_SUB):
        lo_r, hi_r = i * HG_SUB, (i + 1) * HG_SUB
        q_i = q[lo_r:hi_r]
        b_i = b[lo_r:hi_r]
        has_off = (i < N_SUB - 1) if backward else (i > 0)
        if has_off:
            r_i = b[hi_r:hi_r + 1] if backward else b[lo_r - 1:lo_r]
            qs = (q_i * jnp.exp(b_i - r_i)).astype(BF16)
            ks = (kk * jnp.exp(jnp.minimum(r_i - b, 0.0))).astype(BF16)
            a = _dot_nt(qs, ks)
            off_mask = (sub_col >= hi_r) if backward else (sub_col < lo_r)
            a = jnp.where(off_mask, a, 0.0)
        else:
            a = jnp.zeros((HG_SUB, c), F32)
        b2_i = b2[lo_r:hi_r]
        c2_i = c2[lo_r:hi_r]
        for s in range(HG_SUB):
            colsum = jnp.sum(q_i * jnp.exp2(b2_i - c2_i[s:s + 1]), axis=-1, keepdims=True)
            keep = (sub_row <= s) if backward else (sub_row >= s)
            a = jnp.where((sub_col == lo_r + s) & keep, colsum, a)
        att_rows.append(a)
    att = jnp.concatenate(att_rows, axis=0)
    o = o_inter + _dot(att.astype(BF16), v.astype(BF16))

    b_last = b[0:1] if backward else b[c - 1:c]
    kd = (kk * jnp.exp(b_last - b)).astype(BF16)
    st_new = st * jnp.exp(b_last) + _dot_tn(v.astype(BF16), kd)
    return o, st_new


def _hgrn_kernel(q_ref, i_ref, zf_ref, zb_ref, g_ref, lb_ref, s0_ref, o_ref, sfin_ref, of_ref, ob_ref,
                 stf_ref, stb_ref, *, t_len):
    n_chunk = t_len // HG_CHUNK

    def sl(c):
        return pl.ds(pl.multiple_of(c * HG_CHUNK, HG_CHUNK), HG_CHUNK)

    stf_ref[...] = s0_ref[0]
    stb_ref[...] = s0_ref[1]

    def step(j, carry):
        c = n_chunk - 1 - j
        o_f, st_f = _gla_chunk(q_ref[sl(j), :], i_ref[sl(j), :], zf_ref[sl(j), :], lb_ref[0:1, :],
                               stf_ref[...], backward=False)
        o_b, st_b = _gla_chunk(q_ref[sl(c), :], i_ref[sl(c), :], zb_ref[sl(c), :], lb_ref[1:2, :],
                               stb_ref[...], backward=True)
        of_ref[sl(j), :] = o_f
        ob_ref[sl(c), :] = o_b
        stf_ref[...] = st_f
        stb_ref[...] = st_b
        return carry

    lax.fori_loop(0, n_chunk, step, 0, unroll=4)
    sfin_ref[0] = stf_ref[...]
    sfin_ref[1] = stb_ref[...]

    def finish(c, carry):
        tot = of_ref[sl(c), :] + ob_ref[sl(c), :]
        o_ref[sl(c), :] = _rms(tot) * _silu(g_ref[sl(c), :])
        return carry

    lax.fori_loop(0, n_chunk, finish, 0)


def _hgrn2_bidir(p_all, lb, s0_t, row_off, n_batch, t_len):
    off = row_off // t_len
    cb = FN_W // LANES

    def col(j):
        return pl.BlockSpec((t_len, LANES), lambda b, h: (off + b, cb + j * HG_HEADS + h))

    kern = functools.partial(_hgrn_kernel, t_len=t_len)
    state_spec = pl.BlockSpec((None, 2, None, HG_DK, HG_DK), lambda b, h: (b, 0, h, 0, 0))
    return pl.pallas_call(
        kern,
        out_shape=(jax.ShapeDtypeStruct((n_batch * t_len, HG_W), F32),
                   jax.ShapeDtypeStruct((n_batch, 2, HG_HEADS, HG_DK, HG_DK), F32)),
        grid=(n_batch, HG_HEADS),
        in_specs=[col(0), col(1), col(2), col(3), col(4),
                  pl.BlockSpec((2, LANES), lambda b, h: (0, h)),
                  state_spec],
        out_specs=(pl.BlockSpec((t_len, LANES), lambda b, h: (b, h)), state_spec),
        scratch_shapes=[pltpu.VMEM((t_len, LANES), F32), pltpu.VMEM((t_len, LANES), F32),
                        pltpu.VMEM((HG_DK, HG_DK), F32), pltpu.VMEM((HG_DK, HG_DK), F32)],
        compiler_params=pltpu.CompilerParams(
            dimension_semantics=("arbitrary", "arbitrary"), vmem_limit_bytes=VMEM_LIMIT),
    )(p_all, p_all, p_all, p_all, p_all, lb, s0_t)


def _route(h2, rw_ref, rb_ref):
    hi, mid, _ = _split3(h2)
    w_hi = rw_ref[0]
    w_lo = rw_ref[1]
    logits = _dot_nt(w_hi, hi) + _dot_nt(w_hi, mid) + _dot_nt(w_lo, hi) + rb_ref[:, 0:1]
    sub = lax.broadcasted_iota(jnp.int32, (N_EXPERTS, TM), 0)
    work = logits
    vals, hots = [], []
    for k in range(TOP_K):
        m = jnp.max(work, axis=0, keepdims=True)
        idx = jnp.min(jnp.where(work == m, sub, N_EXPERTS), axis=0, keepdims=True)
        hot = sub == idx
        work = jnp.where(hot, -jnp.inf, work)
        vals.append(m)
        hots.append(hot)
    es = [jnp.exp(v - vals[0]) for v in vals]
    den = es[0] + es[1] + es[2] + es[3]
    cnt = jnp.zeros((N_EXPERTS, TM), F32)
    for hot in hots:
        cnt = cnt + hot.astype(F32)
    row = lax.broadcasted_iota(jnp.int32, (TM, TM), 0)
    col = lax.broadcasted_iota(jnp.int32, (TM, TM), 1)
    before = _dot(cnt.astype(BF16), (row < col).astype(BF16))
    tile_cnt = jnp.sum(cnt, axis=1, keepdims=True)
    erow = lax.broadcasted_iota(jnp.int32, (N_EXPERTS, N_EXPERTS), 0)
    ecol = lax.broadcasted_iota(jnp.int32, (N_EXPERTS, N_EXPERTS), 1)
    lower = (ecol < erow).astype(BF16)
    run_off = _dot(lower, jnp.broadcast_to(tile_cnt, (N_EXPERTS, LANES)).astype(BF16))[:, 0:1]
    where_to = before + run_off
    rsub = lax.broadcasted_iota(jnp.int32, (ROUTE_ROWS, TM), 0)
    rows = jnp.zeros((ROUTE_ROWS, TM), F32)
    for k in range(TOP_K):
        pk = jnp.sum(jnp.where(hots[k], where_to, 0.0), axis=0, keepdims=True)
        rows = jnp.where(rsub == k, pk, rows)
        rows = jnp.where(rsub == TOP_K + k, es[k] / den, rows)
    return rows, tile_cnt


def _post_kernel(*refs, with_conv, n_x):
    refs = list(refs)
    a_refs, refs = refs[:2], refs[2:]
    if with_conv:
        (c_ref, cprev_ref, cnext_ref, cw_ref), refs = refs[:4], refs[4:]
    else:
        b_refs, refs = refs[:2], refs[2:]
    x_refs, refs = refs[:n_x], refs[n_x:]
    mod_ref, g_ref, wo_ref, rw_ref, rb_ref, xo_ref, h2_ref, rows_ref, pg_ref, cnt_ref = refs
    i = pl.program_id(0)

    if with_conv:
        j = i - PROMPT_TILES
        is_prompt = i < PROMPT_TILES
        seq_start = is_prompt | (j % TILES_PER_DEC_SEQ == 0)
        seq_end = is_prompt | (j % TILES_PER_DEC_SEQ == TILES_PER_DEC_SEQ - 1)
        bg = c_ref[:, 0:SC_W]
        z = c_ref[:, SC_W:2 * SC_W] * c_ref[:, 2 * SC_W:3 * SC_W]
        z_prev = cprev_ref[SUBLANES - 1:SUBLANES, SC_W:2 * SC_W] * cprev_ref[SUBLANES - 1:SUBLANES, 2 * SC_W:3 * SC_W]
        z_next = cnext_ref[0:1, SC_W:2 * SC_W] * cnext_ref[0:1, 2 * SC_W:3 * SC_W]
        z_prev = jnp.where(seq_start, 0.0, z_prev)
        z_next = jnp.where(seq_end, 0.0, z_next)
        row = lax.broadcasted_iota(jnp.int32, (TM, SC_W), 0)
        zm1 = jnp.where(row == 0, z_prev, pltpu.roll(z, 1, axis=0))
        zp1 = jnp.where(row == TM - 1, z_next, pltpu.roll(z, TM - 1, axis=0))
        b_val = bg * (zm1 * cw_ref[0:1, :] + z * cw_ref[1:2, :] + zp1 * cw_ref[2:3, :])
    else:
        b_val = _by_group(i, b_refs)
    a_val = _by_group(i, a_refs)
    half = a_val.shape[1]
    mix = _dot(a_val.astype(BF16), wo_ref[0:half, :]) + _dot(b_val.astype(BF16), wo_ref[half:, :])
    x_new = _by_group(i, x_refs) + mod_ref[2:3, :] * mix
    xo_ref[...] = x_new
    h2 = _rms(x_new) * g_ref[...] * (1.0 + mod_ref[4:5, :]) + mod_ref[3:4, :]
    h2_ref[...] = h2
    rows, tile_cnt = _route(h2, rw_ref, rb_ref)
    rows_ref[...] = rows
    pick = (lax.broadcasted_iota(jnp.int32, (ROUTE_ROWS, LANES), 0)
            == lax.broadcasted_iota(jnp.int32, (ROUTE_ROWS, LANES), 1)).astype(BF16)
    r_hi, r_mid, r_lo = _split3(rows)
    pg_ref[...] = _dot_tn(r_hi, pick) + _dot_tn(r_mid, pick) + _dot_tn(r_lo, pick)
    cnt_ref[...] = jnp.broadcast_to(tile_cnt, cnt_ref.shape).astype(jnp.int32)


def _post_mix(a_pair, b_src, conv_w, x_src, mod_l, g2, wo_bf, rw2, rb, with_conv):
    tile = lambda w: pl.BlockSpec((TM, w), lambda i: (i, 0))
    full = lambda shp: pl.BlockSpec(shp, lambda i: tuple(0 for _ in shp))
    rows8 = TM // SUBLANES
    n8 = N_TOK // SUBLANES
    in_specs = _group_specs(a_pair[0].shape[1])
    args = list(a_pair)
    if with_conv:
        cw = 3 * SC_W
        in_specs += [
            pl.BlockSpec((TM, cw), lambda i: (i, 1)),
            pl.BlockSpec((SUBLANES, cw), lambda i: (jnp.maximum(i * rows8 - 1, 0), 1)),
            pl.BlockSpec((SUBLANES, cw), lambda i: (jnp.minimum((i + 1) * rows8, n8 - 1), 1)),
            full((3, SC_W)),
        ]
        args += [b_src, b_src, b_src, conv_w]
    else:
        in_specs += _group_specs(b_src[0].shape[1])
        args += list(b_src)
    if isinstance(x_src, tuple):
        in_specs += _group_specs(D_MODEL)
        args += list(x_src)
        n_x = 2
    else:
        in_specs += [tile(D_MODEL)]
        args += [x_src]
        n_x = 1
    in_specs += [
        pl.BlockSpec((None, 6, D_MODEL), lambda i: (_mod_row(i), 0, 0)),
        full((1, D_MODEL)),
        full((D_MODEL, D_MODEL)),
        full((2, N_EXPERTS, D_MODEL)),
        full((N_EXPERTS, LANES)),
    ]
    args += [mod_l, g2.reshape(1, D_MODEL), wo_bf, rw2, rb]
    out_shape = (
        jax.ShapeDtypeStruct((N_TOK, D_MODEL), F32),
        jax.ShapeDtypeStruct((N_TOK, D_MODEL), F32),
        jax.ShapeDtypeStruct((N_TILES * ROUTE_ROWS, TM), F32),
        jax.ShapeDtypeStruct((N_TOK, LANES), F32),
        jax.ShapeDtypeStruct((N_TILES, N_EXPERTS, LANES), jnp.int32),
    )
    out_specs = (tile(D_MODEL), tile(D_MODEL), pl.BlockSpec((ROUTE_ROWS, TM), lambda i: (i, 0)), tile(LANES),
                 pl.BlockSpec((None, N_EXPERTS, LANES), lambda i: (i, 0, 0)))
    return pl.pallas_call(
        functools.partial(_post_kernel, with_conv=with_conv, n_x=n_x),
        out_shape=out_shape,
        grid=(N_TILES,),
        in_specs=in_specs,
        out_specs=out_specs,
        compiler_params=pltpu.CompilerParams(
            dimension_semantics=("arbitrary",), vmem_limit_bytes=VMEM_LIMIT),
    )(*args)


ROW_SUB = D_MODEL // LANES
TILE_ASG = TM * TOP_K
RUN_BITS = TM.bit_length()
N_TAIL_BLOCKS = N_BLK - N_ASG // MOE_BLK
N_ZERO_BLOCKS = N_EXPERTS + N_TAIL_BLOCKS


def _rows_to_wide(ref, n_rows, lead=None):
    parts = []
    for j in range(ROW_SUB):
        idx = (pl.ds(j, n_rows, stride=ROW_SUB), slice(None))
        parts.append(ref[idx] if lead is None else ref[(lead,) + idx])
    return jnp.concatenate(parts, axis=1)


def _wide_to_rows(ref, val, lead=None):
    n_rows = val.shape[0]
    for j in range(ROW_SUB):
        idx = (pl.ds(j, n_rows, stride=ROW_SUB), slice(None))
        if lead is not None:
            idx = (lead,) + idx
        ref[idx] = val[:, j * LANES:(j + 1) * LANES]


def _row_window(ref, start_row, n_rows, lead=None):
    idx = (pl.ds(pl.multiple_of(start_row * ROW_SUB, ROW_SUB), n_rows * ROW_SUB), slice(None))
    return ref.at[idx] if lead is None else ref.at[(lead,) + idx]


def _run_copies(tile, rdst_ref, roff_ref, rlen_ref, make_copy):
    def per_expert(e, carry):
        r = tile * N_EXPERTS + e
        n = rlen_ref[r]
        src = roff_ref[r]
        dst = rdst_ref[r]
        for bit in reversed(range(RUN_BITS)):
            @pl.when((n & (1 << bit)) != 0)
            def _():
                done = n & ~((2 << bit) - 1)
                make_copy(src + done, dst + done, 1 << bit).start()
        return carry

    lax.fori_loop(0, N_EXPERTS, per_expert, 0)


def _dispatch_kernel(rdst_ref, roff_ref, rlen_ref, zrow_ref, h2_ref, rows_ref, xs_hbm,
                     stage, zbuf, sem, zsem):
    i = pl.program_id(0)
    n = pl.num_programs(0)
    slot = i % 2

    def zero_copy(e):
        return pltpu.make_async_copy(zbuf, _row_window(xs_hbm, zrow_ref[e], MOE_BLK), zsem)

    @pl.when(i == 0)
    def _():
        zbuf[...] = jnp.zeros_like(zbuf)

        def start(e, carry):
            @pl.when(zrow_ref[e] >= 0)
            def _():
                zero_copy(e).start()
            return carry

        def wait(e, carry):
            @pl.when(zrow_ref[e] >= 0)
            def _():
                zero_copy(e).wait()
            return carry

        lax.fori_loop(0, N_ZERO_BLOCKS, start, 0)
        lax.fori_loop(0, N_ZERO_BLOCKS, wait, 0)

    def wait_tile(slot_):
        pltpu.make_async_copy(stage.at[slot_], _row_window(xs_hbm, 0, TILE_ASG), sem.at[slot_]).wait()

    @pl.when(i >= 2)
    def _():
        wait_tile(slot)

    pos = rows_ref[0:TOP_K, :].astype(jnp.int32)
    sub = lax.broadcasted_iota(jnp.int32, (TILE_ASG, TM), 0)
    hit = sub == pos[0:1, :]
    for k in range(1, TOP_K):
        hit = hit | (sub == pos[k:k + 1, :])
    sorted_rows = _dot(hit.astype(BF16), h2_ref[...].astype(BF16))
    _wide_to_rows(stage, sorted_rows, lead=slot)

    _run_copies(i, rdst_ref, roff_ref, rlen_ref,
                lambda s, d, nr: pltpu.make_async_copy(_row_window(stage, s, nr, lead=slot),
                                                       _row_window(xs_hbm, d, nr), sem.at[slot]))

    @pl.when(i == n - 1)
    def _():
        wait_tile(slot)
        wait_tile(1 - slot)


def _moe_dispatch(tables, h2, route_rows):
    rdst, roff, rlen, zrow = tables
    grid_spec = pltpu.PrefetchScalarGridSpec(
        num_scalar_prefetch=4,
        grid=(N_TILES,),
        in_specs=[
            pl.BlockSpec((TM, D_MODEL), lambda i, *_: (i, 0)),
            pl.BlockSpec((ROUTE_ROWS, TM), lambda i, *_: (i, 0)),
        ],
        out_specs=pl.BlockSpec(memory_space=pl.ANY),
        scratch_shapes=[
            pltpu.VMEM((2, TILE_ASG * ROW_SUB, LANES), F32),
            pltpu.VMEM((MOE_BLK * ROW_SUB, LANES), F32),
            pltpu.SemaphoreType.DMA((2,)),
            pltpu.SemaphoreType.DMA(()),
        ],
    )
    return pl.pallas_call(
        _dispatch_kernel,
        out_shape=jax.ShapeDtypeStruct((N_SLOT * ROW_SUB, LANES), F32),
        grid_spec=grid_spec,
        compiler_params=pltpu.CompilerParams(
            dimension_semantics=("arbitrary",), vmem_limit_bytes=VMEM_LIMIT, has_side_effects=True),
    )(rdst, roff, rlen, zrow, h2, route_rows)


def _ffn_kernel(bexp_ref, brows_ref, xs_ref, w1_ref, b1_ref, w2_ref, b2_ref, ys_ref, w1b, w2b):
    b = pl.program_id(0)
    rows = brows_ref[b]
    half = MOE_BLK // 2

    def mlp(n_rows):
        x = _rows_to_wide(xs_ref, n_rows).astype(BF16)
        u = _dot(x, w1b[...]) + b1_ref[...]
        glu = jnp.minimum(u[:, :D_FF], SWIGLU_LIMIT)
        lin = jnp.clip(u[:, D_FF:], -SWIGLU_LIMIT, SWIGLU_LIMIT)
        act = glu * _sigmoid_pair(SWIGLU_ALPHA * glu)[0] * (lin + 1.0)
        _wide_to_rows(ys_ref, _dot(act.astype(BF16), w2b[...]) + b2_ref[...])

    def zero_from(first_row):
        n = (MOE_BLK - first_row) * ROW_SUB
        ys_ref[pl.ds(first_row * ROW_SUB, n), :] = jnp.zeros((n, LANES), F32)

    @pl.when(rows > 0)
    def _():
        prev = jnp.maximum(b - 1, 0)

        @pl.when((b == 0) | (bexp_ref[b] != bexp_ref[prev]))
        def _():
            w1b[...] = w1_ref[...].astype(BF16)
            w2b[...] = w2_ref[...].astype(BF16)

    @pl.when(rows > half)
    def _():
        mlp(MOE_BLK)

    @pl.when((rows > 0) & (rows <= half))
    def _():
        mlp(half)
        zero_from(half)

    @pl.when(rows == 0)
    def _():
        zero_from(0)


def _moe_experts(layer, blk_expert, blk_rows, xs, w1, b1, w2, b2):
    def blk(b, be, br):
        return (b, 0)

    def wsel(b, be, br):
        return (layer, be[b], 0, 0)

    grid_spec = pltpu.PrefetchScalarGridSpec(
        num_scalar_prefetch=2,
        grid=(N_BLK,),
        in_specs=[
            pl.BlockSpec((MOE_BLK * ROW_SUB, LANES), blk),
            pl.BlockSpec((None, None, D_MODEL, 2 * D_FF), wsel),
            pl.BlockSpec((None, None, 1, 2 * D_FF), wsel),
            pl.BlockSpec((None, None, D_FF, D_MODEL), wsel),
            pl.BlockSpec((None, None, 1, D_MODEL), wsel),
        ],
        out_specs=pl.BlockSpec((MOE_BLK * ROW_SUB, LANES), blk),
        scratch_shapes=[
            pltpu.VMEM((D_MODEL, 2 * D_FF), BF16),
            pltpu.VMEM((D_FF, D_MODEL), BF16),
        ],
    )
    return pl.pallas_call(
        _ffn_kernel,
        out_shape=jax.ShapeDtypeStruct((N_SLOT * ROW_SUB, LANES), F32),
        grid_spec=grid_spec,
        compiler_params=pltpu.CompilerParams(
            dimension_semantics=("arbitrary",), vmem_limit_bytes=VMEM_LIMIT),
    )(blk_expert, blk_rows, xs, w1, b1, w2, b2)


def _combine_kernel(rdst_ref, roff_ref, rlen_ref, ys_hbm, pg_ref, x_ref, mod_ref, g_ref, *rest,
                    final):
    if final:
        yp_ref, ysm_ref, stage, sem = rest
    else:
        o_ref, stage, sem = rest
    i = pl.program_id(0)
    n = pl.num_programs(0)
    slot = i % 2

    def fetch(tile, slot_):
        _run_copies(tile, rdst_ref, roff_ref, rlen_ref,
                    lambda s, d, nr: pltpu.make_async_copy(_row_window(ys_hbm, d, nr),
                                                           _row_window(stage, s, nr, lead=slot_), sem.at[slot_]))

    @pl.when(i == 0)
    def _():
        fetch(0, 0)

    pltpu.make_async_copy(_row_window(ys_hbm, 0, TILE_ASG), stage.at[slot], sem.at[slot]).wait()

    @pl.when(i + 1 < n)
    def _():
        fetch(i + 1, 1 - slot)

    pg = pg_ref[...]
    pos = pg[:, 0:TOP_K].astype(jnp.int32)
    gates = pg[:, TOP_K:2 * TOP_K]
    lane = lax.broadcasted_iota(jnp.int32, (TM, TILE_ASG), 1)
    gmat = jnp.zeros((TM, TILE_ASG), F32)
    for k in range(TOP_K):
        gmat = jnp.where(lane == pos[:, k:k + 1], gates[:, k:k + 1], gmat)
    g_hi = gmat.astype(BF16)
    g_lo = (gmat - g_hi.astype(F32)).astype(BF16)
    cols = []
    for jj in range(ROW_SUB // 2):
        s = jnp.concatenate([stage[slot, pl.ds(2 * jj, TILE_ASG, stride=ROW_SUB), :],
                             stage[slot, pl.ds(2 * jj + 1, TILE_ASG, stride=ROW_SUB), :]], axis=1)
        s_bf = s.astype(BF16)
        cols.append(_dot(g_hi, s_bf) + _dot(g_lo, s_bf))
    moe = jnp.concatenate(cols, axis=1)
    x = x_ref[...] + mod_ref[5:6, :] * moe
    if final:
        x = _rms(x) * g_ref[...]

        @pl.when(i < PROMPT_TILES)
        def _():
            yp_ref[...] = x

        @pl.when(i >= PROMPT_TILES)
        def _():
            ysm_ref[...] = x
    else:
        o_ref[...] = x


def _moe_combine(tables, ys, route_tok, x_all, mod_l, final_g, final):
    rdst, roff, rlen, _ = tables
    tile = pl.BlockSpec((TM, D_MODEL), lambda i, *_: (i, 0))
    if final:
        out_shape = (jax.ShapeDtypeStruct((N_PROMPT, D_MODEL), F32), jax.ShapeDtypeStruct((N_SAMPLE, D_MODEL), F32))
        out_specs = (pl.BlockSpec((TM, D_MODEL), lambda i, *_: (jnp.minimum(i, PROMPT_TILES - 1), 0)),
                     pl.BlockSpec((TM, D_MODEL), lambda i, *_: (jnp.maximum(i - PROMPT_TILES, 0), 0)))
    else:
        out_shape = jax.ShapeDtypeStruct((N_TOK, D_MODEL), F32)
        out_specs = tile
    grid_spec = pltpu.PrefetchScalarGridSpec(
        num_scalar_prefetch=3,
        grid=(N_TILES,),
        in_specs=[
            pl.BlockSpec(memory_space=pl.ANY),
            pl.BlockSpec((TM, LANES), lambda i, *_: (i, 0)),
            tile,
            pl.BlockSpec((None, 6, D_MODEL), lambda i, *_: (_mod_row(i), 0, 0)),
            pl.BlockSpec((1, D_MODEL), lambda i, *_: (0, 0)),
        ],
        out_specs=out_specs,
        scratch_shapes=[
            pltpu.VMEM((2, TILE_ASG * ROW_SUB, LANES), F32),
            pltpu.SemaphoreType.DMA((2,)),
        ],
    )
    return pl.pallas_call(
        functools.partial(_combine_kernel, final=final),
        out_shape=out_shape,
        grid_spec=grid_spec,
        compiler_params=pltpu.CompilerParams(
            dimension_semantics=("arbitrary",), vmem_limit_bytes=VMEM_LIMIT),
    )(rdst, roff, rlen, ys, route_tok, x_all, mod_l, final_g.reshape(1, D_MODEL))


def _moe_layout(tile_cnt):
    cnt = tile_cnt[:, :, 0]
    total = jnp.sum(cnt, axis=0)
    padded = (total + MOE_BLK - 1) // MOE_BLK * MOE_BLK
    pad_end = jnp.cumsum(padded)
    pad_start = pad_end - padded
    before = jnp.cumsum(cnt, axis=0) - cnt
    run_dst = (pad_start[None, :] + before).reshape(-1).astype(jnp.int32)
    run_off = (jnp.cumsum(cnt, axis=1) - cnt).reshape(-1).astype(jnp.int32)
    run_len = cnt.reshape(-1).astype(jnp.int32)
    last_blk = jnp.where((padded > 0) & (total < padded), pad_end - MOE_BLK, -1)
    tail = pad_end[-1] + jnp.arange(N_TAIL_BLOCKS) * MOE_BLK
    zero_row = jnp.concatenate([last_blk, jnp.where(tail < N_SLOT, tail, -1)]).astype(jnp.int32)
    blk_start = jnp.arange(N_BLK, dtype=jnp.int32) * MOE_BLK
    blk_expert = jnp.minimum(jnp.sum((blk_start[:, None] >= pad_end[None, :]).astype(jnp.int32), axis=1),
                             N_EXPERTS - 1).astype(jnp.int32)
    blk_rows = jnp.clip((pad_start + total)[blk_expert] - blk_start, 0, MOE_BLK)
    blk_rows = jnp.where(blk_start < pad_end[-1], blk_rows, 0).astype(jnp.int32)
    return (run_dst, run_off, run_len, zero_row), blk_expert, blk_rows


def kernel(x_prompt, x_sample, cache_attn_k, cache_attn_v, state_hgrn, c, c_ctx, ada_w, ada_b, norm1_g, norm2_g, final_g, ab_w_in, ab_w_out, na_rpb, sc_conv_w, cd_w_in, cd_w_out, hg_lb_logits, router_w, router_b, moe_w1, moe_b1, moe_w2, moe_b2):
    x_all = (x_prompt.reshape(N_PROMPT, D_MODEL), x_sample.reshape(N_SAMPLE, D_MODEL))
    m16 = jnp.zeros((N_MOD_ROWS, D_MODEL), F32).at[0].set(c_ctx).at[1:1 + DEC_BATCH].set(c)
    mod = _modulation(m16, ada_w, ada_b)

    lb_sm = jax.nn.softmax(hg_lb_logits.astype(F32), axis=0)
    lb_all = jnp.cumsum(lb_sm, axis=0) - lb_sm[0:1]

    rw_t = jnp.swapaxes(router_w.astype(F32), 1, 2)
    rw_hi = rw_t.astype(BF16)
    rw_lo = (rw_t - rw_hi.astype(F32)).astype(BF16)
    rw2 = jnp.stack([rw_hi, rw_lo], axis=1)
    rb_pad = jnp.broadcast_to(router_b.astype(F32)[:, :, None], (DEPTH, N_EXPERTS, LANES))

    new_k = new_v = new_s = None
    for l in range(DEPTH):
        j = l // 2
        if l % 2 == 0:
            p_all = _pre_project(x_all, mod[l], norm1_g[l], ab_w_in[j].astype(BF16))
            att_p = _context_attention(p_all)
            ck = cache_attn_k[:, j].transpose(0, 2, 1, 3).reshape(DEC_BATCH, PAST_LEN, NA_W)
            cv = cache_attn_v[:, j].transpose(0, 2, 1, 3).reshape(DEC_BATCH, PAST_LEN, NA_W)
            att_s = _neighbourhood_attention(p_all, ck, cv, _natten_bias(na_rpb[j]))
            kp = p_all[:N_PROMPT, NA_W:2 * NA_W].reshape(BATCH, SEQ, NA_HEADS, NA_HEAD_DIM).transpose(0, 2, 1, 3)
            vp = p_all[:N_PROMPT, 2 * NA_W:3 * NA_W].reshape(BATCH, SEQ, NA_HEADS, NA_HEAD_DIM).transpose(0, 2, 1, 3)
            new_k, new_v = kp[:, None], vp[:, None]
            outs = _post_mix((att_p, att_s), p_all, sc_conv_w[j], x_all, mod[l], norm2_g[l],
                             ab_w_out[j].astype(BF16), rw2[l], rb_pad[l], with_conv=True)
        else:
            p_all = _pre_project(x_all, mod[l], norm1_g[l], cd_w_in[j].astype(BF16))
            fn_p = _fourier_mix(p_all, 0, BATCH, SEQ)
            fn_s = _fourier_mix(p_all, N_PROMPT, DEC_BATCH, DEC_SEQ)
            s0_p = jnp.zeros((BATCH, 2, HG_HEADS, HG_DK, HG_DK), F32)
            s0_s = jnp.swapaxes(state_hgrn[:, j].astype(F32), -1, -2)
            hg_p, sfin_p = _hgrn2_bidir(p_all, lb_all[l], s0_p, 0, BATCH, SEQ)
            hg_s, _ = _hgrn2_bidir(p_all, lb_all[l], s0_s, N_PROMPT, DEC_BATCH, DEC_SEQ)
            new_s = jnp.swapaxes(sfin_p, -1, -2)[:, None].astype(x_prompt.dtype)
            outs = _post_mix((fn_p, fn_s), (hg_p, hg_s), None, x_all, mod[l], norm2_g[l],
                             cd_w_out[j].astype(BF16), rw2[l], rb_pad[l], with_conv=False)
        x_mid, h2, route_rows, route_tok, tile_cnt = outs
        tables, blk_expert, blk_rows = _moe_layout(tile_cnt)
        xs = _moe_dispatch(tables, h2, route_rows)
        ys = _moe_experts(l, blk_expert, blk_rows, xs, moe_w1, moe_b1[:, :, None, :], moe_w2, moe_b2[:, :, None, :])
        x_all = _moe_combine(tables, ys, route_tok, x_mid, mod[l], final_g, final=(l == DEPTH - 1))

    y_prompt, y_sample = x_all
    return (y_prompt.reshape(BATCH, SEQ, D_MODEL), y_sample.reshape(DEC_BATCH, DEC_SEQ, D_MODEL),
            new_k, new_v, new_s)
```

```python
import functools

import numpy as np
import jax
import jax.numpy as jnp
from jax import lax
from jax.experimental import pallas as pl
from jax.experimental.pallas import tpu as pltpu

F32 = jnp.float32
BF16 = jnp.bfloat16

D_MODEL = 1024
BATCH = 16
SEQ = 256
DEC_BATCH = 8
DEC_SEQ = 2048
PAST_LEN = 512
DEPTH = 2
GRID_W = 64
GRID_ROWS = DEC_SEQ // GRID_W
NA_HEADS = 8
NA_HEAD_DIM = 64
NA_W = NA_HEADS * NA_HEAD_DIM
SC_W = D_MODEL - NA_W
NA_SCALE = NA_HEAD_DIM ** -0.5
MAX_WR = 8
WIN_W = 16
FN_W = 512
FN_GROUPS = 4
FN_GROUP_W = 128
HG_W = 512
HG_HEADS = 4
HG_DK = 128
HG_CHUNK = 64
HG_SUB = 16
N_EXPERTS = 32
TOP_K = 4
D_FF = D_MODEL
SWIGLU_LIMIT = 7.0
SWIGLU_ALPHA = 1.702
EPS = 1e-6
NEG_INF = -1e30

N_PROMPT = BATCH * SEQ
N_SAMPLE = DEC_BATCH * DEC_SEQ
N_TOK = N_PROMPT + N_SAMPLE
TM = 256
N_TILES = N_TOK // TM
PROMPT_TILES = N_PROMPT // TM
TILES_PER_DEC_SEQ = DEC_SEQ // TM
N_MOD_ROWS = 16
LANES = 128
SUBLANES = 8

ROUTE_ROWS = 16
POST_TILES = 2
MOE_BLK = 512
N_ASG = N_TOK * TOP_K
N_BLK = N_ASG // MOE_BLK + N_EXPERTS
N_SLOT = N_BLK * MOE_BLK
VMEM_LIMIT = 56 * 1024 * 1024


def _mod_row(i):
    return jnp.where(i < PROMPT_TILES, 0, 1 + (i - PROMPT_TILES) // TILES_PER_DEC_SEQ)


def _sigmoid_pair(z):
    e = jnp.exp(-jnp.abs(z))
    r = 1.0 / (1.0 + e)
    er = e * r
    pos = z >= 0
    return jnp.where(pos, r, er), jnp.where(pos, er, r)


def _silu(x):
    return x * _sigmoid_pair(x)[0]


def _rms(x):
    return x * lax.rsqrt(jnp.mean(x * x, axis=-1, keepdims=True) + EPS)


def _dot(a, b):
    return jnp.dot(a, b, preferred_element_type=F32)


def _dot_nt(a, b):
    return lax.dot_general(a, b, (((1,), (1,)), ((), ())), preferred_element_type=F32)


def _dot_tn(a, b):
    return lax.dot_general(a, b, (((0,), (0,)), ((), ())), preferred_element_type=F32)


def _split3(x):
    hi = x.astype(BF16)
    r1 = x - hi.astype(F32)
    mid = r1.astype(BF16)
    lo = (r1 - mid.astype(F32)).astype(BF16)
    return hi, mid, lo


MOD_TN = 1024


def _mod_kernel(m_ref, w_ref, b_ref, o_ref):
    a = _silu(m_ref[...]).astype(BF16)
    o_ref[...] = _dot(a, w_ref[...].astype(BF16)) + b_ref[...]


def _modulation(m16, ada_w, ada_b):
    n_out = 6 * D_MODEL
    out = pl.pallas_call(
        _mod_kernel,
        out_shape=jax.ShapeDtypeStruct((DEPTH, N_MOD_ROWS, n_out), F32),
        grid=(DEPTH, n_out // MOD_TN),
        in_specs=[
            pl.BlockSpec((N_MOD_ROWS, D_MODEL), lambda l, j: (0, 0)),
            pl.BlockSpec((None, D_MODEL, MOD_TN), lambda l, j: (l, 0, j)),
            pl.BlockSpec((None, 1, MOD_TN), lambda l, j: (l, 0, j)),
        ],
        out_specs=pl.BlockSpec((None, N_MOD_ROWS, MOD_TN), lambda l, j: (l, 0, j)),
        compiler_params=pltpu.CompilerParams(
            dimension_semantics=("arbitrary", "arbitrary"), vmem_limit_bytes=VMEM_LIMIT),
    )(m16, ada_w, ada_b.reshape(DEPTH, 1, n_out))
    return out.reshape(DEPTH, N_MOD_ROWS, 6, D_MODEL)


def _by_group(i, refs):
    if len(refs) == 1:
        return refs[0][...]
    return jnp.where(i < PROMPT_TILES, refs[0][...], refs[1][...])


def _group_specs(width, tiles=1):
    n_p = PROMPT_TILES // tiles
    return [pl.BlockSpec((tiles * TM, width), lambda i: (jnp.minimum(i, n_p - 1), 0)),
            pl.BlockSpec((tiles * TM, width), lambda i: (jnp.maximum(i - n_p, 0), 0))]


def _pre_kernel(*refs):
    x_refs, (mod_ref, g_ref, w_ref, o_ref) = refs[:-4], refs[-4:]
    y = _rms(_by_group(pl.program_id(0), x_refs)) * g_ref[...]
    h = y * (1.0 + mod_ref[1:2, :]) + mod_ref[0:1, :]
    o_ref[...] = _dot(h.astype(BF16), w_ref[...])


def _pre_project(x_src, mod_l, g, w_bf):
    n_out = w_bf.shape[1]
    if isinstance(x_src, tuple):
        x_specs, x_args = _group_specs(D_MODEL), list(x_src)
    else:
        x_specs, x_args = [pl.BlockSpec((TM, D_MODEL), lambda i: (i, 0))], [x_src]
    return pl.pallas_call(
        _pre_kernel,
        out_shape=jax.ShapeDtypeStruct((N_TOK, n_out), F32),
        grid=(N_TILES,),
        in_specs=x_specs + [
            pl.BlockSpec((None, 6, D_MODEL), lambda i: (_mod_row(i), 0, 0)),
            pl.BlockSpec((1, D_MODEL), lambda i: (0, 0)),
            pl.BlockSpec((D_MODEL, n_out), lambda i: (0, 0)),
        ],
        out_specs=pl.BlockSpec((TM, n_out), lambda i: (i, 0)),
        compiler_params=pltpu.CompilerParams(
            dimension_semantics=("arbitrary",), vmem_limit_bytes=VMEM_LIMIT),
    )(*x_args, mod_l, g.reshape(1, D_MODEL), w_bf)


def _softmax_pv(parts):
    m = None
    for s, _ in parts:
        mi = jnp.max(s, axis=-1, keepdims=True)
        m = mi if m is None else jnp.maximum(m, mi)
    acc = None
    den = None
    for s, v in parts:
        p = jnp.exp(s - m)
        d = jnp.sum(p, axis=-1, keepdims=True)
        o = _dot(p.astype(BF16), v)
        acc = o if acc is None else acc + o
        den = d if den is None else den + d
    return acc / den


def _ctx_attn_kernel(q_ref, k_ref, v_ref, o_ref):
    q = q_ref[...]
    kb = k_ref[...].astype(BF16)
    vb = v_ref[...].astype(BF16)
    lane = lax.broadcasted_iota(jnp.int32, q.shape, 1)
    out = jnp.zeros(q.shape, F32)
    for h in range(2):
        hm = (lane >= NA_HEAD_DIM) if h == 1 else (lane < NA_HEAD_DIM)
        qm = jnp.where(hm, q, 0.0).astype(BF16)
        s = _dot_nt(qm, kb) * NA_SCALE
        out = jnp.where(hm, _softmax_pv([(s, vb)]), out)
    o_ref[...] = out


def _context_attention(p_all):
    n_hp = NA_HEADS // 2
    return pl.pallas_call(
        _ctx_attn_kernel,
        out_shape=jax.ShapeDtypeStruct((N_PROMPT, NA_W), F32),
        grid=(BATCH, n_hp),
        in_specs=[
            pl.BlockSpec((SEQ, LANES), lambda b, hp: (b, hp)),
            pl.BlockSpec((SEQ, LANES), lambda b, hp: (b, n_hp + hp)),
            pl.BlockSpec((SEQ, LANES), lambda b, hp: (b, 2 * n_hp + hp)),
        ],
        out_specs=pl.BlockSpec((SEQ, LANES), lambda b, hp: (b, hp)),
        compiler_params=pltpu.CompilerParams(
            dimension_semantics=("arbitrary", "arbitrary"), vmem_limit_bytes=VMEM_LIMIT),
    )(p_all, p_all, p_all)


N_BIAS_VAR = MAX_WR
WIN_KEYS = MAX_WR * GRID_W


def _natten_kernel(q_ref, k_ref, v_ref, ck_ref, cv_ref, bias_ref, o_ref):
    ckb = ck_ref[...].astype(BF16)
    cvb = cv_ref[...].astype(BF16)
    lane = lax.broadcasted_iota(jnp.int32, (GRID_W, LANES), 1)

    def row(r, carry):
        r0 = jnp.clip(r - MAX_WR // 2, 0, GRID_ROWS - MAX_WR)
        var = r0 - r + (MAX_WR - 1)
        q = q_ref[pl.ds(pl.multiple_of(r * GRID_W, GRID_W), GRID_W), :]
        kw = k_ref[pl.ds(pl.multiple_of(r0 * GRID_W, GRID_W), WIN_KEYS), :].astype(BF16)
        vw = v_ref[pl.ds(pl.multiple_of(r0 * GRID_W, GRID_W), WIN_KEYS), :].astype(BF16)
        first = lane < NA_HEAD_DIM
        q2 = jnp.concatenate([jnp.where(first, q, 0.0), jnp.where(first, 0.0, q)], axis=0).astype(BF16)
        s_loc = _dot_nt(q2, kw) * NA_SCALE + bias_ref[var]
        s_ctx = _dot_nt(q2, ckb) * NA_SCALE
        o2 = _softmax_pv([(s_loc, vw), (s_ctx, cvb)])
        o_ref[pl.ds(pl.multiple_of(r * GRID_W, GRID_W), GRID_W), :] = jnp.where(first, o2[:GRID_W], o2[GRID_W:])
        return carry

    lax.fori_loop(0, GRID_ROWS, row, 0, unroll=4)


def _natten_bias(rpb):
    qc = np.arange(GRID_W)[:, None]
    kc = np.arange(GRID_W)[None, :]
    ws = np.clip(qc - WIN_W // 2, 0, GRID_W - WIN_W)
    valid = (kc >= ws) & (kc < ws + WIN_W)
    dc = np.clip(kc - qc + WIN_W - 1, 0, 2 * WIN_W - 2)
    t = rpb.astype(F32)[:, :, dc]
    t = jnp.where(jnp.asarray(valid)[None, None], t, NEG_INF)
    dr = np.arange(N_BIAS_VAR)[:, None] + np.arange(MAX_WR)[None, :]
    slab = t[:, dr]
    slab = slab.transpose(0, 1, 3, 2, 4).reshape(NA_HEADS // 2, 2, N_BIAS_VAR, GRID_W, WIN_KEYS)
    return slab.transpose(0, 2, 1, 3, 4).reshape(NA_HEADS // 2, N_BIAS_VAR, 2 * GRID_W, WIN_KEYS)


def _neighbourhood_attention(p_all, ck, cv, bias):
    n_hp = NA_HEADS // 2
    off = N_PROMPT // DEC_SEQ
    return pl.pallas_call(
        _natten_kernel,
        out_shape=jax.ShapeDtypeStruct((N_SAMPLE, NA_W), F32),
        grid=(DEC_BATCH, n_hp),
        in_specs=[
            pl.BlockSpec((DEC_SEQ, LANES), lambda b, hp: (off + b, hp)),
            pl.BlockSpec((DEC_SEQ, LANES), lambda b, hp: (off + b, n_hp + hp)),
            pl.BlockSpec((DEC_SEQ, LANES), lambda b, hp: (off + b, 2 * n_hp + hp)),
            pl.BlockSpec((None, PAST_LEN, LANES), lambda b, hp: (b, 0, hp)),
            pl.BlockSpec((None, PAST_LEN, LANES), lambda b, hp: (b, 0, hp)),
            pl.BlockSpec((None, N_BIAS_VAR, 2 * GRID_W, WIN_KEYS), lambda b, hp: (hp, 0, 0, 0)),
        ],
        out_specs=pl.BlockSpec((DEC_SEQ, LANES), lambda b, hp: (b, hp)),
        compiler_params=pltpu.CompilerParams(
            dimension_semantics=("arbitrary", "arbitrary"), vmem_limit_bytes=VMEM_LIMIT),
    )(p_all, p_all, p_all, ck, cv, bias)


def _dft_tables(t):
    j = np.arange(t, dtype=np.int64)
    ang = 2.0 * np.pi * ((j[:, None] * j[None, :]) % t).astype(np.float64) / t
    return np.cos(ang), np.sin(ang)


def _fourier_kernel(z_ref, lt_ref, rc_ref, rs_ref, o_ref, ab_ref, *, t_len, scale):
    @pl.when(pl.program_id(1) == 0)
    def _():
        z = z_ref[...].astype(BF16)
        for g in range(FN_GROUPS):
            zg = z[:, g * FN_GROUP_W:(g + 1) * FN_GROUP_W]
            ab_ref[0:t_len, g * FN_GROUP_W:(g + 1) * FN_GROUP_W] = _dot(zg, rc_ref[...]).astype(BF16)
            ab_ref[t_len:2 * t_len, g * FN_GROUP_W:(g + 1) * FN_GROUP_W] = _dot(zg, rs_ref[...]).astype(BF16)

    o_ref[...] = _dot(lt_ref[...], ab_ref[...]) * scale


def _fourier_mix(p_all, row_off, n_batch, t_len):
    ct, st = _dft_tables(t_len)
    lt = jnp.asarray(np.concatenate([ct, -st], axis=1), dtype=F32).astype(BF16)
    cc, sc = _dft_tables(FN_GROUP_W)
    rc = jnp.asarray(cc, dtype=F32).astype(BF16)
    rs = jnp.asarray(sc, dtype=F32).astype(BF16)
    tr = min(t_len, 512)
    n_rt = t_len // tr
    off = row_off // t_len
    kern = functools.partial(_fourier_kernel, t_len=t_len, scale=float((t_len * FN_GROUP_W) ** -0.5))
    return pl.pallas_call(
        kern,
        out_shape=jax.ShapeDtypeStruct((n_batch * t_len, FN_W), F32),
        grid=(n_batch, n_rt),
        in_specs=[
            pl.BlockSpec((t_len, FN_W), lambda b, i: (off + b, 0)),
            pl.BlockSpec((tr, 2 * t_len), lambda b, i: (i, 0)),
            pl.BlockSpec((FN_GROUP_W, FN_GROUP_W), lambda b, i: (0, 0)),
            pl.BlockSpec((FN_GROUP_W, FN_GROUP_W), lambda b, i: (0, 0)),
        ],
        out_specs=pl.BlockSpec((tr, FN_W), lambda b, i: (b * n_rt + i, 0)),
        scratch_shapes=[pltpu.VMEM((2 * t_len, FN_W), BF16)],
        compiler_params=pltpu.CompilerParams(
            dimension_semantics=("arbitrary", "arbitrary"), vmem_limit_bytes=VMEM_LIMIT),
    )(p_all, lt, rc, rs)


N_SUB = HG_CHUNK // HG_SUB
LOG2_E = 1.4426950408889634


def _gla_chunk(q, v, z, lb, st, backward):
    c = HG_CHUNK
    sg, sgn = _sigmoid_pair(z)
    lf = jnp.log(lb + (1.0 - lb) * sg)
    kk = (1.0 - lb) * sgn
    row = lax.broadcasted_iota(jnp.int32, (c, c), 0)
    col = lax.broadcasted_iota(jnp.int32, (c, c), 1)
    tri = ((col >= row) if backward else (col <= row)).astype(BF16)
    hi, mid, lo = _split3(lf)
    b = _dot(tri, hi) + _dot(tri, mid) + _dot(tri, lo)
    o_inter = _dot_nt((q * jnp.exp(b)).astype(BF16), st.astype(BF16))
    b2 = b * LOG2_E
    c2 = b2 - jnp.log2(kk)

    sub_row = lax.broadcasted_iota(jnp.int32, (HG_SUB, c), 0)
    sub_col = lax.broadcasted_iota(jnp.int32, (HG_SUB, c), 1)
    att_rows = []
    for i in range(N_SUB):
        lo_r, hi_r = i * HG_SUB, (i + 1) * HG_SUB
        q_i = q[lo_r:hi_r]
        b_i = b[lo_r:hi_r]
        has_off = (i < N_SUB - 1) if backward else (i > 0)
        if has_off:
            r_i = b[hi_r:hi_r + 1] if backward else b[lo_r - 1:lo_r]
            qs = (q_i * jnp.exp(b_i - r_i)).astype(BF16)
            ks = (kk * jnp.exp(jnp.minimum(r_i - b, 0.0))).astype(BF16)
            a = _dot_nt(qs, ks)
            off_mask = (sub_col >= hi_r) if backward else (sub_col < lo_r)
            a = jnp.where(off_mask, a, 0.0)
        else:
            a = jnp.zeros((HG_SUB, c), F32)
        b2_i = b2[lo_r:hi_r]
        c2_i = c2[lo_r:hi_r]
        for s in range(HG_SUB):
            colsum = jnp.sum(q_i * jnp.exp2(b2_i - c2_i[s:s + 1]), axis=-1, keepdims=True)
            keep = (sub_row <= s) if backward else (sub_row >= s)
            a = jnp.where((sub_col == lo_r + s) & keep, colsum, a)
        att_rows.append(a)
    att = jnp.concatenate(att_rows, axis=0)
    o = o_inter + _dot(att.astype(BF16), v.astype(BF16))

    b_last = b[0:1] if backward else b[c - 1:c]
    kd = (kk * jnp.exp(b_last - b)).astype(BF16)
    st_new = st * jnp.exp(b_last) + _dot_tn(v.astype(BF16), kd)
    return o, st_new


def _hgrn_kernel(q_ref, i_ref, zf_ref, zb_ref, g_ref, lb_ref, s0_ref, o_ref, sfin_ref, of_ref, ob_ref,
                 stf_ref, stb_ref, *, t_len):
    n_chunk = t_len // HG_CHUNK

    def sl(c):
        return pl.ds(pl.multiple_of(c * HG_CHUNK, HG_CHUNK), HG_CHUNK)

    stf_ref[...] = s0_ref[0]
    stb_ref[...] = s0_ref[1]

    def step(j, carry):
        c = n_chunk - 1 - j
        o_f, st_f = _gla_chunk(q_ref[sl(j), :], i_ref[sl(j), :], zf_ref[sl(j), :], lb_ref[0:1, :],
                               stf_ref[...], backward=False)
        o_b, st_b = _gla_chunk(q_ref[sl(c), :], i_ref[sl(c), :], zb_ref[sl(c), :], lb_ref[1:2, :],
                               stb_ref[...], backward=True)
        of_ref[sl(j), :] = o_f
        ob_ref[sl(c), :] = o_b
        stf_ref[...] = st_f
        stb_ref[...] = st_b
        return carry

    lax.fori_loop(0, n_chunk, step, 0, unroll=4)
    sfin_ref[0] = stf_ref[...]
    sfin_ref[1] = stb_ref[...]

    def finish(c, carry):
        tot = of_ref[sl(c), :] + ob_ref[sl(c), :]
        o_ref[sl(c), :] = _rms(tot) * _silu(g_ref[sl(c), :])
        return carry

    lax.fori_loop(0, n_chunk, finish, 0)


def _hgrn2_bidir(p_all, lb, s0_t, row_off, n_batch, t_len):
    off = row_off // t_len
    cb = FN_W // LANES

    def col(j):
        return pl.BlockSpec((t_len, LANES), lambda b, h: (off + b, cb + j * HG_HEADS + h))

    kern = functools.partial(_hgrn_kernel, t_len=t_len)
    state_spec = pl.BlockSpec((None, 2, None, HG_DK, HG_DK), lambda b, h: (b, 0, h, 0, 0))
    return pl.pallas_call(
        kern,
        out_shape=(jax.ShapeDtypeStruct((n_batch * t_len, HG_W), F32),
                   jax.ShapeDtypeStruct((n_batch, 2, HG_HEADS, HG_DK, HG_DK), F32)),
        grid=(n_batch, HG_HEADS),
        in_specs=[col(0), col(1), col(2), col(3), col(4),
                  pl.BlockSpec((2, LANES), lambda b, h: (0, h)),
                  state_spec],
        out_specs=(pl.BlockSpec((t_len, LANES), lambda b, h: (b, h)), state_spec),
        scratch_shapes=[pltpu.VMEM((t_len, LANES), F32), pltpu.VMEM((t_len, LANES), F32),
                        pltpu.VMEM((HG_DK, HG_DK), F32), pltpu.VMEM((HG_DK, HG_DK), F32)],
        compiler_params=pltpu.CompilerParams(
            dimension_semantics=("arbitrary", "arbitrary"), vmem_limit_bytes=VMEM_LIMIT),
    )(p_all, p_all, p_all, p_all, p_all, lb, s0_t)


def _route(h2, rw_ref, rb_ref):
    hi, mid, _ = _split3(h2)
    w_hi = rw_ref[0]
    w_lo = rw_ref[1]
    logits = _dot_nt(w_hi, hi) + _dot_nt(w_hi, mid) + _dot_nt(w_lo, hi) + rb_ref[:, 0:1]
    width = h2.shape[0]
    sub = lax.broadcasted_iota(jnp.int32, (N_EXPERTS, width), 0)
    work = logits
    vals, hots = [], []
    for k in range(TOP_K):
        m = jnp.max(work, axis=0, keepdims=True)
        idx = jnp.min(jnp.where(work == m, sub, N_EXPERTS), axis=0, keepdims=True)
        hot = sub == idx
        work = jnp.where(hot, -jnp.inf, work)
        vals.append(m)
        hots.append(hot)
    es = [jnp.exp(v - vals[0]) for v in vals]
    den = es[0] + es[1] + es[2] + es[3]
    cnt = jnp.zeros((N_EXPERTS, width), F32)
    for hot in hots:
        cnt = cnt + hot.astype(F32)
    row = lax.broadcasted_iota(jnp.int32, (TM, TM), 0)
    col = lax.broadcasted_iota(jnp.int32, (TM, TM), 1)
    earlier = (row < col).astype(BF16)
    erow = lax.broadcasted_iota(jnp.int32, (N_EXPERTS, N_EXPERTS), 0)
    ecol = lax.broadcasted_iota(jnp.int32, (N_EXPERTS, N_EXPERTS), 1)
    lower = (ecol < erow).astype(BF16)
    where_to, tile_cnts = [], []
    for t in range(width // TM):
        cnt_t = cnt[:, t * TM:(t + 1) * TM]
        before = _dot(cnt_t.astype(BF16), earlier)
        tile_cnt = jnp.sum(cnt_t, axis=1, keepdims=True)
        run_off = _dot(lower, jnp.broadcast_to(tile_cnt, (N_EXPERTS, LANES)).astype(BF16))[:, 0:1]
        where_to.append(before + run_off)
        tile_cnts.append(tile_cnt)
    where_to = where_to[0] if len(where_to) == 1 else jnp.concatenate(where_to, axis=1)
    rsub = lax.broadcasted_iota(jnp.int32, (ROUTE_ROWS, width), 0)
    rows = jnp.zeros((ROUTE_ROWS, width), F32)
    for k in range(TOP_K):
        pk = jnp.sum(jnp.where(hots[k], where_to, 0.0), axis=0, keepdims=True)
        rows = jnp.where(rsub == k, pk, rows)
        rows = jnp.where(rsub == TOP_K + k, es[k] / den, rows)
    return rows, tile_cnts


def _post_kernel(*refs, with_conv, n_x):
    refs = list(refs)
    a_refs, refs = refs[:2], refs[2:]
    if with_conv:
        (c_ref, cprev_ref, cnext_ref, cw_ref), refs = refs[:4], refs[4:]
    else:
        b_refs, refs = refs[:2], refs[2:]
    x_refs, refs = refs[:n_x], refs[n_x:]
    mod_ref, g_ref, wo_ref, rw_ref, rb_ref, xo_ref, h2_ref, rows_ref, pg_ref, cnt_ref = refs
    i = pl.program_id(0)
    is_prompt = i < PROMPT_TILES // POST_TILES

    def grp(refs_, rows):
        if len(refs_) == 1:
            return refs_[0][rows, :]
        return jnp.where(is_prompt, refs_[0][rows, :], refs_[1][rows, :])

    if with_conv:
        zs = [c_ref[t * TM:(t + 1) * TM, SC_W:2 * SC_W] * c_ref[t * TM:(t + 1) * TM, 2 * SC_W:3 * SC_W]
              for t in range(POST_TILES)]
        z_before = cprev_ref[SUBLANES - 1:SUBLANES, SC_W:2 * SC_W] * cprev_ref[SUBLANES - 1:SUBLANES, 2 * SC_W:3 * SC_W]
        z_after = cnext_ref[0:1, SC_W:2 * SC_W] * cnext_ref[0:1, 2 * SC_W:3 * SC_W]
    pick = (lax.broadcasted_iota(jnp.int32, (ROUTE_ROWS, LANES), 0)
            == lax.broadcasted_iota(jnp.int32, (ROUTE_ROWS, LANES), 1)).astype(BF16)

    everything = slice(None)
    if with_conv:
        parts = []
        for t in range(POST_TILES):
            j = i * POST_TILES + t - PROMPT_TILES
            seq_start = is_prompt | (j % TILES_PER_DEC_SEQ == 0)
            seq_end = is_prompt | (j % TILES_PER_DEC_SEQ == TILES_PER_DEC_SEQ - 1)
            z = zs[t]
            z_prev = jnp.where(seq_start, 0.0, z_before if t == 0 else zs[t - 1][TM - 1:TM])
            z_next = jnp.where(seq_end, 0.0, z_after if t == POST_TILES - 1 else zs[t + 1][0:1])
            row = lax.broadcasted_iota(jnp.int32, (TM, SC_W), 0)
            zm1 = jnp.where(row == 0, z_prev, pltpu.roll(z, 1, axis=0))
            zp1 = jnp.where(row == TM - 1, z_next, pltpu.roll(z, TM - 1, axis=0))
            parts.append(c_ref[t * TM:(t + 1) * TM, 0:SC_W]
                         * (zm1 * cw_ref[0:1, :] + z * cw_ref[1:2, :] + zp1 * cw_ref[2:3, :]))
        b_val = jnp.concatenate(parts, axis=0)
    else:
        b_val = grp(b_refs, everything)
    a_val = grp(a_refs, everything)
    half = a_val.shape[1]
    mix = _dot(a_val.astype(BF16), wo_ref[0:half, :]) + _dot(b_val.astype(BF16), wo_ref[half:, :])
    x_new = grp(x_refs, everything) + mod_ref[2:3, :] * mix
    xo_ref[...] = x_new
    h2 = _rms(x_new) * g_ref[...] * (1.0 + mod_ref[4:5, :]) + mod_ref[3:4, :]
    h2_ref[...] = h2
    route, tile_cnts = _route(h2, rw_ref, rb_ref)
    r_hi, r_mid, r_lo = _split3(route)
    pg_ref[...] = _dot_tn(r_hi, pick) + _dot_tn(r_mid, pick) + _dot_tn(r_lo, pick)
    for t in range(POST_TILES):
        rows_ref[t * ROUTE_ROWS:(t + 1) * ROUTE_ROWS, :] = route[:, t * TM:(t + 1) * TM]
        cnt_ref[t] = jnp.broadcast_to(tile_cnts[t], (N_EXPERTS, LANES)).astype(jnp.int32)


def _post_mix(a_pair, b_src, conv_w, x_src, mod_l, g2, wo_bf, rw2, rb, with_conv):
    step_rows = POST_TILES * TM
    tile = lambda w: pl.BlockSpec((step_rows, w), lambda i: (i, 0))
    full = lambda shp: pl.BlockSpec(shp, lambda i: tuple(0 for _ in shp))
    rows8 = step_rows // SUBLANES
    n8 = N_TOK // SUBLANES
    in_specs = _group_specs(a_pair[0].shape[1], POST_TILES)
    args = list(a_pair)
    if with_conv:
        cw = 3 * SC_W
        in_specs += [
            pl.BlockSpec((step_rows, cw), lambda i: (i, 1)),
            pl.BlockSpec((SUBLANES, cw), lambda i: (jnp.maximum(i * rows8 - 1, 0), 1)),
            pl.BlockSpec((SUBLANES, cw), lambda i: (jnp.minimum((i + 1) * rows8, n8 - 1), 1)),
            full((3, SC_W)),
        ]
        args += [b_src, b_src, b_src, conv_w]
    else:
        in_specs += _group_specs(b_src[0].shape[1], POST_TILES)
        args += list(b_src)
    if isinstance(x_src, tuple):
        in_specs += _group_specs(D_MODEL, POST_TILES)
        args += list(x_src)
        n_x = 2
    else:
        in_specs += [tile(D_MODEL)]
        args += [x_src]
        n_x = 1
    in_specs += [
        pl.BlockSpec((None, 6, D_MODEL), lambda i: (_mod_row(i * POST_TILES), 0, 0)),
        full((1, D_MODEL)),
        full((D_MODEL, D_MODEL)),
        full((2, N_EXPERTS, D_MODEL)),
        full((N_EXPERTS, LANES)),
    ]
    args += [mod_l, g2.reshape(1, D_MODEL), wo_bf, rw2, rb]
    out_shape = (
        jax.ShapeDtypeStruct((N_TOK, D_MODEL), F32),
        jax.ShapeDtypeStruct((N_TOK, D_MODEL), F32),
        jax.ShapeDtypeStruct((N_TILES * ROUTE_ROWS, TM), F32),
        jax.ShapeDtypeStruct((N_TOK, LANES), F32),
        jax.ShapeDtypeStruct((N_TILES, N_EXPERTS, LANES), jnp.int32),
    )
    out_specs = (tile(D_MODEL), tile(D_MODEL),
                 pl.BlockSpec((POST_TILES * ROUTE_ROWS, TM), lambda i: (i, 0)), tile(LANES),
                 pl.BlockSpec((POST_TILES, N_EXPERTS, LANES), lambda i: (i, 0, 0)))
    return pl.pallas_call(
        functools.partial(_post_kernel, with_conv=with_conv, n_x=n_x),
        out_shape=out_shape,
        grid=(N_TILES // POST_TILES,),
        in_specs=in_specs,
        out_specs=out_specs,
        compiler_params=pltpu.CompilerParams(
            dimension_semantics=("arbitrary",), vmem_limit_bytes=VMEM_LIMIT),
    )(*args)


ROW_SUB = D_MODEL // LANES
TILE_ASG = TM * TOP_K
RUN_BITS = TM.bit_length()
N_TAIL_BLOCKS = N_BLK - N_ASG // MOE_BLK
N_ZERO_BLOCKS = N_EXPERTS + N_TAIL_BLOCKS


def _rows_to_wide(ref, n_rows, lead=None):
    parts = []
    for j in range(ROW_SUB):
        idx = (pl.ds(j, n_rows, stride=ROW_SUB), slice(None))
        parts.append(ref[idx] if lead is None else ref[(lead,) + idx])
    return jnp.concatenate(parts, axis=1)


def _wide_to_rows(ref, val, lead=None):
    n_rows = val.shape[0]
    for j in range(ROW_SUB):
        idx = (pl.ds(j, n_rows, stride=ROW_SUB), slice(None))
        if lead is not None:
            idx = (lead,) + idx
        ref[idx] = val[:, j * LANES:(j + 1) * LANES]


def _row_window(ref, start_row, n_rows, lead=None):
    idx = (pl.ds(pl.multiple_of(start_row * ROW_SUB, ROW_SUB), n_rows * ROW_SUB), slice(None))
    return ref.at[idx] if lead is None else ref.at[(lead,) + idx]


def _run_copies(tile, rdst_ref, roff_ref, rlen_ref, make_copy):
    def per_expert(e, carry):
        r = tile * N_EXPERTS + e
        n = rlen_ref[r]
        src = roff_ref[r]
        dst = rdst_ref[r]
        for bit in reversed(range(RUN_BITS)):
            @pl.when((n & (1 << bit)) != 0)
            def _():
                done = n & ~((2 << bit) - 1)
                make_copy(src + done, dst + done, 1 << bit).start()
        return carry

    lax.fori_loop(0, N_EXPERTS, per_expert, 0)


def _dispatch_kernel(rdst_ref, roff_ref, rlen_ref, zrow_ref, h2_ref, rows_ref, xs_hbm,
                     stage, zbuf, sem, zsem):
    i = pl.program_id(0)
    n = pl.num_programs(0)
    slot = i % 2

    def zero_copy(e):
        return pltpu.make_async_copy(zbuf, _row_window(xs_hbm, zrow_ref[e], MOE_BLK), zsem)

    @pl.when(i == 0)
    def _():
        zbuf[...] = jnp.zeros_like(zbuf)

        def start(e, carry):
            @pl.when(zrow_ref[e] >= 0)
            def _():
                zero_copy(e).start()
            return carry

        def wait(e, carry):
            @pl.when(zrow_ref[e] >= 0)
            def _():
                zero_copy(e).wait()
            return carry

        lax.fori_loop(0, N_ZERO_BLOCKS, start, 0)
        lax.fori_loop(0, N_ZERO_BLOCKS, wait, 0)

    def wait_tile(slot_):
        pltpu.make_async_copy(stage.at[slot_], _row_window(xs_hbm, 0, TILE_ASG), sem.at[slot_]).wait()

    @pl.when(i >= 2)
    def _():
        wait_tile(slot)

    pos = rows_ref[0:TOP_K, :].astype(jnp.int32)
    sub = lax.broadcasted_iota(jnp.int32, (TILE_ASG, TM), 0)
    hit = sub == pos[0:1, :]
    for k in range(1, TOP_K):
        hit = hit | (sub == pos[k:k + 1, :])
    sorted_rows = _dot(hit.astype(BF16), h2_ref[...].astype(BF16))
    _wide_to_rows(stage, sorted_rows, lead=slot)

    _run_copies(i, rdst_ref, roff_ref, rlen_ref,
                lambda s, d, nr: pltpu.make_async_copy(_row_window(stage, s, nr, lead=slot),
                                                       _row_window(xs_hbm, d, nr), sem.at[slot]))

    @pl.when(i == n - 1)
    def _():
        wait_tile(slot)
        wait_tile(1 - slot)


def _moe_dispatch(tables, h2, route_rows):
    rdst, roff, rlen, zrow = tables
    grid_spec = pltpu.PrefetchScalarGridSpec(
        num_scalar_prefetch=4,
        grid=(N_TILES,),
        in_specs=[
            pl.BlockSpec((TM, D_MODEL), lambda i, *_: (i, 0)),
            pl.BlockSpec((ROUTE_ROWS, TM), lambda i, *_: (i, 0)),
        ],
        out_specs=pl.BlockSpec(memory_space=pl.ANY),
        scratch_shapes=[
            pltpu.VMEM((2, TILE_ASG * ROW_SUB, LANES), F32),
            pltpu.VMEM((MOE_BLK * ROW_SUB, LANES), F32),
            pltpu.SemaphoreType.DMA((2,)),
            pltpu.SemaphoreType.DMA(()),
        ],
    )
    return pl.pallas_call(
        _dispatch_kernel,
        out_shape=jax.ShapeDtypeStruct((N_SLOT * ROW_SUB, LANES), F32),
        grid_spec=grid_spec,
        compiler_params=pltpu.CompilerParams(
            dimension_semantics=("arbitrary",), vmem_limit_bytes=VMEM_LIMIT, has_side_effects=True),
    )(rdst, roff, rlen, zrow, h2, route_rows)


def _ffn_kernel(bexp_ref, brows_ref, xs_ref, w1_ref, b1_ref, w2_ref, b2_ref, ys_ref, w1b, w2b):
    b = pl.program_id(0)
    rows = brows_ref[b]
    half = MOE_BLK // 2

    def mlp(n_rows):
        x = _rows_to_wide(xs_ref, n_rows).astype(BF16)
        u = _dot(x, w1b[...]) + b1_ref[...]
        glu = jnp.minimum(u[:, :D_FF], SWIGLU_LIMIT)
        lin = jnp.clip(u[:, D_FF:], -SWIGLU_LIMIT, SWIGLU_LIMIT)
        act = glu * _sigmoid_pair(SWIGLU_ALPHA * glu)[0] * (lin + 1.0)
        _wide_to_rows(ys_ref, _dot(act.astype(BF16), w2b[...]) + b2_ref[...])

    def zero_from(first_row):
        n = (MOE_BLK - first_row) * ROW_SUB
        ys_ref[pl.ds(first_row * ROW_SUB, n), :] = jnp.zeros((n, LANES), F32)

    @pl.when(rows > 0)
    def _():
        prev = jnp.maximum(b - 1, 0)

        @pl.when((b == 0) | (bexp_ref[b] != bexp_ref[prev]))
        def _():
            w1b[...] = w1_ref[...].astype(BF16)
            w2b[...] = w2_ref[...].astype(BF16)

    @pl.when(rows > half)
    def _():
        mlp(MOE_BLK)

    @pl.when((rows > 0) & (rows <= half))
    def _():
        mlp(half)
        zero_from(half)

    @pl.when(rows == 0)
    def _():
        zero_from(0)


def _moe_experts(layer, blk_expert, blk_rows, xs, w1, b1, w2, b2):
    def blk(b, be, br):
        return (b, 0)

    def wsel(b, be, br):
        return (layer, be[b], 0, 0)

    grid_spec = pltpu.PrefetchScalarGridSpec(
        num_scalar_prefetch=2,
        grid=(N_BLK,),
        in_specs=[
            pl.BlockSpec((MOE_BLK * ROW_SUB, LANES), blk),
            pl.BlockSpec((None, None, D_MODEL, 2 * D_FF), wsel),
            pl.BlockSpec((None, None, 1, 2 * D_FF), wsel),
            pl.BlockSpec((None, None, D_FF, D_MODEL), wsel),
            pl.BlockSpec((None, None, 1, D_MODEL), wsel),
        ],
        out_specs=pl.BlockSpec((MOE_BLK * ROW_SUB, LANES), blk),
        scratch_shapes=[
            pltpu.VMEM((D_MODEL, 2 * D_FF), BF16),
            pltpu.VMEM((D_FF, D_MODEL), BF16),
        ],
    )
    return pl.pallas_call(
        _ffn_kernel,
        out_shape=jax.ShapeDtypeStruct((N_SLOT * ROW_SUB, LANES), F32),
        grid_spec=grid_spec,
        compiler_params=pltpu.CompilerParams(
            dimension_semantics=("arbitrary",), vmem_limit_bytes=VMEM_LIMIT),
    )(blk_expert, blk_rows, xs, w1, b1, w2, b2)


def _combine_kernel(rdst_ref, roff_ref, rlen_ref, ys_hbm, pg_ref, x_ref, mod_ref, g_ref, *rest,
                    final):
    if final:
        yp_ref, ysm_ref, stage, sem = rest
    else:
        o_ref, stage, sem = rest
    i = pl.program_id(0)
    n = pl.num_programs(0)
    slot = i % 2

    def fetch(tile, slot_):
        _run_copies(tile, rdst_ref, roff_ref, rlen_ref,
                    lambda s, d, nr: pltpu.make_async_copy(_row_window(ys_hbm, d, nr),
                                                           _row_window(stage, s, nr, lead=slot_), sem.at[slot_]))

    @pl.when(i == 0)
    def _():
        fetch(0, 0)

    pltpu.make_async_copy(_row_window(ys_hbm, 0, TILE_ASG), stage.at[slot], sem.at[slot]).wait()

    @pl.when(i + 1 < n)
    def _():
        fetch(i + 1, 1 - slot)

    pg = pg_ref[...]
    pos = pg[:, 0:TOP_K].astype(jnp.int32)
    gates = pg[:, TOP_K:2 * TOP_K]
    lane = lax.broadcasted_iota(jnp.int32, (TM, TILE_ASG), 1)
    gmat = jnp.zeros((TM, TILE_ASG), F32)
    for k in range(TOP_K):
        gmat = jnp.where(lane == pos[:, k:k + 1], gates[:, k:k + 1], gmat)
    g_hi = gmat.astype(BF16)
    g_lo = (gmat - g_hi.astype(F32)).astype(BF16)
    cols = []
    for jj in range(ROW_SUB // 2):
        s = jnp.concatenate([stage[slot, pl.ds(2 * jj, TILE_ASG, stride=ROW_SUB), :],
                             stage[slot, pl.ds(2 * jj + 1, TILE_ASG, stride=ROW_SUB), :]], axis=1)
        s_bf = s.astype(BF16)
        cols.append(_dot(g_hi, s_bf) + _dot(g_lo, s_bf))
    moe = jnp.concatenate(cols, axis=1)
    x = x_ref[...] + mod_ref[5:6, :] * moe
    if final:
        x = _rms(x) * g_ref[...]

        @pl.when(i < PROMPT_TILES)
        def _():
            yp_ref[...] = x

        @pl.when(i >= PROMPT_TILES)
        def _():
            ysm_ref[...] = x
    else:
        o_ref[...] = x


def _moe_combine(tables, ys, route_tok, x_all, mod_l, final_g, final):
    rdst, roff, rlen, _ = tables
    tile = pl.BlockSpec((TM, D_MODEL), lambda i, *_: (i, 0))
    if final:
        out_shape = (jax.ShapeDtypeStruct((N_PROMPT, D_MODEL), F32), jax.ShapeDtypeStruct((N_SAMPLE, D_MODEL), F32))
        out_specs = (pl.BlockSpec((TM, D_MODEL), lambda i, *_: (jnp.minimum(i, PROMPT_TILES - 1), 0)),
                     pl.BlockSpec((TM, D_MODEL), lambda i, *_: (jnp.maximum(i - PROMPT_TILES, 0), 0)))
    else:
        out_shape = jax.ShapeDtypeStruct((N_TOK, D_MODEL), F32)
        out_specs = tile
    grid_spec = pltpu.PrefetchScalarGridSpec(
        num_scalar_prefetch=3,
        grid=(N_TILES,),
        in_specs=[
            pl.BlockSpec(memory_space=pl.ANY),
            pl.BlockSpec((TM, LANES), lambda i, *_: (i, 0)),
            tile,
            pl.BlockSpec((None, 6, D_MODEL), lambda i, *_: (_mod_row(i), 0, 0)),
            pl.BlockSpec((1, D_MODEL), lambda i, *_: (0, 0)),
        ],
        out_specs=out_specs,
        scratch_shapes=[
            pltpu.VMEM((2, TILE_ASG * ROW_SUB, LANES), F32),
            pltpu.SemaphoreType.DMA((2,)),
        ],
    )
    return pl.pallas_call(
        functools.partial(_combine_kernel, final=final),
        out_shape=out_shape,
        grid_spec=grid_spec,
        compiler_params=pltpu.CompilerParams(
            dimension_semantics=("arbitrary",), vmem_limit_bytes=VMEM_LIMIT),
    )(rdst, roff, rlen, ys, route_tok, x_all, mod_l, final_g.reshape(1, D_MODEL))


def _moe_layout(tile_cnt):
    cnt = tile_cnt[:, :, 0]
    total = jnp.sum(cnt, axis=0)
    padded = (total + MOE_BLK - 1) // MOE_BLK * MOE_BLK
    pad_end = jnp.cumsum(padded)
    pad_start = pad_end - padded
    before = jnp.cumsum(cnt, axis=0) - cnt
    run_dst = (pad_start[None, :] + before).reshape(-1).astype(jnp.int32)
    run_off = (jnp.cumsum(cnt, axis=1) - cnt).reshape(-1).astype(jnp.int32)
    run_len = cnt.reshape(-1).astype(jnp.int32)
    last_blk = jnp.where((padded > 0) & (total < padded), pad_end - MOE_BLK, -1)
    tail = pad_end[-1] + jnp.arange(N_TAIL_BLOCKS) * MOE_BLK
    zero_row = jnp.concatenate([last_blk, jnp.where(tail < N_SLOT, tail, -1)]).astype(jnp.int32)
    blk_start = jnp.arange(N_BLK, dtype=jnp.int32) * MOE_BLK
    blk_expert = jnp.minimum(jnp.sum((blk_start[:, None] >= pad_end[None, :]).astype(jnp.int32), axis=1),
                             N_EXPERTS - 1).astype(jnp.int32)
    blk_rows = jnp.clip((pad_start + total)[blk_expert] - blk_start, 0, MOE_BLK)
    blk_rows = jnp.where(blk_start < pad_end[-1], blk_rows, 0).astype(jnp.int32)
    return (run_dst, run_off, run_len, zero_row), blk_expert, blk_rows


def kernel(x_prompt, x_sample, cache_attn_k, cache_attn_v, state_hgrn, c, c_ctx, ada_w, ada_b, norm1_g, norm2_g, final_g, ab_w_in, ab_w_out, na_rpb, sc_conv_w, cd_w_in, cd_w_out, hg_lb_logits, router_w, router_b, moe_w1, moe_b1, moe_w2, moe_b2):
    x_all = (x_prompt.reshape(N_PROMPT, D_MODEL), x_sample.reshape(N_SAMPLE, D_MODEL))
    m16 = jnp.zeros((N_MOD_ROWS, D_MODEL), F32).at[0].set(c_ctx).at[1:1 + DEC_BATCH].set(c)
    mod = _modulation(m16, ada_w, ada_b)

    lb_sm = jax.nn.softmax(hg_lb_logits.astype(F32), axis=0)
    lb_all = jnp.cumsum(lb_sm, axis=0) - lb_sm[0:1]

    rw_t = jnp.swapaxes(router_w.astype(F32), 1, 2)
    rw_hi = rw_t.astype(BF16)
    rw_lo = (rw_t - rw_hi.astype(F32)).astype(BF16)
    rw2 = jnp.stack([rw_hi, rw_lo], axis=1)
    rb_pad = jnp.broadcast_to(router_b.astype(F32)[:, :, None], (DEPTH, N_EXPERTS, LANES))

    new_k = new_v = new_s = None
    for l in range(DEPTH):
        j = l // 2
        if l % 2 == 0:
            p_all = _pre_project(x_all, mod[l], norm1_g[l], ab_w_in[j].astype(BF16))
            att_p = _context_attention(p_all)
            ck = cache_attn_k[:, j].transpose(0, 2, 1, 3).reshape(DEC_BATCH, PAST_LEN, NA_W)
            cv = cache_attn_v[:, j].transpose(0, 2, 1, 3).reshape(DEC_BATCH, PAST_LEN, NA_W)
            att_s = _neighbourhood_attention(p_all, ck, cv, _natten_bias(na_rpb[j]))
            kp = p_all[:N_PROMPT, NA_W:2 * NA_W].reshape(BATCH, SEQ, NA_HEADS, NA_HEAD_DIM).transpose(0, 2, 1, 3)
            vp = p_all[:N_PROMPT, 2 * NA_W:3 * NA_W].reshape(BATCH, SEQ, NA_HEADS, NA_HEAD_DIM).transpose(0, 2, 1, 3)
            new_k, new_v = kp[:, None], vp[:, None]
            outs = _post_mix((att_p, att_s), p_all, sc_conv_w[j], x_all, mod[l], norm2_g[l],
                             ab_w_out[j].astype(BF16), rw2[l], rb_pad[l], with_conv=True)
        else:
            p_all = _pre_project(x_all, mod[l], norm1_g[l], cd_w_in[j].astype(BF16))
            fn_p = _fourier_mix(p_all, 0, BATCH, SEQ)
            fn_s = _fourier_mix(p_all, N_PROMPT, DEC_BATCH, DEC_SEQ)
            s0_p = jnp.zeros((BATCH, 2, HG_HEADS, HG_DK, HG_DK), F32)
            s0_s = jnp.swapaxes(state_hgrn[:, j].astype(F32), -1, -2)
            hg_p, sfin_p = _hgrn2_bidir(p_all, lb_all[l], s0_p, 0, BATCH, SEQ)
            hg_s, _ = _hgrn2_bidir(p_all, lb_all[l], s0_s, N_PROMPT, DEC_BATCH, DEC_SEQ)
            new_s = jnp.swapaxes(sfin_p, -1, -2)[:, None].astype(x_prompt.dtype)
            outs = _post_mix((fn_p, fn_s), (hg_p, hg_s), None, x_all, mod[l], norm2_g[l],
                             cd_w_out[j].astype(BF16), rw2[l], rb_pad[l], with_conv=False)
        x_mid, h2, route_rows, route_tok, tile_cnt = outs
        tables, blk_expert, blk_rows = _moe_layout(tile_cnt)
        xs = _moe_dispatch(tables, h2, route_rows)
        ys = _moe_experts(l, blk_expert, blk_rows, xs, moe_w1, moe_b1[:, :, None, :], moe_w2, moe_b2[:, :, None, :])
        x_all = _moe_combine(tables, ys, route_tok, x_mid, mod[l], final_g, final=(l == DEPTH - 1))

    y_prompt, y_sample = x_all
    return (y_prompt.reshape(BATCH, SEQ, D_MODEL), y_sample.reshape(DEC_BATCH, DEC_SEQ, D_MODEL),
            new_k, new_v, new_s)
```

```python
import functools

import numpy as np
import jax
import jax.numpy as jnp
from jax import lax
from jax.experimental import pallas as pl
from jax.experimental.pallas import tpu as pltpu

F32 = jnp.float32
BF16 = jnp.bfloat16

D_MODEL = 1024
BATCH = 16
SEQ = 256
DEC_BATCH = 8
DEC_SEQ = 2048
PAST_LEN = 512
DEPTH = 2
GRID_W = 64
GRID_ROWS = DEC_SEQ // GRID_W
NA_HEADS = 8
NA_HEAD_DIM = 64
NA_W = NA_HEADS * NA_HEAD_DIM
SC_W = D_MODEL - NA_W
NA_SCALE = NA_HEAD_DIM ** -0.5
MAX_WR = 8
WIN_W = 16
FN_W = 512
FN_GROUPS = 4
FN_GROUP_W = 128
HG_W = 512
HG_HEADS = 4
HG_DK = 128
HG_CHUNK = 64
HG_SUB = 16
N_EXPERTS = 32
TOP_K = 4
D_FF = D_MODEL
SWIGLU_LIMIT = 7.0
SWIGLU_ALPHA = 1.702
EPS = 1e-6
NEG_INF = -1e30

N_PROMPT = BATCH * SEQ
N_SAMPLE = DEC_BATCH * DEC_SEQ
N_TOK = N_PROMPT + N_SAMPLE
TM = 256
N_TILES = N_TOK // TM
PROMPT_TILES = N_PROMPT // TM
TILES_PER_DEC_SEQ = DEC_SEQ // TM
N_MOD_ROWS = 16
LANES = 128
SUBLANES = 8

ROUTE_ROWS = 16
POST_TILES = 2
MOE_BLK = 512
N_ASG = N_TOK * TOP_K
N_BLK = N_ASG // MOE_BLK + N_EXPERTS
N_SLOT = N_BLK * MOE_BLK
VMEM_LIMIT = 56 * 1024 * 1024


def _mod_row(i):
    return jnp.where(i < PROMPT_TILES, 0, 1 + (i - PROMPT_TILES) // TILES_PER_DEC_SEQ)


def _sigmoid_pair(z):
    e = jnp.exp(-jnp.abs(z))
    r = 1.0 / (1.0 + e)
    er = e * r
    pos = z >= 0
    return jnp.where(pos, r, er), jnp.where(pos, er, r)


def _silu(x):
    return x * _sigmoid_pair(x)[0]


def _rms(x):
    return x * lax.rsqrt(jnp.mean(x * x, axis=-1, keepdims=True) + EPS)


def _dot(a, b):
    return jnp.dot(a, b, preferred_element_type=F32)


def _dot_nt(a, b):
    return lax.dot_general(a, b, (((1,), (1,)), ((), ())), preferred_element_type=F32)


def _dot_tn(a, b):
    return lax.dot_general(a, b, (((0,), (0,)), ((), ())), preferred_element_type=F32)


def _split3(x):
    hi = x.astype(BF16)
    r1 = x - hi.astype(F32)
    mid = r1.astype(BF16)
    lo = (r1 - mid.astype(F32)).astype(BF16)
    return hi, mid, lo


MOD_TN = 1024


def _mod_kernel(m_ref, w_ref, b_ref, o_ref):
    a = _silu(m_ref[...]).astype(BF16)
    o_ref[...] = _dot(a, w_ref[...].astype(BF16)) + b_ref[...]


def _modulation(m16, ada_w, ada_b):
    n_out = 6 * D_MODEL
    out = pl.pallas_call(
        _mod_kernel,
        out_shape=jax.ShapeDtypeStruct((DEPTH, N_MOD_ROWS, n_out), F32),
        grid=(DEPTH, n_out // MOD_TN),
        in_specs=[
            pl.BlockSpec((N_MOD_ROWS, D_MODEL), lambda l, j: (0, 0)),
            pl.BlockSpec((None, D_MODEL, MOD_TN), lambda l, j: (l, 0, j)),
            pl.BlockSpec((None, 1, MOD_TN), lambda l, j: (l, 0, j)),
        ],
        out_specs=pl.BlockSpec((None, N_MOD_ROWS, MOD_TN), lambda l, j: (l, 0, j)),
        compiler_params=pltpu.CompilerParams(
            dimension_semantics=("arbitrary", "arbitrary"), vmem_limit_bytes=VMEM_LIMIT),
    )(m16, ada_w, ada_b.reshape(DEPTH, 1, n_out))
    return out.reshape(DEPTH, N_MOD_ROWS, 6, D_MODEL)


def _by_group(i, refs):
    if len(refs) == 1:
        return refs[0][...]
    return jnp.where(i < PROMPT_TILES, refs[0][...], refs[1][...])


def _group_specs(width, tiles=1):
    n_p = PROMPT_TILES // tiles
    return [pl.BlockSpec((tiles * TM, width), lambda i: (jnp.minimum(i, n_p - 1), 0)),
            pl.BlockSpec((tiles * TM, width), lambda i: (jnp.maximum(i - n_p, 0), 0))]


def _pre_kernel(*refs):
    x_refs, (mod_ref, g_ref, w_ref, o_ref) = refs[:-4], refs[-4:]
    y = _rms(_by_group(pl.program_id(0), x_refs)) * g_ref[...]
    h = y * (1.0 + mod_ref[1:2, :]) + mod_ref[0:1, :]
    o_ref[...] = _dot(h.astype(BF16), w_ref[...])


def _pre_project(x_src, mod_l, g, w_bf):
    n_out = w_bf.shape[1]
    if isinstance(x_src, tuple):
        x_specs, x_args = _group_specs(D_MODEL), list(x_src)
    else:
        x_specs, x_args = [pl.BlockSpec((TM, D_MODEL), lambda i: (i, 0))], [x_src]
    return pl.pallas_call(
        _pre_kernel,
        out_shape=jax.ShapeDtypeStruct((N_TOK, n_out), F32),
        grid=(N_TILES,),
        in_specs=x_specs + [
            pl.BlockSpec((None, 6, D_MODEL), lambda i: (_mod_row(i), 0, 0)),
            pl.BlockSpec((1, D_MODEL), lambda i: (0, 0)),
            pl.BlockSpec((D_MODEL, n_out), lambda i: (0, 0)),
        ],
        out_specs=pl.BlockSpec((TM, n_out), lambda i: (i, 0)),
        compiler_params=pltpu.CompilerParams(
            dimension_semantics=("arbitrary",), vmem_limit_bytes=VMEM_LIMIT),
    )(*x_args, mod_l, g.reshape(1, D_MODEL), w_bf)


def _softmax_pv(parts):
    m = None
    for s, _ in parts:
        mi = jnp.max(s, axis=-1, keepdims=True)
        m = mi if m is None else jnp.maximum(m, mi)
    acc = None
    den = None
    for s, v in parts:
        p = jnp.exp(s - m)
        d = jnp.sum(p, axis=-1, keepdims=True)
        o = _dot(p.astype(BF16), v)
        acc = o if acc is None else acc + o
        den = d if den is None else den + d
    return acc / den


def _ctx_attn_kernel(q_ref, k_ref, v_ref, o_ref):
    q = q_ref[...]
    kb = k_ref[...].astype(BF16)
    vb = v_ref[...].astype(BF16)
    lane = lax.broadcasted_iota(jnp.int32, q.shape, 1)
    out = jnp.zeros(q.shape, F32)
    for h in range(2):
        hm = (lane >= NA_HEAD_DIM) if h == 1 else (lane < NA_HEAD_DIM)
        qm = jnp.where(hm, q, 0.0).astype(BF16)
        s = _dot_nt(qm, kb) * NA_SCALE
        out = jnp.where(hm, _softmax_pv([(s, vb)]), out)
    o_ref[...] = out


def _context_attention(p_all):
    n_hp = NA_HEADS // 2
    return pl.pallas_call(
        _ctx_attn_kernel,
        out_shape=jax.ShapeDtypeStruct((N_PROMPT, NA_W), F32),
        grid=(BATCH, n_hp),
        in_specs=[
            pl.BlockSpec((SEQ, LANES), lambda b, hp: (b, hp)),
            pl.BlockSpec((SEQ, LANES), lambda b, hp: (b, n_hp + hp)),
            pl.BlockSpec((SEQ, LANES), lambda b, hp: (b, 2 * n_hp + hp)),
        ],
        out_specs=pl.BlockSpec((SEQ, LANES), lambda b, hp: (b, hp)),
        compiler_params=pltpu.CompilerParams(
            dimension_semantics=("arbitrary", "arbitrary"), vmem_limit_bytes=VMEM_LIMIT),
    )(p_all, p_all, p_all)


N_BIAS_VAR = MAX_WR
WIN_KEYS = MAX_WR * GRID_W
NA_ROWS = 2


def _natten_kernel(q_ref, k_ref, v_ref, ck_ref, cv_ref, bias_ref, o_ref):
    ckb = ck_ref[...].astype(BF16)
    cvb = cv_ref[...].astype(BF16)
    lane = lax.broadcasted_iota(jnp.int32, (GRID_W, LANES), 1)

    first = lane < NA_HEAD_DIM
    blk = 2 * GRID_W

    def rows(it, carry):
        qs, kws, vws, biases = [], [], [], []
        for d in range(NA_ROWS):
            r = it * NA_ROWS + d
            r0 = jnp.clip(r - MAX_WR // 2, 0, GRID_ROWS - MAX_WR)
            q = q_ref[pl.ds(pl.multiple_of(r * GRID_W, GRID_W), GRID_W), :]
            qs += [jnp.where(first, q, 0.0), jnp.where(first, 0.0, q)]
            kws.append(k_ref[pl.ds(pl.multiple_of(r0 * GRID_W, GRID_W), WIN_KEYS), :].astype(BF16))
            vws.append(v_ref[pl.ds(pl.multiple_of(r0 * GRID_W, GRID_W), WIN_KEYS), :].astype(BF16))
            biases.append(bias_ref[r0 - r + (MAX_WR - 1)])
        q_all = jnp.concatenate(qs, axis=0).astype(BF16)
        s_ctx = _dot_nt(q_all, ckb) * NA_SCALE
        s_loc = jnp.concatenate([_dot_nt(q_all[d * blk:(d + 1) * blk], kws[d]) * NA_SCALE + biases[d]
                                 for d in range(NA_ROWS)], axis=0)
        m = jnp.maximum(jnp.max(s_loc, axis=-1, keepdims=True), jnp.max(s_ctx, axis=-1, keepdims=True))
        p_loc = jnp.exp(s_loc - m)
        p_ctx = jnp.exp(s_ctx - m)
        den = jnp.sum(p_loc, axis=-1, keepdims=True) + jnp.sum(p_ctx, axis=-1, keepdims=True)
        p_loc = p_loc.astype(BF16)
        o_loc = jnp.concatenate([_dot(p_loc[d * blk:(d + 1) * blk], vws[d]) for d in range(NA_ROWS)], axis=0)
        o = (o_loc + _dot(p_ctx.astype(BF16), cvb)) / den
        for d in range(NA_ROWS):
            r = it * NA_ROWS + d
            o_ref[pl.ds(pl.multiple_of(r * GRID_W, GRID_W), GRID_W), :] = jnp.where(
                first, o[d * blk:d * blk + GRID_W], o[d * blk + GRID_W:(d + 1) * blk])
        return carry

    lax.fori_loop(0, GRID_ROWS // NA_ROWS, rows, 0, unroll=2)


def _natten_bias(rpb):
    qc = np.arange(GRID_W)[:, None]
    kc = np.arange(GRID_W)[None, :]
    ws = np.clip(qc - WIN_W // 2, 0, GRID_W - WIN_W)
    valid = (kc >= ws) & (kc < ws + WIN_W)
    dc = np.clip(kc - qc + WIN_W - 1, 0, 2 * WIN_W - 2)
    t = rpb.astype(F32)[:, :, dc]
    t = jnp.where(jnp.asarray(valid)[None, None], t, NEG_INF)
    dr = np.arange(N_BIAS_VAR)[:, None] + np.arange(MAX_WR)[None, :]
    slab = t[:, dr]
    slab = slab.transpose(0, 1, 3, 2, 4).reshape(NA_HEADS // 2, 2, N_BIAS_VAR, GRID_W, WIN_KEYS)
    return slab.transpose(0, 2, 1, 3, 4).reshape(NA_HEADS // 2, N_BIAS_VAR, 2 * GRID_W, WIN_KEYS)


def _neighbourhood_attention(p_all, ck, cv, bias):
    n_hp = NA_HEADS // 2
    off = N_PROMPT // DEC_SEQ
    return pl.pallas_call(
        _natten_kernel,
        out_shape=jax.ShapeDtypeStruct((N_SAMPLE, NA_W), F32),
        grid=(DEC_BATCH, n_hp),
        in_specs=[
            pl.BlockSpec((DEC_SEQ, LANES), lambda b, hp: (off + b, hp)),
            pl.BlockSpec((DEC_SEQ, LANES), lambda b, hp: (off + b, n_hp + hp)),
            pl.BlockSpec((DEC_SEQ, LANES), lambda b, hp: (off + b, 2 * n_hp + hp)),
            pl.BlockSpec((None, PAST_LEN, LANES), lambda b, hp: (b, 0, hp)),
            pl.BlockSpec((None, PAST_LEN, LANES), lambda b, hp: (b, 0, hp)),
            pl.BlockSpec((None, N_BIAS_VAR, 2 * GRID_W, WIN_KEYS), lambda b, hp: (hp, 0, 0, 0)),
        ],
        out_specs=pl.BlockSpec((DEC_SEQ, LANES), lambda b, hp: (b, hp)),
        compiler_params=pltpu.CompilerParams(
            dimension_semantics=("arbitrary", "arbitrary"), vmem_limit_bytes=VMEM_LIMIT),
    )(p_all, p_all, p_all, ck, cv, bias)


def _dft_tables(t):
    j = np.arange(t, dtype=np.int64)
    ang = 2.0 * np.pi * ((j[:, None] * j[None, :]) % t).astype(np.float64) / t
    return np.cos(ang), np.sin(ang)


def _fourier_kernel(z_ref, lt_ref, rc_ref, rs_ref, o_ref, ab_ref, *, t_len, scale):
    @pl.when(pl.program_id(1) == 0)
    def _():
        z = z_ref[...].astype(BF16)
        for g in range(FN_GROUPS):
            zg = z[:, g * FN_GROUP_W:(g + 1) * FN_GROUP_W]
            ab_ref[0:t_len, g * FN_GROUP_W:(g + 1) * FN_GROUP_W] = _dot(zg, rc_ref[...]).astype(BF16)
            ab_ref[t_len:2 * t_len, g * FN_GROUP_W:(g + 1) * FN_GROUP_W] = _dot(zg, rs_ref[...]).astype(BF16)

    o_ref[...] = _dot(lt_ref[...], ab_ref[...]) * scale


def _fourier_mix(p_all, row_off, n_batch, t_len):
    ct, st = _dft_tables(t_len)
    lt = jnp.asarray(np.concatenate([ct, -st], axis=1), dtype=F32).astype(BF16)
    cc, sc = _dft_tables(FN_GROUP_W)
    rc = jnp.asarray(cc, dtype=F32).astype(BF16)
    rs = jnp.asarray(sc, dtype=F32).astype(BF16)
    tr = min(t_len, 512)
    n_rt = t_len // tr
    off = row_off // t_len
    kern = functools.partial(_fourier_kernel, t_len=t_len, scale=float((t_len * FN_GROUP_W) ** -0.5))
    return pl.pallas_call(
        kern,
        out_shape=jax.ShapeDtypeStruct((n_batch * t_len, FN_W), F32),
        grid=(n_batch, n_rt),
        in_specs=[
            pl.BlockSpec((t_len, FN_W), lambda b, i: (off + b, 0)),
            pl.BlockSpec((tr, 2 * t_len), lambda b, i: (i, 0)),
            pl.BlockSpec((FN_GROUP_W, FN_GROUP_W), lambda b, i: (0, 0)),
            pl.BlockSpec((FN_GROUP_W, FN_GROUP_W), lambda b, i: (0, 0)),
        ],
        out_specs=pl.BlockSpec((tr, FN_W), lambda b, i: (b * n_rt + i, 0)),
        scratch_shapes=[pltpu.VMEM((2 * t_len, FN_W), BF16)],
        compiler_params=pltpu.CompilerParams(
            dimension_semantics=("arbitrary", "arbitrary"), vmem_limit_bytes=VMEM_LIMIT),
    )(p_all, lt, rc, rs)


N_SUB = HG_CHUNK // HG_SUB
LOG2_E = 1.4426950408889634


def _gla_chunk(q, v, z, lb, st, backward):
    c = HG_CHUNK
    sg, sgn = _sigmoid_pair(z)
    lf = jnp.log(lb + (1.0 - lb) * sg)
    kk = (1.0 - lb) * sgn
    row = lax.broadcasted_iota(jnp.int32, (c, c), 0)
    col = lax.broadcasted_iota(jnp.int32, (c, c), 1)
    tri = ((col >= row) if backward else (col <= row)).astype(BF16)
    hi, mid, lo = _split3(lf)
    b = _dot(tri, hi) + _dot(tri, mid) + _dot(tri, lo)
    o_inter = _dot_nt((q * jnp.exp(b)).astype(BF16), st.astype(BF16))
    b2 = b * LOG2_E
    c2 = b2 - jnp.log2(kk)

    sub_row = lax.broadcasted_iota(jnp.int32, (HG_SUB, c), 0)
    sub_col = lax.broadcasted_iota(jnp.int32, (HG_SUB, c), 1)
    att_rows = []
    for i in range(N_SUB):
        lo_r, hi_r = i * HG_SUB, (i + 1) * HG_SUB
        q_i = q[lo_r:hi_r]
        b_i = b[lo_r:hi_r]
        has_off = (i < N_SUB - 1) if backward else (i > 0)
        if has_off:
            r_i = b[hi_r:hi_r + 1] if backward else b[lo_r - 1:lo_r]
            qs = (q_i * jnp.exp(b_i - r_i)).astype(BF16)
            ks = (kk * jnp.exp(jnp.minimum(r_i - b, 0.0))).astype(BF16)
            a = _dot_nt(qs, ks)
            off_mask = (sub_col >= hi_r) if backward else (sub_col < lo_r)
            a = jnp.where(off_mask, a, 0.0)
        else:
            a = jnp.zeros((HG_SUB, c), F32)
        b2_i = b2[lo_r:hi_r]
        c2_i = c2[lo_r:hi_r]
        for s in range(HG_SUB):
            colsum = jnp.sum(q_i * jnp.exp2(b2_i - c2_i[s:s + 1]), axis=-1, keepdims=True)
            keep = (sub_row <= s) if backward else (sub_row >= s)
            a = jnp.where((sub_col == lo_r + s) & keep, colsum, a)
        att_rows.append(a)
    att = jnp.concatenate(att_rows, axis=0)
    o = o_inter + _dot(att.astype(BF16), v.astype(BF16))

    b_last = b[0:1] if backward else b[c - 1:c]
    kd = (kk * jnp.exp(b_last - b)).astype(BF16)
    st_new = st * jnp.exp(b_last) + _dot_tn(v.astype(BF16), kd)
    return o, st_new


def _hgrn_kernel(q_ref, i_ref, zf_ref, zb_ref, g_ref, lb_ref, s0_ref, o_ref, sfin_ref, of_ref, ob_ref,
                 stf_ref, stb_ref, *, t_len):
    n_chunk = t_len // HG_CHUNK

    def sl(c):
        return pl.ds(pl.multiple_of(c * HG_CHUNK, HG_CHUNK), HG_CHUNK)

    stf_ref[...] = s0_ref[0]
    stb_ref[...] = s0_ref[1]

    def step(j, carry):
        c = n_chunk - 1 - j
        o_f, st_f = _gla_chunk(q_ref[sl(j), :], i_ref[sl(j), :], zf_ref[sl(j), :], lb_ref[0:1, :],
                               stf_ref[...], backward=False)
        o_b, st_b = _gla_chunk(q_ref[sl(c), :], i_ref[sl(c), :], zb_ref[sl(c), :], lb_ref[1:2, :],
                               stb_ref[...], backward=True)
        of_ref[sl(j), :] = o_f
        ob_ref[sl(c), :] = o_b
        stf_ref[...] = st_f
        stb_ref[...] = st_b
        return carry

    lax.fori_loop(0, n_chunk, step, 0, unroll=4)
    sfin_ref[0] = stf_ref[...]
    sfin_ref[1] = stb_ref[...]

    def finish(c, carry):
        tot = of_ref[sl(c), :] + ob_ref[sl(c), :]
        o_ref[sl(c), :] = _rms(tot) * _silu(g_ref[sl(c), :])
        return carry

    lax.fori_loop(0, n_chunk, finish, 0)


def _hgrn2_bidir(p_all, lb, s0_t, row_off, n_batch, t_len):
    off = row_off // t_len
    cb = FN_W // LANES

    def col(j):
        return pl.BlockSpec((t_len, LANES), lambda b, h: (off + b, cb + j * HG_HEADS + h))

    kern = functools.partial(_hgrn_kernel, t_len=t_len)
    state_spec = pl.BlockSpec((None, 2, None, HG_DK, HG_DK), lambda b, h: (b, 0, h, 0, 0))
    return pl.pallas_call(
        kern,
        out_shape=(jax.ShapeDtypeStruct((n_batch * t_len, HG_W), F32),
                   jax.ShapeDtypeStruct((n_batch, 2, HG_HEADS, HG_DK, HG_DK), F32)),
        grid=(n_batch, HG_HEADS),
        in_specs=[col(0), col(1), col(2), col(3), col(4),
                  pl.BlockSpec((2, LANES), lambda b, h: (0, h)),
                  state_spec],
        out_specs=(pl.BlockSpec((t_len, LANES), lambda b, h: (b, h)), state_spec),
        scratch_shapes=[pltpu.VMEM((t_len, LANES), F32), pltpu.VMEM((t_len, LANES), F32),
                        pltpu.VMEM((HG_DK, HG_DK), F32), pltpu.VMEM((HG_DK, HG_DK), F32)],
        compiler_params=pltpu.CompilerParams(
            dimension_semantics=("arbitrary", "arbitrary"), vmem_limit_bytes=VMEM_LIMIT),
    )(p_all, p_all, p_all, p_all, p_all, lb, s0_t)


def _route(h2, rw_ref, rb_ref):
    hi, mid, _ = _split3(h2)
    w_hi = rw_ref[0]
    w_lo = rw_ref[1]
    logits = _dot_nt(w_hi, hi) + _dot_nt(w_hi, mid) + _dot_nt(w_lo, hi) + rb_ref[:, 0:1]
    width = h2.shape[0]
    sub = lax.broadcasted_iota(jnp.int32, (N_EXPERTS, width), 0)
    work = logits
    vals, hots = [], []
    for k in range(TOP_K):
        m = jnp.max(work, axis=0, keepdims=True)
        idx = jnp.min(jnp.where(work == m, sub, N_EXPERTS), axis=0, keepdims=True)
        hot = sub == idx
        work = jnp.where(hot, -jnp.inf, work)
        vals.append(m)
        hots.append(hot)
    es = [jnp.exp(v - vals[0]) for v in vals]
    den = es[0] + es[1] + es[2] + es[3]
    cnt = jnp.zeros((N_EXPERTS, width), F32)
    for hot in hots:
        cnt = cnt + hot.astype(F32)
    row = lax.broadcasted_iota(jnp.int32, (TM, TM), 0)
    col = lax.broadcasted_iota(jnp.int32, (TM, TM), 1)
    earlier = (row < col).astype(BF16)
    erow = lax.broadcasted_iota(jnp.int32, (N_EXPERTS, N_EXPERTS), 0)
    ecol = lax.broadcasted_iota(jnp.int32, (N_EXPERTS, N_EXPERTS), 1)
    lower = (ecol < erow).astype(BF16)
    where_to, tile_cnts = [], []
    for t in range(width // TM):
        cnt_t = cnt[:, t * TM:(t + 1) * TM]
        before = _dot(cnt_t.astype(BF16), earlier)
        tile_cnt = jnp.sum(cnt_t, axis=1, keepdims=True)
        run_off = _dot(lower, jnp.broadcast_to(tile_cnt, (N_EXPERTS, LANES)).astype(BF16))[:, 0:1]
        where_to.append(before + run_off)
        tile_cnts.append(tile_cnt)
    where_to = where_to[0] if len(where_to) == 1 else jnp.concatenate(where_to, axis=1)
    rsub = lax.broadcasted_iota(jnp.int32, (ROUTE_ROWS, width), 0)
    rows = jnp.zeros((ROUTE_ROWS, width), F32)
    for k in range(TOP_K):
        pk = jnp.sum(jnp.where(hots[k], where_to, 0.0), axis=0, keepdims=True)
        rows = jnp.where(rsub == k, pk, rows)
        rows = jnp.where(rsub == TOP_K + k, es[k] / den, rows)
    return rows, tile_cnts


def _post_kernel(*refs, with_conv, n_x):
    refs = list(refs)
    a_refs, refs = refs[:2], refs[2:]
    if with_conv:
        (c_ref, cprev_ref, cnext_ref, cw_ref), refs = refs[:4], refs[4:]
    else:
        b_refs, refs = refs[:2], refs[2:]
    x_refs, refs = refs[:n_x], refs[n_x:]
    mod_ref, g_ref, wo_ref, rw_ref, rb_ref, xo_ref, h2_ref, rows_ref, pg_ref, cnt_ref = refs
    i = pl.program_id(0)
    is_prompt = i < PROMPT_TILES // POST_TILES

    def grp(refs_, rows):
        if len(refs_) == 1:
            return refs_[0][rows, :]
        return jnp.where(is_prompt, refs_[0][rows, :], refs_[1][rows, :])

    if with_conv:
        zs = [c_ref[t * TM:(t + 1) * TM, SC_W:2 * SC_W] * c_ref[t * TM:(t + 1) * TM, 2 * SC_W:3 * SC_W]
              for t in range(POST_TILES)]
        z_before = cprev_ref[SUBLANES - 1:SUBLANES, SC_W:2 * SC_W] * cprev_ref[SUBLANES - 1:SUBLANES, 2 * SC_W:3 * SC_W]
        z_after = cnext_ref[0:1, SC_W:2 * SC_W] * cnext_ref[0:1, 2 * SC_W:3 * SC_W]
    pick = (lax.broadcasted_iota(jnp.int32, (ROUTE_ROWS, LANES), 0)
            == lax.broadcasted_iota(jnp.int32, (ROUTE_ROWS, LANES), 1)).astype(BF16)

    everything = slice(None)
    if with_conv:
        parts = []
        for t in range(POST_TILES):
            j = i * POST_TILES + t - PROMPT_TILES
            seq_start = is_prompt | (j % TILES_PER_DEC_SEQ == 0)
            seq_end = is_prompt | (j % TILES_PER_DEC_SEQ == TILES_PER_DEC_SEQ - 1)
            z = zs[t]
            z_prev = jnp.where(seq_start, 0.0, z_before if t == 0 else zs[t - 1][TM - 1:TM])
            z_next = jnp.where(seq_end, 0.0, z_after if t == POST_TILES - 1 else zs[t + 1][0:1])
            row = lax.broadcasted_iota(jnp.int32, (TM, SC_W), 0)
            zm1 = jnp.where(row == 0, z_prev, pltpu.roll(z, 1, axis=0))
            zp1 = jnp.where(row == TM - 1, z_next, pltpu.roll(z, TM - 1, axis=0))
            parts.append(c_ref[t * TM:(t + 1) * TM, 0:SC_W]
                         * (zm1 * cw_ref[0:1, :] + z * cw_ref[1:2, :] + zp1 * cw_ref[2:3, :]))
        b_val = jnp.concatenate(parts, axis=0)
    else:
        b_val = grp(b_refs, everything)
    a_val = grp(a_refs, everything)
    half = a_val.shape[1]
    mix = _dot(a_val.astype(BF16), wo_ref[0:half, :]) + _dot(b_val.astype(BF16), wo_ref[half:, :])
    x_new = grp(x_refs, everything) + mod_ref[2:3, :] * mix
    xo_ref[...] = x_new
    h2 = _rms(x_new) * g_ref[...] * (1.0 + mod_ref[4:5, :]) + mod_ref[3:4, :]
    h2_ref[...] = h2
    route, tile_cnts = _route(h2, rw_ref, rb_ref)
    r_hi, r_mid, r_lo = _split3(route)
    pg_ref[...] = _dot_tn(r_hi, pick) + _dot_tn(r_mid, pick) + _dot_tn(r_lo, pick)
    for t in range(POST_TILES):
        rows_ref[t * ROUTE_ROWS:(t + 1) * ROUTE_ROWS, :] = route[:, t * TM:(t + 1) * TM]
        cnt_ref[t] = jnp.broadcast_to(tile_cnts[t], (N_EXPERTS, LANES)).astype(jnp.int32)


def _post_mix(a_pair, b_src, conv_w, x_src, mod_l, g2, wo_bf, rw2, rb, with_conv):
    step_rows = POST_TILES * TM
    tile = lambda w: pl.BlockSpec((step_rows, w), lambda i: (i, 0))
    full = lambda shp: pl.BlockSpec(shp, lambda i: tuple(0 for _ in shp))
    rows8 = step_rows // SUBLANES
    n8 = N_TOK // SUBLANES
    in_specs = _group_specs(a_pair[0].shape[1], POST_TILES)
    args = list(a_pair)
    if with_conv:
        cw = 3 * SC_W
        in_specs += [
            pl.BlockSpec((step_rows, cw), lambda i: (i, 1)),
            pl.BlockSpec((SUBLANES, cw), lambda i: (jnp.maximum(i * rows8 - 1, 0), 1)),
            pl.BlockSpec((SUBLANES, cw), lambda i: (jnp.minimum((i + 1) * rows8, n8 - 1), 1)),
            full((3, SC_W)),
        ]
        args += [b_src, b_src, b_src, conv_w]
    else:
        in_specs += _group_specs(b_src[0].shape[1], POST_TILES)
        args += list(b_src)
    if isinstance(x_src, tuple):
        in_specs += _group_specs(D_MODEL, POST_TILES)
        args += list(x_src)
        n_x = 2
    else:
        in_specs += [tile(D_MODEL)]
        args += [x_src]
        n_x = 1
    in_specs += [
        pl.BlockSpec((None, 6, D_MODEL), lambda i: (_mod_row(i * POST_TILES), 0, 0)),
        full((1, D_MODEL)),
        full((D_MODEL, D_MODEL)),
        full((2, N_EXPERTS, D_MODEL)),
        full((N_EXPERTS, LANES)),
    ]
    args += [mod_l, g2.reshape(1, D_MODEL), wo_bf, rw2, rb]
    out_shape = (
        jax.ShapeDtypeStruct((N_TOK, D_MODEL), F32),
        jax.ShapeDtypeStruct((N_TOK, D_MODEL), F32),
        jax.ShapeDtypeStruct((N_TILES * ROUTE_ROWS, TM), F32),
        jax.ShapeDtypeStruct((N_TOK, LANES), F32),
        jax.ShapeDtypeStruct((N_TILES, N_EXPERTS, LANES), jnp.int32),
    )
    out_specs = (tile(D_MODEL), tile(D_MODEL),
                 pl.BlockSpec((POST_TILES * ROUTE_ROWS, TM), lambda i: (i, 0)), tile(LANES),
                 pl.BlockSpec((POST_TILES, N_EXPERTS, LANES), lambda i: (i, 0, 0)))
    return pl.pallas_call(
        functools.partial(_post_kernel, with_conv=with_conv, n_x=n_x),
        out_shape=out_shape,
        grid=(N_TILES // POST_TILES,),
        in_specs=in_specs,
        out_specs=out_specs,
        compiler_params=pltpu.CompilerParams(
            dimension_semantics=("arbitrary",), vmem_limit_bytes=VMEM_LIMIT),
    )(*args)


ROW_SUB = D_MODEL // LANES
TILE_ASG = TM * TOP_K
RUN_BITS = TM.bit_length()
N_TAIL_BLOCKS = N_BLK - N_ASG // MOE_BLK
N_ZERO_BLOCKS = N_EXPERTS + N_TAIL_BLOCKS


def _rows_to_wide(ref, n_rows, lead=None):
    parts = []
    for j in range(ROW_SUB):
        idx = (pl.ds(j, n_rows, stride=ROW_SUB), slice(None))
        parts.append(ref[idx] if lead is None else ref[(lead,) + idx])
    return jnp.concatenate(parts, axis=1)


def _wide_to_rows(ref, val, lead=None):
    n_rows = val.shape[0]
    for j in range(ROW_SUB):
        idx = (pl.ds(j, n_rows, stride=ROW_SUB), slice(None))
        if lead is not None:
            idx = (lead,) + idx
        ref[idx] = val[:, j * LANES:(j + 1) * LANES]


def _row_window(ref, start_row, n_rows, lead=None):
    idx = (pl.ds(pl.multiple_of(start_row * ROW_SUB, ROW_SUB), n_rows * ROW_SUB), slice(None))
    return ref.at[idx] if lead is None else ref.at[(lead,) + idx]


def _run_copies(tile, rdst_ref, roff_ref, rlen_ref, make_copy):
    def per_expert(e, carry):
        r = tile * N_EXPERTS + e
        n = rlen_ref[r]
        src = roff_ref[r]
        dst = rdst_ref[r]
        for bit in reversed(range(RUN_BITS)):
            @pl.when((n & (1 << bit)) != 0)
            def _():
                done = n & ~((2 << bit) - 1)
                make_copy(src + done, dst + done, 1 << bit).start()
        return carry

    lax.fori_loop(0, N_EXPERTS, per_expert, 0)


def _dispatch_kernel(rdst_ref, roff_ref, rlen_ref, zrow_ref, h2_ref, rows_ref, xs_hbm,
                     stage, zbuf, sem, zsem):
    i = pl.program_id(0)
    n = pl.num_programs(0)
    slot = i % 2

    def zero_copy(e):
        return pltpu.make_async_copy(zbuf, _row_window(xs_hbm, zrow_ref[e], MOE_BLK), zsem)

    @pl.when(i == 0)
    def _():
        zbuf[...] = jnp.zeros_like(zbuf)

        def start(e, carry):
            @pl.when(zrow_ref[e] >= 0)
            def _():
                zero_copy(e).start()
            return carry

        def wait(e, carry):
            @pl.when(zrow_ref[e] >= 0)
            def _():
                zero_copy(e).wait()
            return carry

        lax.fori_loop(0, N_ZERO_BLOCKS, start, 0)
        lax.fori_loop(0, N_ZERO_BLOCKS, wait, 0)

    def wait_tile(slot_):
        pltpu.make_async_copy(stage.at[slot_], _row_window(xs_hbm, 0, TILE_ASG), sem.at[slot_]).wait()

    @pl.when(i >= 2)
    def _():
        wait_tile(slot)

    pos = rows_ref[0:TOP_K, :].astype(jnp.int32)
    sub = lax.broadcasted_iota(jnp.int32, (TILE_ASG, TM), 0)
    hit = sub == pos[0:1, :]
    for k in range(1, TOP_K):
        hit = hit | (sub == pos[k:k + 1, :])
    sorted_rows = _dot(hit.astype(BF16), h2_ref[...].astype(BF16))
    _wide_to_rows(stage, sorted_rows, lead=slot)

    _run_copies(i, rdst_ref, roff_ref, rlen_ref,
                lambda s, d, nr: pltpu.make_async_copy(_row_window(stage, s, nr, lead=slot),
                                                       _row_window(xs_hbm, d, nr), sem.at[slot]))

    @pl.when(i == n - 1)
    def _():
        wait_tile(slot)
        wait_tile(1 - slot)


def _moe_dispatch(tables, h2, route_rows):
    rdst, roff, rlen, zrow = tables
    grid_spec = pltpu.PrefetchScalarGridSpec(
        num_scalar_prefetch=4,
        grid=(N_TILES,),
        in_specs=[
            pl.BlockSpec((TM, D_MODEL), lambda i, *_: (i, 0)),
            pl.BlockSpec((ROUTE_ROWS, TM), lambda i, *_: (i, 0)),
        ],
        out_specs=pl.BlockSpec(memory_space=pl.ANY),
        scratch_shapes=[
            pltpu.VMEM((2, TILE_ASG * ROW_SUB, LANES), F32),
            pltpu.VMEM((MOE_BLK * ROW_SUB, LANES), F32),
            pltpu.SemaphoreType.DMA((2,)),
            pltpu.SemaphoreType.DMA(()),
        ],
    )
    return pl.pallas_call(
        _dispatch_kernel,
        out_shape=jax.ShapeDtypeStruct((N_SLOT * ROW_SUB, LANES), F32),
        grid_spec=grid_spec,
        compiler_params=pltpu.CompilerParams(
            dimension_semantics=("arbitrary",), vmem_limit_bytes=VMEM_LIMIT, has_side_effects=True),
    )(rdst, roff, rlen, zrow, h2, route_rows)


def _ffn_kernel(bexp_ref, brows_ref, xs_ref, w1_ref, b1_ref, w2_ref, b2_ref, ys_ref, w1b, w2b):
    b = pl.program_id(0)
    rows = brows_ref[b]
    half = MOE_BLK // 2

    def mlp(n_rows):
        x = _rows_to_wide(xs_ref, n_rows).astype(BF16)
        u = _dot(x, w1b[...]) + b1_ref[...]
        glu = jnp.minimum(u[:, :D_FF], SWIGLU_LIMIT)
        lin = jnp.clip(u[:, D_FF:], -SWIGLU_LIMIT, SWIGLU_LIMIT)
        act = glu * _sigmoid_pair(SWIGLU_ALPHA * glu)[0] * (lin + 1.0)
        _wide_to_rows(ys_ref, _dot(act.astype(BF16), w2b[...]) + b2_ref[...])

    def zero_from(first_row):
        n = (MOE_BLK - first_row) * ROW_SUB
        ys_ref[pl.ds(first_row * ROW_SUB, n), :] = jnp.zeros((n, LANES), F32)

    @pl.when(rows > 0)
    def _():
        prev = jnp.maximum(b - 1, 0)

        @pl.when((b == 0) | (bexp_ref[b] != bexp_ref[prev]))
        def _():
            w1b[...] = w1_ref[...].astype(BF16)
            w2b[...] = w2_ref[...].astype(BF16)

    @pl.when(rows > half)
    def _():
        mlp(MOE_BLK)

    @pl.when((rows > 0) & (rows <= half))
    def _():
        mlp(half)
        zero_from(half)

    @pl.when(rows == 0)
    def _():
        zero_from(0)


def _moe_experts(layer, blk_expert, blk_rows, xs, w1, b1, w2, b2):
    def blk(b, be, br):
        return (b, 0)

    def wsel(b, be, br):
        return (layer, be[b], 0, 0)

    grid_spec = pltpu.PrefetchScalarGridSpec(
        num_scalar_prefetch=2,
        grid=(N_BLK,),
        in_specs=[
            pl.BlockSpec((MOE_BLK * ROW_SUB, LANES), blk),
            pl.BlockSpec((None, None, D_MODEL, 2 * D_FF), wsel),
            pl.BlockSpec((None, None, 1, 2 * D_FF), wsel),
            pl.BlockSpec((None, None, D_FF, D_MODEL), wsel),
            pl.BlockSpec((None, None, 1, D_MODEL), wsel),
        ],
        out_specs=pl.BlockSpec((MOE_BLK * ROW_SUB, LANES), blk),
        scratch_shapes=[
            pltpu.VMEM((D_MODEL, 2 * D_FF), BF16),
            pltpu.VMEM((D_FF, D_MODEL), BF16),
        ],
    )
    return pl.pallas_call(
        _ffn_kernel,
        out_shape=jax.ShapeDtypeStruct((N_SLOT * ROW_SUB, LANES), F32),
        grid_spec=grid_spec,
        compiler_params=pltpu.CompilerParams(
            dimension_semantics=("arbitrary",), vmem_limit_bytes=VMEM_LIMIT),
    )(blk_expert, blk_rows, xs, w1, b1, w2, b2)


def _combine_kernel(rdst_ref, roff_ref, rlen_ref, ys_hbm, pg_ref, x_ref, mod_ref, g_ref, *rest,
                    final):
    if final:
        yp_ref, ysm_ref, stage, sem = rest
    else:
        o_ref, stage, sem = rest
    i = pl.program_id(0)
    n = pl.num_programs(0)
    slot = i % 2

    def fetch(tile, slot_):
        _run_copies(tile, rdst_ref, roff_ref, rlen_ref,
                    lambda s, d, nr: pltpu.make_async_copy(_row_window(ys_hbm, d, nr),
                                                           _row_window(stage, s, nr, lead=slot_), sem.at[slot_]))

    @pl.when(i == 0)
    def _():
        fetch(0, 0)

    pltpu.make_async_copy(_row_window(ys_hbm, 0, TILE_ASG), stage.at[slot], sem.at[slot]).wait()

    @pl.when(i + 1 < n)
    def _():
        fetch(i + 1, 1 - slot)

    pg = pg_ref[...]
    pos = pg[:, 0:TOP_K].astype(jnp.int32)
    gates = pg[:, TOP_K:2 * TOP_K]
    lane = lax.broadcasted_iota(jnp.int32, (TM, TILE_ASG), 1)
    gmat = jnp.zeros((TM, TILE_ASG), F32)
    for k in range(TOP_K):
        gmat = jnp.where(lane == pos[:, k:k + 1], gates[:, k:k + 1], gmat)
    g_hi = gmat.astype(BF16)
    g_lo = (gmat - g_hi.astype(F32)).astype(BF16)
    cols = []
    for jj in range(ROW_SUB // 2):
        s = jnp.concatenate([stage[slot, pl.ds(2 * jj, TILE_ASG, stride=ROW_SUB), :],
                             stage[slot, pl.ds(2 * jj + 1, TILE_ASG, stride=ROW_SUB), :]], axis=1)
        s_bf = s.astype(BF16)
        cols.append(_dot(g_hi, s_bf) + _dot(g_lo, s_bf))
    moe = jnp.concatenate(cols, axis=1)
    x = x_ref[...] + mod_ref[5:6, :] * moe
    if final:
        x = _rms(x) * g_ref[...]

        @pl.when(i < PROMPT_TILES)
        def _():
            yp_ref[...] = x

        @pl.when(i >= PROMPT_TILES)
        def _():
            ysm_ref[...] = x
    else:
        o_ref[...] = x


def _moe_combine(tables, ys, route_tok, x_all, mod_l, final_g, final):
    rdst, roff, rlen, _ = tables
    tile = pl.BlockSpec((TM, D_MODEL), lambda i, *_: (i, 0))
    if final:
        out_shape = (jax.ShapeDtypeStruct((N_PROMPT, D_MODEL), F32), jax.ShapeDtypeStruct((N_SAMPLE, D_MODEL), F32))
        out_specs = (pl.BlockSpec((TM, D_MODEL), lambda i, *_: (jnp.minimum(i, PROMPT_TILES - 1), 0)),
                     pl.BlockSpec((TM, D_MODEL), lambda i, *_: (jnp.maximum(i - PROMPT_TILES, 0), 0)))
    else:
        out_shape = jax.ShapeDtypeStruct((N_TOK, D_MODEL), F32)
        out_specs = tile
    grid_spec = pltpu.PrefetchScalarGridSpec(
        num_scalar_prefetch=3,
        grid=(N_TILES,),
        in_specs=[
            pl.BlockSpec(memory_space=pl.ANY),
            pl.BlockSpec((TM, LANES), lambda i, *_: (i, 0)),
            tile,
            pl.BlockSpec((None, 6, D_MODEL), lambda i, *_: (_mod_row(i), 0, 0)),
            pl.BlockSpec((1, D_MODEL), lambda i, *_: (0, 0)),
        ],
        out_specs=out_specs,
        scratch_shapes=[
            pltpu.VMEM((2, TILE_ASG * ROW_SUB, LANES), F32),
            pltpu.SemaphoreType.DMA((2,)),
        ],
    )
    return pl.pallas_call(
        functools.partial(_combine_kernel, final=final),
        out_shape=out_shape,
        grid_spec=grid_spec,
        compiler_params=pltpu.CompilerParams(
            dimension_semantics=("arbitrary",), vmem_limit_bytes=VMEM_LIMIT),
    )(rdst, roff, rlen, ys, route_tok, x_all, mod_l, final_g.reshape(1, D_MODEL))


def _moe_layout(tile_cnt):
    cnt = tile_cnt[:, :, 0]
    total = jnp.sum(cnt, axis=0)
    padded = (total + MOE_BLK - 1) // MOE_BLK * MOE_BLK
    pad_end = jnp.cumsum(padded)
    pad_start = pad_end - padded
    before = jnp.cumsum(cnt, axis=0) - cnt
    run_dst = (pad_start[None, :] + before).reshape(-1).astype(jnp.int32)
    run_off = (jnp.cumsum(cnt, axis=1) - cnt).reshape(-1).astype(jnp.int32)
    run_len = cnt.reshape(-1).astype(jnp.int32)
    last_blk = jnp.where((padded > 0) & (total < padded), pad_end - MOE_BLK, -1)
    tail = pad_end[-1] + jnp.arange(N_TAIL_BLOCKS) * MOE_BLK
    zero_row = jnp.concatenate([last_blk, jnp.where(tail < N_SLOT, tail, -1)]).astype(jnp.int32)
    blk_start = jnp.arange(N_BLK, dtype=jnp.int32) * MOE_BLK
    blk_expert = jnp.minimum(jnp.sum((blk_start[:, None] >= pad_end[None, :]).astype(jnp.int32), axis=1),
                             N_EXPERTS - 1).astype(jnp.int32)
    blk_rows = jnp.clip((pad_start + total)[blk_expert] - blk_start, 0, MOE_BLK)
    blk_rows = jnp.where(blk_start < pad_end[-1], blk_rows, 0).astype(jnp.int32)
    return (run_dst, run_off, run_len, zero_row), blk_expert, blk_rows


def kernel(x_prompt, x_sample, cache_attn_k, cache_attn_v, state_hgrn, c, c_ctx, ada_w, ada_b, norm1_g, norm2_g, final_g, ab_w_in, ab_w_out, na_rpb, sc_conv_w, cd_w_in, cd_w_out, hg_lb_logits, router_w, router_b, moe_w1, moe_b1, moe_w2, moe_b2):
    x_all = (x_prompt.reshape(N_PROMPT, D_MODEL), x_sample.reshape(N_SAMPLE, D_MODEL))
    m16 = jnp.zeros((N_MOD_ROWS, D_MODEL), F32).at[0].set(c_ctx).at[1:1 + DEC_BATCH].set(c)
    mod = _modulation(m16, ada_w, ada_b)

    lb_sm = jax.nn.softmax(hg_lb_logits.astype(F32), axis=0)
    lb_all = jnp.cumsum(lb_sm, axis=0) - lb_sm[0:1]

    rw_t = jnp.swapaxes(router_w.astype(F32), 1, 2)
    rw_hi = rw_t.astype(BF16)
    rw_lo = (rw_t - rw_hi.astype(F32)).astype(BF16)
    rw2 = jnp.stack([rw_hi, rw_lo], axis=1)
    rb_pad = jnp.broadcast_to(router_b.astype(F32)[:, :, None], (DEPTH, N_EXPERTS, LANES))

    new_k = new_v = new_s = None
    for l in range(DEPTH):
        j = l // 2
        if l % 2 == 0:
            p_all = _pre_project(x_all, mod[l], norm1_g[l], ab_w_in[j].astype(BF16))
            att_p = _context_attention(p_all)
            ck = cache_attn_k[:, j].transpose(0, 2, 1, 3).reshape(DEC_BATCH, PAST_LEN, NA_W)
            cv = cache_attn_v[:, j].transpose(0, 2, 1, 3).reshape(DEC_BATCH, PAST_LEN, NA_W)
            att_s = _neighbourhood_attention(p_all, ck, cv, _natten_bias(na_rpb[j]))
            kp = p_all[:N_PROMPT, NA_W:2 * NA_W].reshape(BATCH, SEQ, NA_HEADS, NA_HEAD_DIM).transpose(0, 2, 1, 3)
            vp = p_all[:N_PROMPT, 2 * NA_W:3 * NA_W].reshape(BATCH, SEQ, NA_HEADS, NA_HEAD_DIM).transpose(0, 2, 1, 3)
            new_k, new_v = kp[:, None], vp[:, None]
            outs = _post_mix((att_p, att_s), p_all, sc_conv_w[j], x_all, mod[l], norm2_g[l],
                             ab_w_out[j].astype(BF16), rw2[l], rb_pad[l], with_conv=True)
        else:
            p_all = _pre_project(x_all, mod[l], norm1_g[l], cd_w_in[j].astype(BF16))
            fn_p = _fourier_mix(p_all, 0, BATCH, SEQ)
            fn_s = _fourier_mix(p_all, N_PROMPT, DEC_BATCH, DEC_SEQ)
            s0_p = jnp.zeros((BATCH, 2, HG_HEADS, HG_DK, HG_DK), F32)
            s0_s = jnp.swapaxes(state_hgrn[:, j].astype(F32), -1, -2)
            hg_p, sfin_p = _hgrn2_bidir(p_all, lb_all[l], s0_p, 0, BATCH, SEQ)
            hg_s, _ = _hgrn2_bidir(p_all, lb_all[l], s0_s, N_PROMPT, DEC_BATCH, DEC_SEQ)
            new_s = jnp.swapaxes(sfin_p, -1, -2)[:, None].astype(x_prompt.dtype)
            outs = _post_mix((fn_p, fn_s), (hg_p, hg_s), None, x_all, mod[l], norm2_g[l],
                             cd_w_out[j].astype(BF16), rw2[l], rb_pad[l], with_conv=False)
        x_mid, h2, route_rows, route_tok, tile_cnt = outs
        tables, blk_expert, blk_rows = _moe_layout(tile_cnt)
        xs = _moe_dispatch(tables, h2, route_rows)
        ys = _moe_experts(l, blk_expert, blk_rows, xs, moe_w1, moe_b1[:, :, None, :], moe_w2, moe_b2[:, :, None, :])
        x_all = _moe_combine(tables, ys, route_tok, x_mid, mod[l], final_g, final=(l == DEPTH - 1))

    y_prompt, y_sample = x_all
    return (y_prompt.reshape(BATCH, SEQ, D_MODEL), y_sample.reshape(DEC_BATCH, DEC_SEQ, D_MODEL),
            new_k, new_v, new_s)
```

```python
import functools

import numpy as np
import jax
import jax.numpy as jnp
from jax import lax
from jax.experimental import pallas as pl
from jax.experimental.pallas import tpu as pltpu

F32 = jnp.float32
BF16 = jnp.bfloat16

D_MODEL = 1024
BATCH = 16
SEQ = 256
DEC_BATCH = 8
DEC_SEQ = 2048
PAST_LEN = 512
DEPTH = 2
GRID_W = 64
GRID_ROWS = DEC_SEQ // GRID_W
NA_HEADS = 8
NA_HEAD_DIM = 64
NA_W = NA_HEADS * NA_HEAD_DIM
SC_W = D_MODEL - NA_W
NA_SCALE = NA_HEAD_DIM ** -0.5
MAX_WR = 8
WIN_W = 16
FN_W = 512
FN_GROUPS = 4
FN_GROUP_W = 128
HG_W = 512
HG_HEADS = 4
HG_DK = 128
HG_CHUNK = 64
HG_SUB = 16
N_EXPERTS = 32
TOP_K = 4
D_FF = D_MODEL
SWIGLU_LIMIT = 7.0
SWIGLU_ALPHA = 1.702
EPS = 1e-6
NEG_INF = -1e30

N_PROMPT = BATCH * SEQ
N_SAMPLE = DEC_BATCH * DEC_SEQ
N_TOK = N_PROMPT + N_SAMPLE
TM = 256
N_TILES = N_TOK // TM
PROMPT_TILES = N_PROMPT // TM
TILES_PER_DEC_SEQ = DEC_SEQ // TM
N_MOD_ROWS = 16
LANES = 128
SUBLANES = 8

ROUTE_ROWS = 16
POST_TILES = 2
MOE_BLK = 512
N_ASG = N_TOK * TOP_K
N_BLK = N_ASG // MOE_BLK + N_EXPERTS
N_SLOT = N_BLK * MOE_BLK
VMEM_LIMIT = 56 * 1024 * 1024


def _mod_row(i):
    return jnp.where(i < PROMPT_TILES, 0, 1 + (i - PROMPT_TILES) // TILES_PER_DEC_SEQ)


def _sigmoid_pair(z):
    e = jnp.exp(-jnp.abs(z))
    r = 1.0 / (1.0 + e)
    er = e * r
    pos = z >= 0
    return jnp.where(pos, r, er), jnp.where(pos, er, r)


def _silu(x):
    return x * _sigmoid_pair(x)[0]


def _rms(x):
    return x * lax.rsqrt(jnp.mean(x * x, axis=-1, keepdims=True) + EPS)


def _dot(a, b):
    return jnp.dot(a, b, preferred_element_type=F32)


def _dot_nt(a, b):
    return lax.dot_general(a, b, (((1,), (1,)), ((), ())), preferred_element_type=F32)


def _dot_tn(a, b):
    return lax.dot_general(a, b, (((0,), (0,)), ((), ())), preferred_element_type=F32)


def _split3(x):
    hi = x.astype(BF16)
    r1 = x - hi.astype(F32)
    mid = r1.astype(BF16)
    lo = (r1 - mid.astype(F32)).astype(BF16)
    return hi, mid, lo


MOD_TN = 1024


def _mod_kernel(m_ref, w_ref, b_ref, o_ref):
    a = _silu(m_ref[...]).astype(BF16)
    o_ref[...] = _dot(a, w_ref[...].astype(BF16)) + b_ref[...]


def _modulation(m16, ada_w, ada_b):
    n_out = 6 * D_MODEL
    out = pl.pallas_call(
        _mod_kernel,
        out_shape=jax.ShapeDtypeStruct((DEPTH, N_MOD_ROWS, n_out), F32),
        grid=(DEPTH, n_out // MOD_TN),
        in_specs=[
            pl.BlockSpec((N_MOD_ROWS, D_MODEL), lambda l, j: (0, 0)),
            pl.BlockSpec((None, D_MODEL, MOD_TN), lambda l, j: (l, 0, j)),
            pl.BlockSpec((None, 1, MOD_TN), lambda l, j: (l, 0, j)),
        ],
        out_specs=pl.BlockSpec((None, N_MOD_ROWS, MOD_TN), lambda l, j: (l, 0, j)),
        compiler_params=pltpu.CompilerParams(
            dimension_semantics=("arbitrary", "arbitrary"), vmem_limit_bytes=VMEM_LIMIT),
    )(m16, ada_w, ada_b.reshape(DEPTH, 1, n_out))
    return out.reshape(DEPTH, N_MOD_ROWS, 6, D_MODEL)


def _by_group(i, refs):
    if len(refs) == 1:
        return refs[0][...]
    return jnp.where(i < PROMPT_TILES, refs[0][...], refs[1][...])


def _group_specs(width, tiles=1):
    n_p = PROMPT_TILES // tiles
    return [pl.BlockSpec((tiles * TM, width), lambda i: (jnp.minimum(i, n_p - 1), 0)),
            pl.BlockSpec((tiles * TM, width), lambda i: (jnp.maximum(i - n_p, 0), 0))]


def _pre_kernel(*refs):
    x_refs, (mod_ref, g_ref, w_ref, o_ref) = refs[:-4], refs[-4:]
    y = _rms(_by_group(pl.program_id(0), x_refs)) * g_ref[...]
    h = y * (1.0 + mod_ref[1:2, :]) + mod_ref[0:1, :]
    o_ref[...] = _dot(h.astype(BF16), w_ref[...])


def _pre_project(x_src, mod_l, g, w_bf):
    n_out = w_bf.shape[1]
    if isinstance(x_src, tuple):
        x_specs, x_args = _group_specs(D_MODEL), list(x_src)
    else:
        x_specs, x_args = [pl.BlockSpec((TM, D_MODEL), lambda i: (i, 0))], [x_src]
    return pl.pallas_call(
        _pre_kernel,
        out_shape=jax.ShapeDtypeStruct((N_TOK, n_out), F32),
        grid=(N_TILES,),
        in_specs=x_specs + [
            pl.BlockSpec((None, 6, D_MODEL), lambda i: (_mod_row(i), 0, 0)),
            pl.BlockSpec((1, D_MODEL), lambda i: (0, 0)),
            pl.BlockSpec((D_MODEL, n_out), lambda i: (0, 0)),
        ],
        out_specs=pl.BlockSpec((TM, n_out), lambda i: (i, 0)),
        compiler_params=pltpu.CompilerParams(
            dimension_semantics=("arbitrary",), vmem_limit_bytes=VMEM_LIMIT),
    )(*x_args, mod_l, g.reshape(1, D_MODEL), w_bf)


def _softmax_pv(parts):
    m = None
    for s, _ in parts:
        mi = jnp.max(s, axis=-1, keepdims=True)
        m = mi if m is None else jnp.maximum(m, mi)
    acc = None
    den = None
    for s, v in parts:
        p = jnp.exp(s - m)
        d = jnp.sum(p, axis=-1, keepdims=True)
        o = _dot(p.astype(BF16), v)
        acc = o if acc is None else acc + o
        den = d if den is None else den + d
    return acc / den


def _ctx_attn_kernel(q_ref, k_ref, v_ref, o_ref):
    q = q_ref[...]
    kb = k_ref[...].astype(BF16)
    vb = v_ref[...].astype(BF16)
    lane = lax.broadcasted_iota(jnp.int32, q.shape, 1)
    out = jnp.zeros(q.shape, F32)
    for h in range(2):
        hm = (lane >= NA_HEAD_DIM) if h == 1 else (lane < NA_HEAD_DIM)
        qm = jnp.where(hm, q, 0.0).astype(BF16)
        s = _dot_nt(qm, kb) * NA_SCALE
        out = jnp.where(hm, _softmax_pv([(s, vb)]), out)
    o_ref[...] = out


def _context_attention(p_all):
    n_hp = NA_HEADS // 2
    return pl.pallas_call(
        _ctx_attn_kernel,
        out_shape=jax.ShapeDtypeStruct((N_PROMPT, NA_W), F32),
        grid=(BATCH, n_hp),
        in_specs=[
            pl.BlockSpec((SEQ, LANES), lambda b, hp: (b, hp)),
            pl.BlockSpec((SEQ, LANES), lambda b, hp: (b, n_hp + hp)),
            pl.BlockSpec((SEQ, LANES), lambda b, hp: (b, 2 * n_hp + hp)),
        ],
        out_specs=pl.BlockSpec((SEQ, LANES), lambda b, hp: (b, hp)),
        compiler_params=pltpu.CompilerParams(
            dimension_semantics=("arbitrary", "arbitrary"), vmem_limit_bytes=VMEM_LIMIT),
    )(p_all, p_all, p_all)


N_BIAS_VAR = MAX_WR
WIN_KEYS = MAX_WR * GRID_W
NA_ROWS = 4


def _natten_kernel(q_ref, k_ref, v_ref, ck_ref, cv_ref, bias_ref, o_ref):
    ckb = ck_ref[...].astype(BF16)
    cvb = cv_ref[...].astype(BF16)
    lane = lax.broadcasted_iota(jnp.int32, (GRID_W, LANES), 1)

    first = lane < NA_HEAD_DIM
    blk = 2 * GRID_W

    def rows(it, carry):
        qs, kws, vws, biases = [], [], [], []
        for d in range(NA_ROWS):
            r = it * NA_ROWS + d
            r0 = jnp.clip(r - MAX_WR // 2, 0, GRID_ROWS - MAX_WR)
            q = q_ref[pl.ds(pl.multiple_of(r * GRID_W, GRID_W), GRID_W), :]
            qs += [jnp.where(first, q, 0.0), jnp.where(first, 0.0, q)]
            kws.append(k_ref[pl.ds(pl.multiple_of(r0 * GRID_W, GRID_W), WIN_KEYS), :].astype(BF16))
            vws.append(v_ref[pl.ds(pl.multiple_of(r0 * GRID_W, GRID_W), WIN_KEYS), :].astype(BF16))
            biases.append(bias_ref[r0 - r + (MAX_WR - 1)])
        q_all = jnp.concatenate(qs, axis=0).astype(BF16)
        s_ctx = _dot_nt(q_all, ckb) * NA_SCALE
        s_loc = jnp.concatenate([_dot_nt(q_all[d * blk:(d + 1) * blk], kws[d]) * NA_SCALE + biases[d]
                                 for d in range(NA_ROWS)], axis=0)
        m = jnp.maximum(jnp.max(s_loc, axis=-1, keepdims=True), jnp.max(s_ctx, axis=-1, keepdims=True))
        p_loc = jnp.exp(s_loc - m)
        p_ctx = jnp.exp(s_ctx - m)
        den = jnp.sum(p_loc, axis=-1, keepdims=True) + jnp.sum(p_ctx, axis=-1, keepdims=True)
        p_loc = p_loc.astype(BF16)
        o_loc = jnp.concatenate([_dot(p_loc[d * blk:(d + 1) * blk], vws[d]) for d in range(NA_ROWS)], axis=0)
        o = (o_loc + _dot(p_ctx.astype(BF16), cvb)) / den
        for d in range(NA_ROWS):
            r = it * NA_ROWS + d
            o_ref[pl.ds(pl.multiple_of(r * GRID_W, GRID_W), GRID_W), :] = jnp.where(
                first, o[d * blk:d * blk + GRID_W], o[d * blk + GRID_W:(d + 1) * blk])
        return carry

    lax.fori_loop(0, GRID_ROWS // NA_ROWS, rows, 0, unroll=2)


def _natten_bias(rpb):
    qc = np.arange(GRID_W)[:, None]
    kc = np.arange(GRID_W)[None, :]
    ws = np.clip(qc - WIN_W // 2, 0, GRID_W - WIN_W)
    valid = (kc >= ws) & (kc < ws + WIN_W)
    dc = np.clip(kc - qc + WIN_W - 1, 0, 2 * WIN_W - 2)
    t = rpb.astype(F32)[:, :, dc]
    t = jnp.where(jnp.asarray(valid)[None, None], t, NEG_INF)
    dr = np.arange(N_BIAS_VAR)[:, None] + np.arange(MAX_WR)[None, :]
    slab = t[:, dr]
    slab = slab.transpose(0, 1, 3, 2, 4).reshape(NA_HEADS // 2, 2, N_BIAS_VAR, GRID_W, WIN_KEYS)
    return slab.transpose(0, 2, 1, 3, 4).reshape(NA_HEADS // 2, N_BIAS_VAR, 2 * GRID_W, WIN_KEYS)


def _neighbourhood_attention(p_all, ck, cv, bias):
    n_hp = NA_HEADS // 2
    off = N_PROMPT // DEC_SEQ
    return pl.pallas_call(
        _natten_kernel,
        out_shape=jax.ShapeDtypeStruct((N_SAMPLE, NA_W), F32),
        grid=(DEC_BATCH, n_hp),
        in_specs=[
            pl.BlockSpec((DEC_SEQ, LANES), lambda b, hp: (off + b, hp)),
            pl.BlockSpec((DEC_SEQ, LANES), lambda b, hp: (off + b, n_hp + hp)),
            pl.BlockSpec((DEC_SEQ, LANES), lambda b, hp: (off + b, 2 * n_hp + hp)),
            pl.BlockSpec((None, PAST_LEN, LANES), lambda b, hp: (b, 0, hp)),
            pl.BlockSpec((None, PAST_LEN, LANES), lambda b, hp: (b, 0, hp)),
            pl.BlockSpec((None, N_BIAS_VAR, 2 * GRID_W, WIN_KEYS), lambda b, hp: (hp, 0, 0, 0)),
        ],
        out_specs=pl.BlockSpec((DEC_SEQ, LANES), lambda b, hp: (b, hp)),
        compiler_params=pltpu.CompilerParams(
            dimension_semantics=("arbitrary", "arbitrary"), vmem_limit_bytes=VMEM_LIMIT),
    )(p_all, p_all, p_all, ck, cv, bias)


def _dft_tables(t):
    j = np.arange(t, dtype=np.int64)
    ang = 2.0 * np.pi * ((j[:, None] * j[None, :]) % t).astype(np.float64) / t
    return np.cos(ang), np.sin(ang)


def _fourier_kernel(z_ref, lt_ref, rc_ref, rs_ref, o_ref, ab_ref, *, t_len, scale):
    @pl.when(pl.program_id(1) == 0)
    def _():
        z = z_ref[...].astype(BF16)
        for g in range(FN_GROUPS):
            zg = z[:, g * FN_GROUP_W:(g + 1) * FN_GROUP_W]
            ab_ref[0:t_len, g * FN_GROUP_W:(g + 1) * FN_GROUP_W] = _dot(zg, rc_ref[...]).astype(BF16)
            ab_ref[t_len:2 * t_len, g * FN_GROUP_W:(g + 1) * FN_GROUP_W] = _dot(zg, rs_ref[...]).astype(BF16)

    o_ref[...] = _dot(lt_ref[...], ab_ref[...]) * scale


def _fourier_mix(p_all, row_off, n_batch, t_len):
    ct, st = _dft_tables(t_len)
    lt = jnp.asarray(np.concatenate([ct, -st], axis=1), dtype=F32).astype(BF16)
    cc, sc = _dft_tables(FN_GROUP_W)
    rc = jnp.asarray(cc, dtype=F32).astype(BF16)
    rs = jnp.asarray(sc, dtype=F32).astype(BF16)
    tr = min(t_len, 512)
    n_rt = t_len // tr
    off = row_off // t_len
    kern = functools.partial(_fourier_kernel, t_len=t_len, scale=float((t_len * FN_GROUP_W) ** -0.5))
    return pl.pallas_call(
        kern,
        out_shape=jax.ShapeDtypeStruct((n_batch * t_len, FN_W), F32),
        grid=(n_batch, n_rt),
        in_specs=[
            pl.BlockSpec((t_len, FN_W), lambda b, i: (off + b, 0)),
            pl.BlockSpec((tr, 2 * t_len), lambda b, i: (i, 0)),
            pl.BlockSpec((FN_GROUP_W, FN_GROUP_W), lambda b, i: (0, 0)),
            pl.BlockSpec((FN_GROUP_W, FN_GROUP_W), lambda b, i: (0, 0)),
        ],
        out_specs=pl.BlockSpec((tr, FN_W), lambda b, i: (b * n_rt + i, 0)),
        scratch_shapes=[pltpu.VMEM((2 * t_len, FN_W), BF16)],
        compiler_params=pltpu.CompilerParams(
            dimension_semantics=("arbitrary", "arbitrary"), vmem_limit_bytes=VMEM_LIMIT),
    )(p_all, lt, rc, rs)


N_SUB = HG_CHUNK // HG_SUB
LOG2_E = 1.4426950408889634


def _gla_chunk(q, v, z, lb, st, backward):
    c = HG_CHUNK
    sg, sgn = _sigmoid_pair(z)
    lf = jnp.log(lb + (1.0 - lb) * sg)
    kk = (1.0 - lb) * sgn
    row = lax.broadcasted_iota(jnp.int32, (c, c), 0)
    col = lax.broadcasted_iota(jnp.int32, (c, c), 1)
    tri = ((col >= row) if backward else (col <= row)).astype(BF16)
    hi, mid, lo = _split3(lf)
    b = _dot(tri, hi) + _dot(tri, mid) + _dot(tri, lo)
    o_inter = _dot_nt((q * jnp.exp(b)).astype(BF16), st.astype(BF16))
    b2 = b * LOG2_E
    c2 = b2 - jnp.log2(kk)

    sub_row = lax.broadcasted_iota(jnp.int32, (HG_SUB, c), 0)
    sub_col = lax.broadcasted_iota(jnp.int32, (HG_SUB, c), 1)
    att_rows = []
    for i in range(N_SUB):
        lo_r, hi_r = i * HG_SUB, (i + 1) * HG_SUB
        q_i = q[lo_r:hi_r]
        b_i = b[lo_r:hi_r]
        has_off = (i < N_SUB - 1) if backward else (i > 0)
        if has_off:
            r_i = b[hi_r:hi_r + 1] if backward else b[lo_r - 1:lo_r]
            qs = (q_i * jnp.exp(b_i - r_i)).astype(BF16)
            ks = (kk * jnp.exp(jnp.minimum(r_i - b, 0.0))).astype(BF16)
            a = _dot_nt(qs, ks)
            off_mask = (sub_col >= hi_r) if backward else (sub_col < lo_r)
            a = jnp.where(off_mask, a, 0.0)
        else:
            a = jnp.zeros((HG_SUB, c), F32)
        b2_i = b2[lo_r:hi_r]
        c2_i = c2[lo_r:hi_r]
        for s in range(HG_SUB):
            colsum = jnp.sum(q_i * jnp.exp2(b2_i - c2_i[s:s + 1]), axis=-1, keepdims=True)
            keep = (sub_row <= s) if backward else (sub_row >= s)
            a = jnp.where((sub_col == lo_r + s) & keep, colsum, a)
        att_rows.append(a)
    att = jnp.concatenate(att_rows, axis=0)
    o = o_inter + _dot(att.astype(BF16), v.astype(BF16))

    b_last = b[0:1] if backward else b[c - 1:c]
    kd = (kk * jnp.exp(b_last - b)).astype(BF16)
    st_new = st * jnp.exp(b_last) + _dot_tn(v.astype(BF16), kd)
    return o, st_new


def _hgrn_kernel(q_ref, i_ref, zf_ref, zb_ref, g_ref, lb_ref, s0_ref, o_ref, sfin_ref, of_ref, ob_ref,
                 stf_ref, stb_ref, *, t_len):
    n_chunk = t_len // HG_CHUNK

    def sl(c):
        return pl.ds(pl.multiple_of(c * HG_CHUNK, HG_CHUNK), HG_CHUNK)

    stf_ref[...] = s0_ref[0]
    stb_ref[...] = s0_ref[1]

    def step(j, carry):
        c = n_chunk - 1 - j
        o_f, st_f = _gla_chunk(q_ref[sl(j), :], i_ref[sl(j), :], zf_ref[sl(j), :], lb_ref[0:1, :],
                               stf_ref[...], backward=False)
        o_b, st_b = _gla_chunk(q_ref[sl(c), :], i_ref[sl(c), :], zb_ref[sl(c), :], lb_ref[1:2, :],
                               stb_ref[...], backward=True)
        of_ref[sl(j), :] = o_f
        ob_ref[sl(c), :] = o_b
        stf_ref[...] = st_f
        stb_ref[...] = st_b
        return carry

    lax.fori_loop(0, n_chunk, step, 0, unroll=4)
    sfin_ref[0] = stf_ref[...]
    sfin_ref[1] = stb_ref[...]

    def finish(c, carry):
        tot = of_ref[sl(c), :] + ob_ref[sl(c), :]
        o_ref[sl(c), :] = _rms(tot) * _silu(g_ref[sl(c), :])
        return carry

    lax.fori_loop(0, n_chunk, finish, 0)


def _hgrn2_bidir(p_all, lb, s0_t, row_off, n_batch, t_len):
    off = row_off // t_len
    cb = FN_W // LANES

    def col(j):
        return pl.BlockSpec((t_len, LANES), lambda b, h: (off + b, cb + j * HG_HEADS + h))

    kern = functools.partial(_hgrn_kernel, t_len=t_len)
    state_spec = pl.BlockSpec((None, 2, None, HG_DK, HG_DK), lambda b, h: (b, 0, h, 0, 0))
    return pl.pallas_call(
        kern,
        out_shape=(jax.ShapeDtypeStruct((n_batch * t_len, HG_W), F32),
                   jax.ShapeDtypeStruct((n_batch, 2, HG_HEADS, HG_DK, HG_DK), F32)),
        grid=(n_batch, HG_HEADS),
        in_specs=[col(0), col(1), col(2), col(3), col(4),
                  pl.BlockSpec((2, LANES), lambda b, h: (0, h)),
                  state_spec],
        out_specs=(pl.BlockSpec((t_len, LANES), lambda b, h: (b, h)), state_spec),
        scratch_shapes=[pltpu.VMEM((t_len, LANES), F32), pltpu.VMEM((t_len, LANES), F32),
                        pltpu.VMEM((HG_DK, HG_DK), F32), pltpu.VMEM((HG_DK, HG_DK), F32)],
        compiler_params=pltpu.CompilerParams(
            dimension_semantics=("arbitrary", "arbitrary"), vmem_limit_bytes=VMEM_LIMIT),
    )(p_all, p_all, p_all, p_all, p_all, lb, s0_t)


def _route(h2, rw_ref, rb_ref):
    hi, mid, _ = _split3(h2)
    w_hi = rw_ref[0]
    w_lo = rw_ref[1]
    logits = _dot_nt(w_hi, hi) + _dot_nt(w_hi, mid) + _dot_nt(w_lo, hi) + rb_ref[:, 0:1]
    width = h2.shape[0]
    sub = lax.broadcasted_iota(jnp.int32, (N_EXPERTS, width), 0)
    work = logits
    vals, hots = [], []
    for k in range(TOP_K):
        m = jnp.max(work, axis=0, keepdims=True)
        idx = jnp.min(jnp.where(work == m, sub, N_EXPERTS), axis=0, keepdims=True)
        hot = sub == idx
        work = jnp.where(hot, -jnp.inf, work)
        vals.append(m)
        hots.append(hot)
    es = [jnp.exp(v - vals[0]) for v in vals]
    den = es[0] + es[1] + es[2] + es[3]
    cnt = jnp.zeros((N_EXPERTS, width), F32)
    for hot in hots:
        cnt = cnt + hot.astype(F32)
    row = lax.broadcasted_iota(jnp.int32, (TM, TM), 0)
    col = lax.broadcasted_iota(jnp.int32, (TM, TM), 1)
    earlier = (row < col).astype(BF16)
    erow = lax.broadcasted_iota(jnp.int32, (N_EXPERTS, N_EXPERTS), 0)
    ecol = lax.broadcasted_iota(jnp.int32, (N_EXPERTS, N_EXPERTS), 1)
    lower = (ecol < erow).astype(BF16)
    where_to, tile_cnts = [], []
    for t in range(width // TM):
        cnt_t = cnt[:, t * TM:(t + 1) * TM]
        before = _dot(cnt_t.astype(BF16), earlier)
        tile_cnt = jnp.sum(cnt_t, axis=1, keepdims=True)
        run_off = _dot(lower, jnp.broadcast_to(tile_cnt, (N_EXPERTS, LANES)).astype(BF16))[:, 0:1]
        where_to.append(before + run_off)
        tile_cnts.append(tile_cnt)
    where_to = where_to[0] if len(where_to) == 1 else jnp.concatenate(where_to, axis=1)
    rsub = lax.broadcasted_iota(jnp.int32, (ROUTE_ROWS, width), 0)
    rows = jnp.zeros((ROUTE_ROWS, width), F32)
    for k in range(TOP_K):
        pk = jnp.sum(jnp.where(hots[k], where_to, 0.0), axis=0, keepdims=True)
        rows = jnp.where(rsub == k, pk, rows)
        rows = jnp.where(rsub == TOP_K + k, es[k] / den, rows)
    return rows, tile_cnts


def _post_kernel(*refs, with_conv, n_x):
    refs = list(refs)
    a_refs, refs = refs[:2], refs[2:]
    if with_conv:
        (c_ref, cprev_ref, cnext_ref, cw_ref), refs = refs[:4], refs[4:]
    else:
        b_refs, refs = refs[:2], refs[2:]
    x_refs, refs = refs[:n_x], refs[n_x:]
    mod_ref, g_ref, wo_ref, rw_ref, rb_ref, xo_ref, h2_ref, rows_ref, pg_ref, cnt_ref = refs
    i = pl.program_id(0)
    is_prompt = i < PROMPT_TILES // POST_TILES

    def grp(refs_, rows):
        if len(refs_) == 1:
            return refs_[0][rows, :]
        return jnp.where(is_prompt, refs_[0][rows, :], refs_[1][rows, :])

    if with_conv:
        zs = [c_ref[t * TM:(t + 1) * TM, SC_W:2 * SC_W] * c_ref[t * TM:(t + 1) * TM, 2 * SC_W:3 * SC_W]
              for t in range(POST_TILES)]
        z_before = cprev_ref[SUBLANES - 1:SUBLANES, SC_W:2 * SC_W] * cprev_ref[SUBLANES - 1:SUBLANES, 2 * SC_W:3 * SC_W]
        z_after = cnext_ref[0:1, SC_W:2 * SC_W] * cnext_ref[0:1, 2 * SC_W:3 * SC_W]
    pick = (lax.broadcasted_iota(jnp.int32, (ROUTE_ROWS, LANES), 0)
            == lax.broadcasted_iota(jnp.int32, (ROUTE_ROWS, LANES), 1)).astype(BF16)

    everything = slice(None)
    if with_conv:
        parts = []
        for t in range(POST_TILES):
            j = i * POST_TILES + t - PROMPT_TILES
            seq_start = is_prompt | (j % TILES_PER_DEC_SEQ == 0)
            seq_end = is_prompt | (j % TILES_PER_DEC_SEQ == TILES_PER_DEC_SEQ - 1)
            z = zs[t]
            z_prev = jnp.where(seq_start, 0.0, z_before if t == 0 else zs[t - 1][TM - 1:TM])
            z_next = jnp.where(seq_end, 0.0, z_after if t == POST_TILES - 1 else zs[t + 1][0:1])
            row = lax.broadcasted_iota(jnp.int32, (TM, SC_W), 0)
            zm1 = jnp.where(row == 0, z_prev, pltpu.roll(z, 1, axis=0))
            zp1 = jnp.where(row == TM - 1, z_next, pltpu.roll(z, TM - 1, axis=0))
            parts.append(c_ref[t * TM:(t + 1) * TM, 0:SC_W]
                         * (zm1 * cw_ref[0:1, :] + z * cw_ref[1:2, :] + zp1 * cw_ref[2:3, :]))
        b_val = jnp.concatenate(parts, axis=0)
    else:
        b_val = grp(b_refs, everything)
    a_val = grp(a_refs, everything)
    half = a_val.shape[1]
    mix = _dot(a_val.astype(BF16), wo_ref[0:half, :]) + _dot(b_val.astype(BF16), wo_ref[half:, :])
    x_new = grp(x_refs, everything) + mod_ref[2:3, :] * mix
    xo_ref[...] = x_new
    h2 = _rms(x_new) * g_ref[...] * (1.0 + mod_ref[4:5, :]) + mod_ref[3:4, :]
    h2_ref[...] = h2
    route, tile_cnts = _route(h2, rw_ref, rb_ref)
    r_hi, r_mid, r_lo = _split3(route)
    pg_ref[...] = _dot_tn(r_hi, pick) + _dot_tn(r_mid, pick) + _dot_tn(r_lo, pick)
    for t in range(POST_TILES):
        rows_ref[t * ROUTE_ROWS:(t + 1) * ROUTE_ROWS, :] = route[:, t * TM:(t + 1) * TM]
        cnt_ref[t] = jnp.broadcast_to(tile_cnts[t], (N_EXPERTS, LANES)).astype(jnp.int32)


def _post_mix(a_pair, b_src, conv_w, x_src, mod_l, g2, wo_bf, rw2, rb, with_conv):
    step_rows = POST_TILES * TM
    tile = lambda w: pl.BlockSpec((step_rows, w), lambda i: (i, 0))
    full = lambda shp: pl.BlockSpec(shp, lambda i: tuple(0 for _ in shp))
    rows8 = step_rows // SUBLANES
    n8 = N_TOK // SUBLANES
    in_specs = _group_specs(a_pair[0].shape[1], POST_TILES)
    args = list(a_pair)
    if with_conv:
        cw = 3 * SC_W
        in_specs += [
            pl.BlockSpec((step_rows, cw), lambda i: (i, 1)),
            pl.BlockSpec((SUBLANES, cw), lambda i: (jnp.maximum(i * rows8 - 1, 0), 1)),
            pl.BlockSpec((SUBLANES, cw), lambda i: (jnp.minimum((i + 1) * rows8, n8 - 1), 1)),
            full((3, SC_W)),
        ]
        args += [b_src, b_src, b_src, conv_w]
    else:
        in_specs += _group_specs(b_src[0].shape[1], POST_TILES)
        args += list(b_src)
    if isinstance(x_src, tuple):
        in_specs += _group_specs(D_MODEL, POST_TILES)
        args += list(x_src)
        n_x = 2
    else:
        in_specs += [tile(D_MODEL)]
        args += [x_src]
        n_x = 1
    in_specs += [
        pl.BlockSpec((None, 6, D_MODEL), lambda i: (_mod_row(i * POST_TILES), 0, 0)),
        full((1, D_MODEL)),
        full((D_MODEL, D_MODEL)),
        full((2, N_EXPERTS, D_MODEL)),
        full((N_EXPERTS, LANES)),
    ]
    args += [mod_l, g2.reshape(1, D_MODEL), wo_bf, rw2, rb]
    out_shape = (
        jax.ShapeDtypeStruct((N_TOK, D_MODEL), F32),
        jax.ShapeDtypeStruct((N_TOK, D_MODEL), F32),
        jax.ShapeDtypeStruct((N_TILES * ROUTE_ROWS, TM), F32),
        jax.ShapeDtypeStruct((N_TOK, LANES), F32),
        jax.ShapeDtypeStruct((N_TILES, N_EXPERTS, LANES), jnp.int32),
    )
    out_specs = (tile(D_MODEL), tile(D_MODEL),
                 pl.BlockSpec((POST_TILES * ROUTE_ROWS, TM), lambda i: (i, 0)), tile(LANES),
                 pl.BlockSpec((POST_TILES, N_EXPERTS, LANES), lambda i: (i, 0, 0)))
    return pl.pallas_call(
        functools.partial(_post_kernel, with_conv=with_conv, n_x=n_x),
        out_shape=out_shape,
        grid=(N_TILES // POST_TILES,),
        in_specs=in_specs,
        out_specs=out_specs,
        compiler_params=pltpu.CompilerParams(
            dimension_semantics=("arbitrary",), vmem_limit_bytes=VMEM_LIMIT),
    )(*args)


ROW_SUB = D_MODEL // LANES
TILE_ASG = TM * TOP_K
RUN_BITS = TM.bit_length()
N_TAIL_BLOCKS = N_BLK - N_ASG // MOE_BLK
N_ZERO_BLOCKS = N_EXPERTS + N_TAIL_BLOCKS


def _rows_to_wide(ref, n_rows, lead=None):
    parts = []
    for j in range(ROW_SUB):
        idx = (pl.ds(j, n_rows, stride=ROW_SUB), slice(None))
        parts.append(ref[idx] if lead is None else ref[(lead,) + idx])
    return jnp.concatenate(parts, axis=1)


def _wide_to_rows(ref, val, lead=None):
    n_rows = val.shape[0]
    for j in range(ROW_SUB):
        idx = (pl.ds(j, n_rows, stride=ROW_SUB), slice(None))
        if lead is not None:
            idx = (lead,) + idx
        ref[idx] = val[:, j * LANES:(j + 1) * LANES]


def _row_window(ref, start_row, n_rows, lead=None):
    idx = (pl.ds(pl.multiple_of(start_row * ROW_SUB, ROW_SUB), n_rows * ROW_SUB), slice(None))
    return ref.at[idx] if lead is None else ref.at[(lead,) + idx]


def _run_copies(tile, rdst_ref, roff_ref, rlen_ref, make_copy):
    def per_expert(e, carry):
        r = tile * N_EXPERTS + e
        n = rlen_ref[r]
        src = roff_ref[r]
        dst = rdst_ref[r]
        for bit in reversed(range(RUN_BITS)):
            @pl.when((n & (1 << bit)) != 0)
            def _():
                done = n & ~((2 << bit) - 1)
                make_copy(src + done, dst + done, 1 << bit).start()
        return carry

    lax.fori_loop(0, N_EXPERTS, per_expert, 0)


def _dispatch_kernel(rdst_ref, roff_ref, rlen_ref, zrow_ref, h2_ref, rows_ref, xs_hbm,
                     stage, zbuf, sem, zsem):
    i = pl.program_id(0)
    n = pl.num_programs(0)
    slot = i % 2

    def zero_copy(e):
        return pltpu.make_async_copy(zbuf, _row_window(xs_hbm, zrow_ref[e], MOE_BLK), zsem)

    @pl.when(i == 0)
    def _():
        zbuf[...] = jnp.zeros_like(zbuf)

        def start(e, carry):
            @pl.when(zrow_ref[e] >= 0)
            def _():
                zero_copy(e).start()
            return carry

        def wait(e, carry):
            @pl.when(zrow_ref[e] >= 0)
            def _():
                zero_copy(e).wait()
            return carry

        lax.fori_loop(0, N_ZERO_BLOCKS, start, 0)
        lax.fori_loop(0, N_ZERO_BLOCKS, wait, 0)

    def wait_tile(slot_):
        pltpu.make_async_copy(stage.at[slot_], _row_window(xs_hbm, 0, TILE_ASG), sem.at[slot_]).wait()

    @pl.when(i >= 2)
    def _():
        wait_tile(slot)

    pos = rows_ref[0:TOP_K, :].astype(jnp.int32)
    sub = lax.broadcasted_iota(jnp.int32, (TILE_ASG, TM), 0)
    hit = sub == pos[0:1, :]
    for k in range(1, TOP_K):
        hit = hit | (sub == pos[k:k + 1, :])
    sorted_rows = _dot(hit.astype(BF16), h2_ref[...].astype(BF16))
    _wide_to_rows(stage, sorted_rows, lead=slot)

    _run_copies(i, rdst_ref, roff_ref, rlen_ref,
                lambda s, d, nr: pltpu.make_async_copy(_row_window(stage, s, nr, lead=slot),
                                                       _row_window(xs_hbm, d, nr), sem.at[slot]))

    @pl.when(i == n - 1)
    def _():
        wait_tile(slot)
        wait_tile(1 - slot)


def _moe_dispatch(tables, h2, route_rows):
    rdst, roff, rlen, zrow = tables
    grid_spec = pltpu.PrefetchScalarGridSpec(
        num_scalar_prefetch=4,
        grid=(N_TILES,),
        in_specs=[
            pl.BlockSpec((TM, D_MODEL), lambda i, *_: (i, 0)),
            pl.BlockSpec((ROUTE_ROWS, TM), lambda i, *_: (i, 0)),
        ],
        out_specs=pl.BlockSpec(memory_space=pl.ANY),
        scratch_shapes=[
            pltpu.VMEM((2, TILE_ASG * ROW_SUB, LANES), F32),
            pltpu.VMEM((MOE_BLK * ROW_SUB, LANES), F32),
            pltpu.SemaphoreType.DMA((2,)),
            pltpu.SemaphoreType.DMA(()),
        ],
    )
    return pl.pallas_call(
        _dispatch_kernel,
        out_shape=jax.ShapeDtypeStruct((N_SLOT * ROW_SUB, LANES), F32),
        grid_spec=grid_spec,
        compiler_params=pltpu.CompilerParams(
            dimension_semantics=("arbitrary",), vmem_limit_bytes=VMEM_LIMIT, has_side_effects=True),
    )(rdst, roff, rlen, zrow, h2, route_rows)


def _ffn_kernel(bexp_ref, brows_ref, xs_ref, w1_ref, b1_ref, w2_ref, b2_ref, ys_ref, w1b, w2b):
    b = pl.program_id(0)
    rows = brows_ref[b]
    half = MOE_BLK // 2

    def mlp(n_rows):
        x = _rows_to_wide(xs_ref, n_rows).astype(BF16)
        u = _dot(x, w1b[...]) + b1_ref[...]
        glu = jnp.minimum(u[:, :D_FF], SWIGLU_LIMIT)
        lin = jnp.clip(u[:, D_FF:], -SWIGLU_LIMIT, SWIGLU_LIMIT)
        act = glu * _sigmoid_pair(SWIGLU_ALPHA * glu)[0] * (lin + 1.0)
        _wide_to_rows(ys_ref, _dot(act.astype(BF16), w2b[...]) + b2_ref[...])

    def zero_from(first_row):
        n = (MOE_BLK - first_row) * ROW_SUB
        ys_ref[pl.ds(first_row * ROW_SUB, n), :] = jnp.zeros((n, LANES), F32)

    @pl.when(rows > 0)
    def _():
        prev = jnp.maximum(b - 1, 0)

        @pl.when((b == 0) | (bexp_ref[b] != bexp_ref[prev]))
        def _():
            w1b[...] = w1_ref[...].astype(BF16)
            w2b[...] = w2_ref[...].astype(BF16)

    @pl.when(rows > half)
    def _():
        mlp(MOE_BLK)

    @pl.when((rows > 0) & (rows <= half))
    def _():
        mlp(half)
        zero_from(half)

    @pl.when(rows == 0)
    def _():
        zero_from(0)


def _moe_experts(layer, blk_expert, blk_rows, xs, w1, b1, w2, b2):
    def blk(b, be, br):
        return (b, 0)

    def wsel(b, be, br):
        return (layer, be[b], 0, 0)

    grid_spec = pltpu.PrefetchScalarGridSpec(
        num_scalar_prefetch=2,
        grid=(N_BLK,),
        in_specs=[
            pl.BlockSpec((MOE_BLK * ROW_SUB, LANES), blk),
            pl.BlockSpec((None, None, D_MODEL, 2 * D_FF), wsel),
            pl.BlockSpec((None, None, 1, 2 * D_FF), wsel),
            pl.BlockSpec((None, None, D_FF, D_MODEL), wsel),
            pl.BlockSpec((None, None, 1, D_MODEL), wsel),
        ],
        out_specs=pl.BlockSpec((MOE_BLK * ROW_SUB, LANES), blk),
        scratch_shapes=[
            pltpu.VMEM((D_MODEL, 2 * D_FF), BF16),
            pltpu.VMEM((D_FF, D_MODEL), BF16),
        ],
    )
    return pl.pallas_call(
        _ffn_kernel,
        out_shape=jax.ShapeDtypeStruct((N_SLOT * ROW_SUB, LANES), F32),
        grid_spec=grid_spec,
        compiler_params=pltpu.CompilerParams(
            dimension_semantics=("arbitrary",), vmem_limit_bytes=VMEM_LIMIT),
    )(blk_expert, blk_rows, xs, w1, b1, w2, b2)


def _combine_kernel(rdst_ref, roff_ref, rlen_ref, ys_hbm, pg_ref, x_ref, mod_ref, g_ref, *rest,
                    final):
    if final:
        yp_ref, ysm_ref, stage, sem = rest
    else:
        o_ref, stage, sem = rest
    i = pl.program_id(0)
    n = pl.num_programs(0)
    slot = i % 2

    def fetch(tile, slot_):
        _run_copies(tile, rdst_ref, roff_ref, rlen_ref,
                    lambda s, d, nr: pltpu.make_async_copy(_row_window(ys_hbm, d, nr),
                                                           _row_window(stage, s, nr, lead=slot_), sem.at[slot_]))

    @pl.when(i == 0)
    def _():
        fetch(0, 0)

    pltpu.make_async_copy(_row_window(ys_hbm, 0, TILE_ASG), stage.at[slot], sem.at[slot]).wait()

    @pl.when(i + 1 < n)
    def _():
        fetch(i + 1, 1 - slot)

    pg = pg_ref[...]
    pos = pg[:, 0:TOP_K].astype(jnp.int32)
    gates = pg[:, TOP_K:2 * TOP_K]
    lane = lax.broadcasted_iota(jnp.int32, (TM, TILE_ASG), 1)
    gmat = jnp.zeros((TM, TILE_ASG), F32)
    for k in range(TOP_K):
        gmat = jnp.where(lane == pos[:, k:k + 1], gates[:, k:k + 1], gmat)
    g_hi = gmat.astype(BF16)
    g_lo = (gmat - g_hi.astype(F32)).astype(BF16)
    cols = []
    for jj in range(ROW_SUB // 2):
        s = jnp.concatenate([stage[slot, pl.ds(2 * jj, TILE_ASG, stride=ROW_SUB), :],
                             stage[slot, pl.ds(2 * jj + 1, TILE_ASG, stride=ROW_SUB), :]], axis=1)
        s_bf = s.astype(BF16)
        cols.append(_dot(g_hi, s_bf) + _dot(g_lo, s_bf))
    moe = jnp.concatenate(cols, axis=1)
    x = x_ref[...] + mod_ref[5:6, :] * moe
    if final:
        x = _rms(x) * g_ref[...]

        @pl.when(i < PROMPT_TILES)
        def _():
            yp_ref[...] = x

        @pl.when(i >= PROMPT_TILES)
        def _():
            ysm_ref[...] = x
    else:
        o_ref[...] = x


def _moe_combine(tables, ys, route_tok, x_all, mod_l, final_g, final):
    rdst, roff, rlen, _ = tables
    tile = pl.BlockSpec((TM, D_MODEL), lambda i, *_: (i, 0))
    if final:
        out_shape = (jax.ShapeDtypeStruct((N_PROMPT, D_MODEL), F32), jax.ShapeDtypeStruct((N_SAMPLE, D_MODEL), F32))
        out_specs = (pl.BlockSpec((TM, D_MODEL), lambda i, *_: (jnp.minimum(i, PROMPT_TILES - 1), 0)),
                     pl.BlockSpec((TM, D_MODEL), lambda i, *_: (jnp.maximum(i - PROMPT_TILES, 0), 0)))
    else:
        out_shape = jax.ShapeDtypeStruct((N_TOK, D_MODEL), F32)
        out_specs = tile
    grid_spec = pltpu.PrefetchScalarGridSpec(
        num_scalar_prefetch=3,
        grid=(N_TILES,),
        in_specs=[
            pl.BlockSpec(memory_space=pl.ANY),
            pl.BlockSpec((TM, LANES), lambda i, *_: (i, 0)),
            tile,
            pl.BlockSpec((None, 6, D_MODEL), lambda i, *_: (_mod_row(i), 0, 0)),
            pl.BlockSpec((1, D_MODEL), lambda i, *_: (0, 0)),
        ],
        out_specs=out_specs,
        scratch_shapes=[
            pltpu.VMEM((2, TILE_ASG * ROW_SUB, LANES), F32),
            pltpu.SemaphoreType.DMA((2,)),
        ],
    )
    return pl.pallas_call(
        functools.partial(_combine_kernel, final=final),
        out_shape=out_shape,
        grid_spec=grid_spec,
        compiler_params=pltpu.CompilerParams(
            dimension_semantics=("arbitrary",), vmem_limit_bytes=VMEM_LIMIT),
    )(rdst, roff, rlen, ys, route_tok, x_all, mod_l, final_g.reshape(1, D_MODEL))


def _moe_layout(tile_cnt):
    cnt = tile_cnt[:, :, 0]
    total = jnp.sum(cnt, axis=0)
    padded = (total + MOE_BLK - 1) // MOE_BLK * MOE_BLK
    pad_end = jnp.cumsum(padded)
    pad_start = pad_end - padded
    before = jnp.cumsum(cnt, axis=0) - cnt
    run_dst = (pad_start[None, :] + before).reshape(-1).astype(jnp.int32)
    run_off = (jnp.cumsum(cnt, axis=1) - cnt).reshape(-1).astype(jnp.int32)
    run_len = cnt.reshape(-1).astype(jnp.int32)
    last_blk = jnp.where((padded > 0) & (total < padded), pad_end - MOE_BLK, -1)
    tail = pad_end[-1] + jnp.arange(N_TAIL_BLOCKS) * MOE_BLK
    zero_row = jnp.concatenate([last_blk, jnp.where(tail < N_SLOT, tail, -1)]).astype(jnp.int32)
    blk_start = jnp.arange(N_BLK, dtype=jnp.int32) * MOE_BLK
    blk_expert = jnp.minimum(jnp.sum((blk_start[:, None] >= pad_end[None, :]).astype(jnp.int32), axis=1),
                             N_EXPERTS - 1).astype(jnp.int32)
    blk_rows = jnp.clip((pad_start + total)[blk_expert] - blk_start, 0, MOE_BLK)
    blk_rows = jnp.where(blk_start < pad_end[-1], blk_rows, 0).astype(jnp.int32)
    return (run_dst, run_off, run_len, zero_row), blk_expert, blk_rows


def kernel(x_prompt, x_sample, cache_attn_k, cache_attn_v, state_hgrn, c, c_ctx, ada_w, ada_b, norm1_g, norm2_g, final_g, ab_w_in, ab_w_out, na_rpb, sc_conv_w, cd_w_in, cd_w_out, hg_lb_logits, router_w, router_b, moe_w1, moe_b1, moe_w2, moe_b2):
    x_all = (x_prompt.reshape(N_PROMPT, D_MODEL), x_sample.reshape(N_SAMPLE, D_MODEL))
    m16 = jnp.zeros((N_MOD_ROWS, D_MODEL), F32).at[0].set(c_ctx).at[1:1 + DEC_BATCH].set(c)
    mod = _modulation(m16, ada_w, ada_b)

    lb_sm = jax.nn.softmax(hg_lb_logits.astype(F32), axis=0)
    lb_all = jnp.cumsum(lb_sm, axis=0) - lb_sm[0:1]

    rw_t = jnp.swapaxes(router_w.astype(F32), 1, 2)
    rw_hi = rw_t.astype(BF16)
    rw_lo = (rw_t - rw_hi.astype(F32)).astype(BF16)
    rw2 = jnp.stack([rw_hi, rw_lo], axis=1)
    rb_pad = jnp.broadcast_to(router_b.astype(F32)[:, :, None], (DEPTH, N_EXPERTS, LANES))

    new_k = new_v = new_s = None
    for l in range(DEPTH):
        j = l // 2
        if l % 2 == 0:
            p_all = _pre_project(x_all, mod[l], norm1_g[l], ab_w_in[j].astype(BF16))
            att_p = _context_attention(p_all)
            ck = cache_attn_k[:, j].transpose(0, 2, 1, 3).reshape(DEC_BATCH, PAST_LEN, NA_W)
            cv = cache_attn_v[:, j].transpose(0, 2, 1, 3).reshape(DEC_BATCH, PAST_LEN, NA_W)
            att_s = _neighbourhood_attention(p_all, ck, cv, _natten_bias(na_rpb[j]))
            kp = p_all[:N_PROMPT, NA_W:2 * NA_W].reshape(BATCH, SEQ, NA_HEADS, NA_HEAD_DIM).transpose(0, 2, 1, 3)
            vp = p_all[:N_PROMPT, 2 * NA_W:3 * NA_W].reshape(BATCH, SEQ, NA_HEADS, NA_HEAD_DIM).transpose(0, 2, 1, 3)
            new_k, new_v = kp[:, None], vp[:, None]
            outs = _post_mix((att_p, att_s), p_all, sc_conv_w[j], x_all, mod[l], norm2_g[l],
                             ab_w_out[j].astype(BF16), rw2[l], rb_pad[l], with_conv=True)
        else:
            p_all = _pre_project(x_all, mod[l], norm1_g[l], cd_w_in[j].astype(BF16))
            fn_p = _fourier_mix(p_all, 0, BATCH, SEQ)
            fn_s = _fourier_mix(p_all, N_PROMPT, DEC_BATCH, DEC_SEQ)
            s0_p = jnp.zeros((BATCH, 2, HG_HEADS, HG_DK, HG_DK), F32)
            s0_s = jnp.swapaxes(state_hgrn[:, j].astype(F32), -1, -2)
            hg_p, sfin_p = _hgrn2_bidir(p_all, lb_all[l], s0_p, 0, BATCH, SEQ)
            hg_s, _ = _hgrn2_bidir(p_all, lb_all[l], s0_s, N_PROMPT, DEC_BATCH, DEC_SEQ)
            new_s = jnp.swapaxes(sfin_p, -1, -2)[:, None].astype(x_prompt.dtype)
            outs = _post_mix((fn_p, fn_s), (hg_p, hg_s), None, x_all, mod[l], norm2_g[l],
                             cd_w_out[j].astype(BF16), rw2[l], rb_pad[l], with_conv=False)
        x_mid, h2, route_rows, route_tok, tile_cnt = outs
        tables, blk_expert, blk_rows = _moe_layout(tile_cnt)
        xs = _moe_dispatch(tables, h2, route_rows)
        ys = _moe_experts(l, blk_expert, blk_rows, xs, moe_w1, moe_b1[:, :, None, :], moe_w2, moe_b2[:, :, None, :])
        x_all = _moe_combine(tables, ys, route_tok, x_mid, mod[l], final_g, final=(l == DEPTH - 1))

    y_prompt, y_sample = x_all
    return (y_prompt.reshape(BATCH, SEQ, D_MODEL), y_sample.reshape(DEC_BATCH, DEC_SEQ, D_MODEL),
            new_k, new_v, new_s)
```
